```python
import math
import jax
import jax.numpy as jnp
from jax import lax
import numpy as np

D_MODEL = 4096
BATCH = 4
SEQ = 2048
DEPTH = 4
DEC_BATCH = 8
DEC_SEQ = 1
PAST_LEN = 8192
PAGE_SIZE = 128

N_MIXERS = 4
HEAD_DIM = 128
QBLK = 128
FOX_HEADS = D_MODEL // HEAD_DIM
HG_HEADS = D_MODEL // 128
HG_DK = D_MODEL // HG_HEADS
HG_DV = D_MODEL // HG_HEADS
HG_CHUNK = 16
S5_CH = 16
S5_GROUPS = D_MODEL // S5_CH
S5_STATE = 64
S5_CHUNK = 128
NSA_HEADS = D_MODEL // HEAD_DIM
NSA_KVH = 4
NSA_GROUP = NSA_HEADS // NSA_KVH
NSA_KVW = NSA_KVH * HEAD_DIM
CMP_BLK = 64
SEL_BLK = 64
N_SEL = 16
WINDOW = 512
SLC_QBLK = 32
N_BUCKETS = 32
MAX_DIST = 128
D_FF = 11008
N_EXPERTS = 8
TOP_K = 2
D_FF_EXPERT = 11008
MOE_BLK = 128
N_FOX = (DEPTH + 3) // 4
N_HG = (DEPTH + 2) // 4
N_S5 = (DEPTH + 1) // 4
N_NSA = DEPTH // 4
N_DENSE = (DEPTH + 1) // 2
N_MOE = DEPTH // 2
EPS = 1e-6
NEG = -1e30
FORCE = 1e9
F32 = jnp.float32

kernel_name = 'hybrid_fox_hgrn2_s5_nsa_decode_step'


def _rms(x, g):
    xf = x.astype(F32)
    y = xf * lax.rsqrt(jnp.mean(xf * xf, axis=-1, keepdims=True) + EPS)
    return (y * g.astype(F32)).astype(x.dtype)


def _gather_pages(pool, page_table):
    g = pool[page_table]
    return g.reshape((g.shape[0], g.shape[1] * g.shape[2]) + g.shape[3:])


def _t5_bucket(dist):
    n = jnp.maximum(dist, 0)
    exact = N_BUCKETS // 2
    lg = jnp.log(jnp.maximum(n, exact).astype(F32) / exact) / math.log(MAX_DIST / exact)
    large = jnp.minimum(exact + (lg * (N_BUCKETS - exact)).astype(jnp.int32), N_BUCKETS - 1)
    return jnp.where(n < exact, n, large)


def _bias_heads(table, dist):
    b = table[_t5_bucket(dist)].astype(F32).reshape(dist.shape + (NSA_KVH, NSA_GROUP))
    return jnp.transpose(b, (2, 3, 0, 1))


def _bias_group(table, dist):
    tg = jnp.transpose(table.reshape(N_BUCKETS, NSA_KVH, NSA_GROUP), (1, 0, 2))
    hi = jnp.arange(NSA_KVH)[None, :, None, None]
    return jnp.moveaxis(tg[hi, _t5_bucket(dist)].astype(F32), -1, 2)


def _swiglu(x, w1, w3, w2):
    return (jax.nn.silu(x @ w1) * (x @ w3)) @ w2


def _moe(x, router, router_b, w1, w3, w2):
    B, T = x.shape[0], x.shape[1]
    xt = x.reshape(-1, D_MODEL)
    N = xt.shape[0]
    logits = (xt @ router).astype(F32) + router_b.astype(F32)
    top_l, top_e = lax.top_k(logits, TOP_K)
    gate = jax.nn.softmax(top_l, axis=-1)
    A = N * TOP_K
    e_flat = top_e.reshape(-1)
    tok = jnp.arange(A) // TOP_K
    order = jnp.argsort(e_flat)
    e_sorted = e_flat[order]
    tok_sorted = tok[order]
    counts = jnp.bincount(e_flat, length=N_EXPERTS)
    padded = (counts + MOE_BLK - 1) // MOE_BLK * MOE_BLK
    pad_end = jnp.cumsum(padded)
    grp_start = jnp.cumsum(counts) - counts
    dest = (pad_end - padded)[e_sorted] + jnp.arange(A) - grp_start[e_sorted]
    n_blocks = -(-A // MOE_BLK) + N_EXPERTS
    buf = jnp.zeros((n_blocks * MOE_BLK, D_MODEL), xt.dtype).at[dest].set(xt[tok_sorted])
    blk_e = jnp.minimum(jnp.searchsorted(pad_end, jnp.arange(n_blocks) * MOE_BLK, side='right'), N_EXPERTS - 1)

    def expert_block(args):
        xb, e = args
        return (jax.nn.silu(xb @ w1[e]) * (xb @ w3[e])) @ w2[e]

    out = lax.map(expert_block, (buf.reshape(n_blocks, MOE_BLK, D_MODEL), blk_e)).reshape(-1, D_MODEL)
    contrib = out[dest] * gate.reshape(-1)[order][:, None].astype(out.dtype)
    y = jax.ops.segment_sum(contrib, tok_sorted, num_segments=N)
    return y.reshape(B, T, D_MODEL).astype(x.dtype)


def _fox_proj(h, w_in, b_f, qn, kn):
    B, T = h.shape[0], h.shape[1]
    p = h @ w_in
    q = _rms(p[..., :D_MODEL].reshape(B, T, FOX_HEADS, HEAD_DIM), qn)
    k = _rms(p[..., D_MODEL:2 * D_MODEL].reshape(B, T, FOX_HEADS, HEAD_DIM), kn)
    v = p[..., 2 * D_MODEL:3 * D_MODEL].reshape(B, T, FOX_HEADS, HEAD_DIM)
    gate = jax.nn.sigmoid(p[..., 3 * D_MODEL:4 * D_MODEL])
    logf = jax.nn.log_sigmoid((p[..., 4 * D_MODEL:] + b_f).astype(F32))
    return q, k, v, gate, logf


def _fox_scores(q, k, cq, ck):
    s = jnp.einsum('bqhd,bkhd->bhqk', q, k).astype(F32) * HEAD_DIM ** -0.5
    return s + jnp.swapaxes(cq, 1, 2)[..., :, None] - jnp.swapaxes(ck, 1, 2)[..., None, :]


def _fox_prompt(h, w_in, b_f, qn, kn, w_out):
    B, T = h.shape[0], h.shape[1]
    q, k, v, gate, logf = _fox_proj(h, w_in, b_f, qn, kn)
    c = jnp.cumsum(logf, axis=1)
    pos = jnp.arange(T)
    nqb = T // QBLK

    def blk(args):
        qb, cb, pb = args
        s = jnp.where(pb[:, None] >= pos[None, :], _fox_scores(qb, k, cb, c), NEG)
        p = jax.nn.softmax(s, axis=-1)
        return jnp.einsum('bhqk,bkhd->bqhd', p.astype(v.dtype), v)

    qb = jnp.moveaxis(q.reshape(B, nqb, QBLK, FOX_HEADS, HEAD_DIM), 1, 0)
    cb = jnp.moveaxis(c.reshape(B, nqb, QBLK, FOX_HEADS), 1, 0)
    o = lax.map(blk, (qb, cb, pos.reshape(nqb, QBLK)))
    o = jnp.moveaxis(o, 0, 1).reshape(B, T, D_MODEL)
    y = ((o * gate) @ w_out).astype(h.dtype)
    return y, jnp.stack([k, v], axis=2), logf


def _fox_sample(h, kv_pool, lf_pool, page_table, w_in, b_f, qn, kn, w_out):
    B, S = h.shape[0], h.shape[1]
    q, k, v, gate, logf = _fox_proj(h, w_in, b_f, qn, kn)
    kv_past = _gather_pages(kv_pool, page_table)
    lf_past = _gather_pages(lf_pool, page_table).astype(F32)
    P = kv_past.shape[1]
    c_past = jnp.cumsum(lf_past, axis=1)
    c_new = c_past[:, -1:] + jnp.cumsum(logf, axis=1)
    s_past = _fox_scores(q, kv_past[:, :, 0], c_new, c_past)
    causal = jnp.tril(jnp.ones((S, S), dtype=bool))
    s_new = jnp.where(causal, _fox_scores(q, k, c_new, c_new), NEG)
    p = jax.nn.softmax(jnp.concatenate([s_past, s_new], axis=-1), axis=-1).astype(v.dtype)
    o = (jnp.einsum('bhqk,bkhd->bqhd', p[..., :P], kv_past[:, :, 1])
         + jnp.einsum('bhqk,bkhd->bqhd', p[..., P:], v))
    y = ((o.reshape(B, S, D_MODEL) * gate) @ w_out).astype(h.dtype)
    return y, jnp.stack([k, v], axis=2), logf


def _hg_proj(h, w_in, lb):
    B, T = h.shape[0], h.shape[1]
    p = h @ w_in
    shp = (B, T, HG_HEADS, HG_DK)
    q = jax.nn.silu(p[..., :D_MODEL].astype(F32)).reshape(shp)
    f = lb + (1.0 - lb) * jax.nn.sigmoid(p[..., D_MODEL:2 * D_MODEL].astype(F32).reshape(shp))
    v = p[..., 2 * D_MODEL:3 * D_MODEL].astype(F32).reshape(B, T, HG_HEADS, HG_DV)
    gate = jax.nn.sigmoid(p[..., 3 * D_MODEL:])
    return q, 1.0 - f, v, jnp.log(f), gate


def _gla_chunk(S, q, k, v, logf):
    C = q.shape[1]
    b = jnp.cumsum(logf, axis=1)
    o_inter = jnp.einsum('bchk,bhkv->bchv', q * jnp.exp(b), S)
    causal = jnp.tril(jnp.ones((C, C), dtype=bool))[None, :, :, None, None]
    dec = jnp.where(causal, jnp.exp(jnp.minimum(b[:, :, None] - b[:, None, :], 0.0)), 0.0)
    att = jnp.einsum('bthk,bshk,btshk->bhts', q, k, dec)
    o_intra = jnp.einsum('bhts,bshv->bthv', att, v)
    b_last = b[:, -1]
    S_new = jnp.exp(b_last)[..., None] * S + jnp.einsum('bshk,bshv->bhkv', k * jnp.exp(b_last[:, None] - b), v)
    return S_new, o_inter + o_intra


def _hg_out(o, gate, on, w_out, dtype):
    B, T = o.shape[0], o.shape[1]
    o = _rms(o, on).reshape(B, T, D_MODEL).astype(dtype) * gate
    return (o @ w_out).astype(dtype)


def _hg_prompt(h, w_in, lb, on, w_out):
    B, T = h.shape[0], h.shape[1]
    q, k, v, logf, gate = _hg_proj(h, w_in, lb)
    nc = T // HG_CHUNK

    def chunks(a):
        return jnp.moveaxis(a.reshape((B, nc, HG_CHUNK) + a.shape[2:]), 1, 0)

    S0 = jnp.zeros((B, HG_HEADS, HG_DK, HG_DV), F32)
    S_fin, o = lax.scan(lambda S, xs: _gla_chunk(S, *xs), S0, (chunks(q), chunks(k), chunks(v), chunks(logf)))
    o = jnp.moveaxis(o, 0, 1).reshape(B, T, HG_HEADS, HG_DV)
    return _hg_out(o, gate, on, w_out, h.dtype), S_fin


def _hg_sample(h, state, w_in, lb, on, w_out):
    q, k, v, logf, gate = _hg_proj(h, w_in, lb)
    S_new, o = _gla_chunk(state.astype(F32), q, k, v, logf)
    return _hg_out(o, gate, on, w_out, h.dtype), S_new


def _s5_params(a_re, a_im, log_dt, b_re, b_im, c_re, c_im):
    A = lax.complex(a_re.astype(F32), a_im.astype(F32))
    dt = jnp.exp(log_dt.astype(F32))[:, None]
    Ab = jnp.exp(A * dt)
    Bb = ((Ab - 1.0) / A)[..., None] * lax.complex(b_re.astype(F32), b_im.astype(F32))
    Cc = lax.complex(c_re.astype(F32), c_im.astype(F32))
    return Ab, Bb, Cc


def _s5_chunk(h_prev, u, Ab, Bb, Cc):
    bu = jnp.einsum('gpc,blgc->blgp', Bb, u.astype(jnp.complex64))
    a = jnp.broadcast_to(Ab, bu.shape)

    def comb(x, y):
        return (x[0] * y[0], y[0] * x[1] + y[1])

    acum, xloc = lax.associative_scan(comb, (a, bu), axis=1)
    x = xloc + acum * h_prev[:, None]
    y = jnp.einsum('gcp,blgp->blgc', Cc, x).real
    return x[:, -1], y


def _s5_out(y, u, d, w_glu, dtype):
    z = jax.nn.gelu(y + d.astype(F32) * u).astype(dtype)
    ab = z @ w_glu
    return (ab[..., :D_MODEL] * jax.nn.sigmoid(ab[..., D_MODEL:])).astype(dtype)


def _s5_prompt(h, Ab, Bb, Cc, d, w_glu):
    B, T = h.shape[0], h.shape[1]
    u = h.astype(F32).reshape(B, T, S5_GROUPS, S5_CH)
    nc = T // S5_CHUNK
    uc = jnp.moveaxis(u.reshape(B, nc, S5_CHUNK, S5_GROUPS, S5_CH), 1, 0)
    h0 = jnp.zeros((B, S5_GROUPS, S5_STATE), jnp.complex64)
    h_fin, y = lax.scan(lambda hs, uu: _s5_chunk(hs, uu, Ab, Bb, Cc), h0, uc)
    y = jnp.moveaxis(y, 0, 1).reshape(B, T, D_MODEL)
    out = _s5_out(y, u.reshape(B, T, D_MODEL), d, w_glu, h.dtype)
    return out, jnp.stack([h_fin.real, h_fin.imag], axis=-1)


def _s5_sample(h, st, Ab, Bb, Cc, d, w_glu):
    B, S = h.shape[0], h.shape[1]
    u = h.astype(F32).reshape(B, S, S5_GROUPS, S5_CH)
    h_prev = lax.complex(st[..., 0].astype(F32), st[..., 1].astype(F32))
    h_new, y = _s5_chunk(h_prev, u, Ab, Bb, Cc)
    out = _s5_out(y.reshape(B, S, D_MODEL), u.reshape(B, S, D_MODEL), d, w_glu, h.dtype)
    return out, jnp.stack([h_new.real, h_new.imag], axis=-1)


def _nsa_proj(h, w_in, qn, kn):
    B, T = h.shape[0], h.shape[1]
    p = h @ w_in
    q = _rms(p[..., :D_MODEL].reshape(B, T, NSA_HEADS, HEAD_DIM), qn)
    kv = p[..., D_MODEL:D_MODEL + 6 * NSA_KVW].reshape(B, T, 6, NSA_KVH, HEAD_DIM)
    gates = jax.nn.sigmoid(p[..., D_MODEL + 6 * NSA_KVW:].reshape(B, T, 3, NSA_HEADS))
    rows = jnp.stack([kv[:, :, 0], kv[:, :, 1], _rms(kv[:, :, 2], kn[1]), kv[:, :, 3]], axis=2)
    wrows = jnp.stack([_rms(kv[:, :, 4], kn[2]), kv[:, :, 5]], axis=2)
    return q, gates, rows, wrows


def _nsa_compress(rows, pos, w, w1, w2):
    B, T = rows.shape[0], rows.shape[1]
    blk = rows.reshape(B, T // CMP_BLK, CMP_BLK, NSA_KVH, HEAD_DIM) + pos[:, None, :]
    pooled = jnp.einsum('bnjhd,j->bnhd', blk, w)
    return jax.nn.silu(pooled @ w1) @ w2


def _nsa_cmp_slc(q, rows, q_pos, cmp_pos, cmp_w, cmp_w1, cmp_w2, kn_cmp, table):
    B, Tq = q.shape[0], q.shape[1]
    T = rows.shape[1]
    nb = -(-T // CMP_BLK)
    rows = jnp.pad(rows, ((0, 0), (0, nb * CMP_BLK - T), (0, 0), (0, 0), (0, 0)))
    kc = _rms(_nsa_compress(rows[:, :, 0], cmp_pos[0], cmp_w[0], cmp_w1[0], cmp_w2[0]), kn_cmp)
    vc = _nsa_compress(rows[:, :, 1], cmp_pos[1], cmp_w[1], cmp_w1[1], cmp_w2[1])
    qg = q.reshape(B, Tq, NSA_KVH, NSA_GROUP, HEAD_DIM)
    scale = HEAD_DIM ** -0.5
    blk_idx = jnp.arange(nb)
    dist_c = q_pos[:, None] - (blk_idx * CMP_BLK + CMP_BLK - 1)[None, :]
    vis = dist_c >= 0
    s_c = jnp.einsum('bqhgd,bnhd->bhgqn', qg, kc).astype(F32) * scale + _bias_heads(table, dist_c)
    p_c = jax.nn.softmax(jnp.where(vis, s_c, NEG), axis=-1) * vis
    o_cmp = jnp.einsum('bhgqn,bnhd->bqhgd', p_c.astype(vc.dtype), vc).reshape(B, Tq, NSA_HEADS, HEAD_DIM)
    imp = p_c.sum(axis=2)
    cur = (q_pos // SEL_BLK)[:, None]
    cand = blk_idx[None, :] <= cur
    forced = (blk_idx[None, :] == 0) | (blk_idx[None, :] == cur) | (blk_idx[None, :] == cur - 1)
    score = jnp.where(cand, jnp.where(forced, FORCE, imp), -1.0)
    n_sel = min(N_SEL, nb)
    top_v, top_i = lax.top_k(score, n_sel)
    valid = top_v >= 0.0
    ks = jnp.moveaxis(rows[:, :, 2], 2, 1)
    vs = jnp.moveaxis(rows[:, :, 3], 2, 1)
    qblk = SLC_QBLK if Tq % SLC_QBLK == 0 else Tq
    nqb = Tq // qblk
    bi = jnp.arange(B)[:, None, None, None]
    hi = jnp.arange(NSA_KVH)[None, :, None, None]

    def sel_block(args):
        qb, ib, vb, pb = args
        kpos = (ib[..., None] * SEL_BLK + jnp.arange(SEL_BLK)).reshape(B, NSA_KVH, qblk, n_sel * SEL_BLK)
        kmask = jnp.repeat(vb, SEL_BLK, axis=-1)
        kg = ks[bi, hi, kpos]
        vg = vs[bi, hi, kpos]
        dist = pb[None, None, :, None] - kpos
        mask = (kmask & (dist >= 0))[:, :, None]
        s = jnp.einsum('bqhgd,bhqnd->bhgqn', qb, kg).astype(F32) * scale + _bias_group(table, dist)
        p = jax.nn.softmax(jnp.where(mask, s, NEG), axis=-1)
        return jnp.einsum('bhgqn,bhqnd->bqhgd', p.astype(vg.dtype), vg)

    qs = jnp.moveaxis(qg.reshape(B, nqb, qblk, NSA_KVH, NSA_GROUP, HEAD_DIM), 1, 0)
    i_s = jnp.moveaxis(top_i.reshape(B, NSA_KVH, nqb, qblk, n_sel), 2, 0)
    v_s = jnp.moveaxis(valid.reshape(B, NSA_KVH, nqb, qblk, n_sel), 2, 0)
    o_slc = lax.map(sel_block, (qs, i_s, v_s, q_pos.reshape(nqb, qblk)))
    o_slc = jnp.moveaxis(o_slc, 0, 1).reshape(B, Tq, NSA_HEADS, HEAD_DIM)
    return o_cmp, o_slc


def _win_attend(q, kw, vw, q_pos, k_pos, table):
    B, Q = q.shape[0], q.shape[1]
    qg = q.reshape(B, Q, NSA_KVH, NSA_GROUP, HEAD_DIM)
    dist = q_pos[:, None] - k_pos[None, :]
    mask = (dist >= 0) & (dist <= WINDOW) & (k_pos[None, :] >= 0)
    s = jnp.einsum('bqhgd,bkhd->bhgqk', qg, kw).astype(F32) * HEAD_DIM ** -0.5 + _bias_heads(table, dist)
    p = jax.nn.softmax(jnp.where(mask, s, NEG), axis=-1)
    return jnp.einsum('bhgqk,bkhd->bqhgd', p.astype(vw.dtype), vw).reshape(B, Q, NSA_HEADS, HEAD_DIM)


def _nsa_combine(gates, o_cmp, o_slc, o_win, w_out, dtype):
    B, T = o_cmp.shape[0], o_cmp.shape[1]
    o = gates[:, :, 0, :, None] * o_cmp + gates[:, :, 1, :, None] * o_slc + gates[:, :, 2, :, None] * o_win
    return (o.reshape(B, T, D_MODEL) @ w_out).astype(dtype)


def _nsa_prompt(h, w_in, qn, kn, cmp_pos, cmp_w, cmp_w1, cmp_w2, w_out, table):
    B, T = h.shape[0], h.shape[1]
    q, gates, rows, wrows = _nsa_proj(h, w_in, qn, kn)
    o_cmp, o_slc = _nsa_cmp_slc(q, rows, jnp.arange(T), cmp_pos, cmp_w, cmp_w1, cmp_w2, kn[0], table)
    padded = jnp.pad(wrows, ((0, 0), (WINDOW, 0), (0, 0), (0, 0), (0, 0)))
    nqb = T // QBLK

    def win_block(args):
        i, qq = args
        s0 = i * QBLK
        kv = lax.dynamic_slice_in_dim(padded, s0, WINDOW + QBLK, axis=1)
        return _win_attend(qq, kv[:, :, 0], kv[:, :, 1], s0 + jnp.arange(QBLK),
                           s0 - WINDOW + jnp.arange(WINDOW + QBLK), table)

    qb = jnp.moveaxis(q.reshape(B, nqb, QBLK, NSA_HEADS, HEAD_DIM), 1, 0)
    o_win = jnp.moveaxis(lax.map(win_block, (jnp.arange(nqb), qb)), 0, 1).reshape(B, T, NSA_HEADS, HEAD_DIM)
    y = _nsa_combine(gates, o_cmp, o_slc, o_win, w_out, h.dtype)
    lb = min(WINDOW, T)
    return y, rows, wrows[:, T - lb:]


def _nsa_sample(h, pool, win_buf, page_table, w_in, qn, kn, cmp_pos, cmp_w, cmp_w1, cmp_w2, w_out, table):
    B, S = h.shape[0], h.shape[1]
    q, gates, rows_new, wrows_new = _nsa_proj(h, w_in, qn, kn)
    past = _gather_pages(pool, page_table)
    P = past.shape[1]
    q_pos = P + jnp.arange(S)
    rows = jnp.concatenate([past, rows_new], axis=1)
    o_cmp, o_slc = _nsa_cmp_slc(q, rows, q_pos, cmp_pos, cmp_w, cmp_w1, cmp_w2, kn[0], table)
    Lb = win_buf.shape[1]
    kw = jnp.concatenate([win_buf, wrows_new], axis=1)
    o_win = _win_attend(q, kw[:, :, 0], kw[:, :, 1], q_pos, P - Lb + jnp.arange(Lb + S), table)
    y = _nsa_combine(gates, o_cmp, o_slc, o_win, w_out, h.dtype)
    return y, rows_new, kw[:, -Lb:]


def setup_inputs(seed: int = 0) -> dict:
    key = jax.random.key(seed)
    ks = iter(jax.random.split(key, 64))

    def nrm(shape, scale=1.0):
        return jax.random.normal(next(ks), shape, F32) * scale

    def gain(shape):
        return 1.0 + 0.01 * jax.random.normal(next(ks), shape, F32)

    n_pages = PAST_LEN // PAGE_SIZE
    n_phys = (DEC_BATCH * n_pages * 5) // 4
    win_buf = min(WINDOW, PAST_LEN)
    page_table = jax.random.permutation(next(ks), n_phys)[:DEC_BATCH * n_pages].reshape(DEC_BATCH, n_pages).astype(jnp.int32)
    n_idx = jnp.arange(S5_STATE, dtype=F32)
    lo, hi = math.log(1e-3), math.log(1e-1)
    d = D_MODEL
    return {
        'x_prompt': nrm((BATCH, SEQ, d)),
        'x_sample': nrm((DEC_BATCH, DEC_SEQ, d)),
        'cache_fox_kv': nrm((N_FOX, n_phys, PAGE_SIZE, 2, FOX_HEADS, HEAD_DIM)),
        'cache_fox_logf': jax.nn.log_sigmoid(2.0 + nrm((N_FOX, n_phys, PAGE_SIZE, FOX_HEADS))),
        'state_hgrn': nrm((N_HG, DEC_BATCH, HG_HEADS, HG_DK, HG_DV), 0.3),
        'state_s5': nrm((N_S5, DEC_BATCH, S5_GROUPS, S5_STATE, 2), 0.1),
        'cache_nsa_kv': nrm((N_NSA, n_phys, PAGE_SIZE, 4, NSA_KVH, HEAD_DIM)),
        'state_nsa_win': nrm((N_NSA, DEC_BATCH, win_buf, 2, NSA_KVH, HEAD_DIM)),
        'page_table': page_table,
        'norm_mix': gain((DEPTH, d)),
        'norm_ffn': gain((DEPTH, d)),
        'fox_w_in': nrm((N_FOX, d, 4 * d + FOX_HEADS), d ** -0.5),
        'fox_b_f': 2.0 + nrm((N_FOX, FOX_HEADS), 0.1),
        'fox_q_norm': gain((N_FOX, HEAD_DIM)),
        'fox_k_norm': gain((N_FOX, HEAD_DIM)),
        'fox_w_out': nrm((N_FOX, d, d), d ** -0.5),
        'hg_w_in': nrm((N_HG, d, 4 * d), d ** -0.5),
        'hg_lb': nrm((DEPTH, d), 0.1),
        'hg_o_norm': gain((N_HG, HG_DV)),
        'hg_w_out': nrm((N_HG, d, d), d ** -0.5),
        's5_a_re': -0.5 + nrm((N_S5, S5_GROUPS, S5_STATE), 0.01),
        's5_a_im': jnp.pi * n_idx + nrm((N_S5, S5_GROUPS, S5_STATE), 0.01),
        's5_log_dt': lo + (hi - lo) * jax.random.uniform(next(ks), (N_S5, S5_GROUPS), F32),
        's5_b_re': nrm((N_S5, S5_GROUPS, S5_STATE, S5_CH), (2 * S5_CH) ** -0.5),
        's5_b_im': nrm((N_S5, S5_GROUPS, S5_STATE, S5_CH), (2 * S5_CH) ** -0.5),
        's5_c_re': nrm((N_S5, S5_GROUPS, S5_CH, S5_STATE), 0.5),
        's5_c_im': nrm((N_S5, S5_GROUPS, S5_CH, S5_STATE), 0.5),
        's5_d': nrm((N_S5, d)),
        's5_w_glu': nrm((N_S5, d, 2 * d), d ** -0.5),
        'nsa_w_in': nrm((N_NSA, d, d + 6 * NSA_KVW + 3 * NSA_HEADS), d ** -0.5),
        'nsa_q_norm': gain((N_NSA, HEAD_DIM)),
        'nsa_k_norm': gain((N_NSA, 3, HEAD_DIM)),
        'nsa_cmp_pos': nrm((N_NSA, 2, CMP_BLK, HEAD_DIM), 0.02),
        'nsa_cmp_w': nrm((N_NSA, 2, CMP_BLK), CMP_BLK ** -0.5),
        'nsa_cmp_w1': nrm((N_NSA, 2, HEAD_DIM, HEAD_DIM), HEAD_DIM ** -0.5),
        'nsa_cmp_w2': nrm((N_NSA, 2, HEAD_DIM, HEAD_DIM), 2.0 * HEAD_DIM ** -0.5),
        'nsa_w_out': nrm((N_NSA, d, d), d ** -0.5),
        'rel_bias': nrm((N_BUCKETS, NSA_HEADS), 0.5),
        'ffn_w1': nrm((N_DENSE, d, D_FF), d ** -0.5),
        'ffn_w3': nrm((N_DENSE, d, D_FF), d ** -0.5),
        'ffn_w2': nrm((N_DENSE, D_FF, d), D_FF ** -0.5),
        'moe_router': nrm((N_MOE, d, N_EXPERTS), d ** -0.5),
        'moe_router_b': nrm((N_MOE, N_EXPERTS), 0.01),
        'moe_w1': nrm((N_MOE, N_EXPERTS, d, D_FF_EXPERT), d ** -0.5),
        'moe_w3': nrm((N_MOE, N_EXPERTS, d, D_FF_EXPERT), d ** -0.5),
        'moe_w2': nrm((N_MOE, N_EXPERTS, D_FF_EXPERT, d), D_FF_EXPERT ** -0.5),
    }


def reference(x_prompt, x_sample, cache_fox_kv, cache_fox_logf, state_hgrn, state_s5, cache_nsa_kv,
              state_nsa_win, page_table, norm_mix, norm_ffn, fox_w_in, fox_b_f, fox_q_norm, fox_k_norm,
              fox_w_out, hg_w_in, hg_lb, hg_o_norm, hg_w_out, s5_a_re, s5_a_im, s5_log_dt, s5_b_re, s5_b_im,
              s5_c_re, s5_c_im, s5_d, s5_w_glu, nsa_w_in, nsa_q_norm, nsa_k_norm, nsa_cmp_pos, nsa_cmp_w,
              nsa_cmp_w1, nsa_cmp_w2, nsa_w_out, rel_bias, ffn_w1, ffn_w3, ffn_w2, moe_router, moe_router_b,
              moe_w1, moe_w3, moe_w2):
    xp, xs = x_prompt, x_sample
    sm = jax.nn.softmax(hg_lb.astype(F32), axis=0)
    lower_bounds = jnp.cumsum(sm, axis=0) - sm[0]
    fox_kv_p, fox_lf_p, fox_kv_s, fox_lf_s = [], [], [], []
    hg_p, hg_s, s5_p, s5_s = [], [], [], []
    nsa_kv_p, nsa_win_p, nsa_kv_s, nsa_win_s = [], [], [], []
    for i in range(DEPTH):
        j = i // N_MIXERS
        kind = i % N_MIXERS
        hp = _rms(xp, norm_mix[i])
        hs = _rms(xs, norm_mix[i])
        if kind == 0:
            yp, kvp, lfp = _fox_prompt(hp, fox_w_in[j], fox_b_f[j], fox_q_norm[j], fox_k_norm[j], fox_w_out[j])
            ys, kvs, lfs = _fox_sample(hs, cache_fox_kv[j], cache_fox_logf[j], page_table, fox_w_in[j],
                                       fox_b_f[j], fox_q_norm[j], fox_k_norm[j], fox_w_out[j])
            fox_kv_p.append(kvp); fox_lf_p.append(lfp); fox_kv_s.append(kvs); fox_lf_s.append(lfs)
        elif kind == 1:
            lb = lower_bounds[i].reshape(HG_HEADS, HG_DK)
            yp, sp = _hg_prompt(hp, hg_w_in[j], lb, hg_o_norm[j], hg_w_out[j])
            ys, ss = _hg_sample(hs, state_hgrn[j], hg_w_in[j], lb, hg_o_norm[j], hg_w_out[j])
            hg_p.append(sp); hg_s.append(ss)
        elif kind == 2:
            Ab, Bb, Cc = _s5_params(s5_a_re[j], s5_a_im[j], s5_log_dt[j], s5_b_re[j], s5_b_im[j], s5_c_re[j], s5_c_im[j])
            yp, sp = _s5_prompt(hp, Ab, Bb, Cc, s5_d[j], s5_w_glu[j])
            ys, ss = _s5_sample(hs, state_s5[j], Ab, Bb, Cc, s5_d[j], s5_w_glu[j])
            s5_p.append(sp); s5_s.append(ss)
        else:
            yp, rp, wp = _nsa_prompt(hp, nsa_w_in[j], nsa_q_norm[j], nsa_k_norm[j], nsa_cmp_pos[j], nsa_cmp_w[j],
                                     nsa_cmp_w1[j], nsa_cmp_w2[j], nsa_w_out[j], rel_bias)
            ys, rs, wsn = _nsa_sample(hs, cache_nsa_kv[j], state_nsa_win[j], page_table, nsa_w_in[j], nsa_q_norm[j],
                                      nsa_k_norm[j], nsa_cmp_pos[j], nsa_cmp_w[j], nsa_cmp_w1[j], nsa_cmp_w2[j],
                                      nsa_w_out[j], rel_bias)
            nsa_kv_p.append(rp); nsa_win_p.append(wp); nsa_kv_s.append(rs); nsa_win_s.append(wsn)
        xp = xp + yp
        xs = xs + ys
        hp = _rms(xp, norm_ffn[i])
        hs = _rms(xs, norm_ffn[i])
        f = i // 2
        if i % 2 == 0:
            xp = xp + _swiglu(hp, ffn_w1[f], ffn_w3[f], ffn_w2[f]).astype(xp.dtype)
            xs = xs + _swiglu(hs, ffn_w1[f], ffn_w3[f], ffn_w2[f]).astype(xs.dtype)
        else:
            xp = xp + _moe(hp, moe_router[f], moe_router_b[f], moe_w1[f], moe_w3[f], moe_w2[f])
            xs = xs + _moe(hs, moe_router[f], moe_router_b[f], moe_w1[f], moe_w3[f], moe_w2[f])
    return (xp, xs,
            jnp.stack(fox_kv_p), jnp.stack(fox_lf_p), jnp.stack(fox_kv_s), jnp.stack(fox_lf_s),
            jnp.stack(hg_p), jnp.stack(hg_s), jnp.stack(s5_p), jnp.stack(s5_s),
            jnp.stack(nsa_kv_p), jnp.stack(nsa_win_p), jnp.stack(nsa_kv_s), jnp.stack(nsa_win_s))
```

```python
import functools
import math

import numpy as np
import jax
import jax.numpy as jnp
from jax import lax
from jax.experimental import pallas as pl
from jax.experimental.pallas import tpu as pltpu

F32 = jnp.float32
BF16 = jnp.bfloat16
EPS = 1e-6
NEG = -1e30
FORCE = 1e9

HEAD_DIM = 128
LANES = 128
V7X_VMEM_CAP_MB = 60
SAMPLE_ROWS = 16

HG_CHUNK = 16
S5_CH = 16
S5_CHUNK = 128
CMP_BLK = 64
SEL_BLK = 64
N_SEL = 16
WINDOW = 512
N_BUCKETS = 32
MAX_DIST = 128
TOP_K = 2
MOE_ROWS = 512


def _cparams(sem, vmem_bytes):
    mb = min(V7X_VMEM_CAP_MB, max(16, int(vmem_bytes / (1 << 20)) + 8))
    return pltpu.CompilerParams(dimension_semantics=sem, vmem_limit_bytes=mb << 20)


def _rms_kernel(x_ref, g_ref, o_ref):
    x = x_ref[...]
    y = x * lax.rsqrt(jnp.mean(x * x, axis=-1, keepdims=True) + EPS)
    o_ref[...] = (y * g_ref[...]).astype(o_ref.dtype)


def rms_norm(x, g, out_dtype=BF16):
    n, d = x.shape
    tm = min(n, 512)
    return pl.pallas_call(
        _rms_kernel,
        grid=(n // tm,),
        in_specs=[pl.BlockSpec((tm, d), lambda i: (i, 0)),
                  pl.BlockSpec((1, d), lambda i: (0, 0))],
        out_specs=pl.BlockSpec((tm, d), lambda i: (i, 0)),
        out_shape=jax.ShapeDtypeStruct((n, d), out_dtype),
        compiler_params=_cparams(("parallel",), 2 * tm * d * (4 + 4)),
        name="rms_norm",
    )(x, g.reshape(1, d).astype(F32))


def _mm_kernel(*refs, nw, nv, nt, no, ni, has_e, epilogue):
    pos = 0
    eid_ref = None
    if has_e:
        eid_ref = refs[0]
        pos = 1
    lhs_ref = refs[pos]
    pos += 1
    w_refs = refs[pos:pos + nw]
    pos += nw
    vec_refs = refs[pos:pos + nv]
    pos += nv
    tile_refs = refs[pos:pos + nt]
    pos += nt
    out_refs = refs[pos:pos + no]
    wc_ref = refs[-1]
    j = pl.program_id(0)
    i = pl.program_id(1)
    if has_e:
        prev = eid_ref[jnp.maximum(i - 1, 0)]
        recast = jnp.logical_or(i == 0, eid_ref[i] != prev)
        active = i < eid_ref[ni]
    else:
        recast = i == 0
        active = None

    @pl.when(recast)
    def _():
        for t in range(nw):
            wc_ref[t] = w_refs[t][...].astype(BF16)

    def compute():
        x = lhs_ref[...]
        accs = [jnp.dot(x, wc_ref[t], preferred_element_type=F32) for t in range(nw)]
        res = epilogue(accs, [v[...] for v in vec_refs], [t[...] for t in tile_refs], j)
        for o, r in zip(out_refs, res):
            o[...] = r.astype(o.dtype)

    if has_e:
        pl.when(active)(compute)

        @pl.when(jnp.logical_not(active))
        def _():
            for o in out_refs:
                o[...] = jnp.zeros(o.shape, o.dtype)
    else:
        compute()


def matmul(lhs, ws, outs, epilogue, *, tm, tn, nj, tk=None, kb=0, vecs=(), tiles=(), eids=None, name="mm"):
    n = lhs.shape[0]
    assert n % tm == 0, (n, tm)
    ni = n // tm
    kdim = ws[0][0].shape[-2]
    tk = tk or kdim
    assert kdim % tk == 0
    has_e = eids is not None
    nw, nv, nt, no = len(ws), len(vecs), len(tiles), len(outs)

    def wspec(arr, off):
        if arr.ndim == 3:
            return pl.BlockSpec((None, tk, tn), lambda j, i, e: (e[i], kb, j + off))
        if has_e:
            return pl.BlockSpec((tk, tn), lambda j, i, e: (kb, j + off))
        return pl.BlockSpec((tk, tn), lambda j, i: (kb, j + off))

    def fix(f):
        return (lambda j, i, e: f(j, i)) if has_e else f

    in_specs = [pl.BlockSpec((tm, tk), fix(lambda j, i: (i, kb)))]
    in_specs += [wspec(a, off) for a, off in ws]
    in_specs += [pl.BlockSpec((1, tn), fix(functools.partial(lambda j, i, off: (0, j + off), off=off))) for _, off in vecs]
    in_specs += [pl.BlockSpec((tm, tn), fix(functools.partial(lambda j, i, off: (i, j + off), off=off))) for _, off in tiles]
    out_specs = [pl.BlockSpec((tm, tn), fix(functools.partial(lambda j, i, off: (i, j + off), off=off))) for _, _, off in outs]
    out_shape = [jax.ShapeDtypeStruct((n, c), dt) for c, dt, _ in outs]
    vmem = (2 * tm * tk * 2 + nw * (2 * tk * tn * 4 + tk * tn * 2)
            + sum(2 * tm * tn * jnp.dtype(dt).itemsize for _, dt, _ in outs)
            + nt * 2 * tm * tn * 4 + (nw + 1) * tm * tn * 4)
    kern = functools.partial(_mm_kernel, nw=nw, nv=nv, nt=nt, no=no, ni=ni, has_e=has_e, epilogue=epilogue)
    scratch = [pltpu.VMEM((nw, tk, tn), BF16)]
    args = [lhs] + [a for a, _ in ws] + [a for a, _ in vecs] + [a for a, _ in tiles]
    if has_e:
        grid_spec = pltpu.PrefetchScalarGridSpec(num_scalar_prefetch=1, grid=(nj, ni), in_specs=in_specs,
                                                 out_specs=out_specs, scratch_shapes=scratch)
        args = [eids] + args
    else:
        grid_spec = pl.GridSpec(grid=(nj, ni), in_specs=in_specs, out_specs=out_specs, scratch_shapes=scratch)
    return pl.pallas_call(kern, grid_spec=grid_spec, out_shape=out_shape,
                          compiler_params=_cparams(("arbitrary", "arbitrary"), vmem), name=name)(*args)


def _row_tile(n, pref):
    return pref if n % pref == 0 else n


def _head_rms(x, g):
    parts = []
    for h in range(x.shape[1] // HEAD_DIM):
        xh = x[:, h * HEAD_DIM:(h + 1) * HEAD_DIM]
        gh = g[:, h * HEAD_DIM:(h + 1) * HEAD_DIM]
        parts.append(xh * lax.rsqrt(jnp.mean(xh * xh, axis=-1, keepdims=True) + EPS) * gh)
    return parts[0] if len(parts) == 1 else jnp.concatenate(parts, axis=-1)


def _vec(v, width=None):
    v = v.reshape(1, -1).astype(F32)
    if width is not None and v.shape[1] < width:
        v = jnp.pad(v, ((0, 0), (0, width - v.shape[1])))
    return v


def _log_sigmoid(x):
    return jnp.minimum(x, 0.0) - jnp.log(1.0 + jnp.exp(-jnp.abs(x)))


def _swiglu_epi(accs, vecs, tiles, j):
    a, b = accs
    return [a * jax.nn.sigmoid(a) * b]


def _resid_epi(accs, vecs, tiles, j):
    return [accs[0] + tiles[0]]


def ffn_up(h, w1, w3, eids=None, tm=1024):
    dff = w1.shape[-1]
    tn = 256
    assert dff % tn == 0
    return matmul(h, [(w1, 0), (w3, 0)], [(dff, BF16, 0)], _swiglu_epi, tm=tm, tn=tn, nj=dff // tn,
                  eids=eids, name="ffn_up")[0]


def ffn_down(hmid, w2, resid, eids=None, tm=512):
    dff, d = w2.shape[-2], w2.shape[-1]
    halves = 2 if dff % (2 * LANES) == 0 and dff > 4096 else 1
    tk = dff // halves
    tn = min(512, d)
    out = resid
    for kb in range(halves):
        out = matmul(hmid, [(w2, 0)], [(d, F32, 0)], _resid_epi, tm=tm, tn=tn, nj=d // tn, tk=tk, kb=kb,
                     tiles=[(out, 0)], eids=eids, name="ffn_down")[0]
    return out


def dense_ffn(xp, xs, hp, hs, w1, w3, w2):
    mp = ffn_up(hp, w1, w3, tm=_row_tile(hp.shape[0], 1024))
    xp = ffn_down(mp, w2, xp, tm=_row_tile(hp.shape[0], 512))
    ms = ffn_up(hs, w1, w3, tm=hs.shape[0])
    xs = ffn_down(ms, w2, xs, tm=hs.shape[0])
    return xp, xs


def _router_epi(accs, vecs, tiles, j):
    return [accs[0] + vecs[0]]


def moe_ffn(xp, xs, hp, hs, n_s, router, router_b, w1, w3, w2):
    n_p, d = hp.shape
    n_e = router.shape[-1]
    h_all = jnp.concatenate([hp, hs[:n_s]], axis=0)
    n = n_p + n_s
    rw = jnp.pad(router, ((0, 0), (0, LANES - n_e)))
    lp = matmul(hp, [(rw, 0)], [(LANES, F32, 0)], _router_epi, tm=_row_tile(n_p, 1024), tn=LANES, nj=1,
                vecs=[(_vec(router_b, LANES), 0)], name="router")[0]
    ls = matmul(hs, [(rw, 0)], [(LANES, F32, 0)], _router_epi, tm=hs.shape[0], tn=LANES, nj=1,
                vecs=[(_vec(router_b, LANES), 0)], name="router")[0]
    logits = jnp.concatenate([lp[:, :n_e], ls[:n_s, :n_e]], axis=0)
    top_l, top_e = lax.top_k(logits, TOP_K)
    gate = jax.nn.softmax(top_l, axis=-1)
    a = n * TOP_K
    bm = MOE_ROWS
    e_flat = top_e.reshape(-1)
    tok = jnp.arange(a) // TOP_K
    order = jnp.argsort(e_flat)
    e_sorted = e_flat[order]
    tok_sorted = tok[order]
    counts = jnp.bincount(e_flat, length=n_e)
    padded = (counts + bm - 1) // bm * bm
    pad_end = jnp.cumsum(padded)
    grp_start = jnp.cumsum(counts) - counts
    dest = (pad_end - padded)[e_sorted] + jnp.arange(a) - grp_start[e_sorted]
    n_blocks = -(-a // bm) + n_e
    blk_e = jnp.minimum(jnp.searchsorted(pad_end, jnp.arange(n_blocks) * bm, side='right'), n_e - 1)
    n_act = pad_end[-1] // bm
    eids = jnp.concatenate([blk_e, n_act[None]]).astype(jnp.int32)
    buf = jnp.zeros((n_blocks * bm, d), h_all.dtype).at[dest].set(h_all[tok_sorted])
    mid = ffn_up(buf, w1, w3, eids=eids, tm=bm)
    zero = jnp.zeros((n_blocks * bm, d), F32)
    out = ffn_down(mid, w2, zero, eids=eids, tm=bm)
    pos = jnp.zeros((a,), jnp.int32).at[order].set(dest.astype(jnp.int32)).reshape(n, TOP_K)
    y = out[pos[:, 0]] * gate[:, 0:1] + out[pos[:, 1]] * gate[:, 1:2]
    xp = xp + y[:n_p]
    xs = xs.at[:n_s].add(y[n_p:])
    return xp, xs


def _fox_attn_kernel(q_ref, k_ref, v_ref, ccol_ref, crow_ref, g_ref, o_ref, *, tq, tk, scale):
    qi = pl.program_id(2)
    q = q_ref[...]
    cq = ccol_ref[...]
    qpos = qi * tq + lax.broadcasted_iota(jnp.int32, (tq, tk), 0)
    koff = lax.broadcasted_iota(jnp.int32, (tq, tk), 1)

    def body(j, carry):
        m, l, acc = carry
        start = pl.multiple_of(j * tk, tk)
        k = k_ref[pl.ds(start, tk), :]
        v = v_ref[pl.ds(start, tk), :]
        s = lax.dot_general(q, k, (((1,), (1,)), ((), ())), preferred_element_type=F32) * scale
        s = s + cq - crow_ref[pl.ds(j, 1), :]
        s = jnp.where(qpos >= koff + j * tk, s, NEG)
        m_new = jnp.maximum(m, jnp.max(s, axis=-1, keepdims=True))
        alpha = jnp.exp(m - m_new)
        p = jnp.exp(s - m_new)
        l = alpha * l + jnp.sum(p, axis=-1, keepdims=True)
        acc = alpha * acc + jnp.dot(p.astype(BF16), v, preferred_element_type=F32)
        return m_new, l, acc

    n_kv = (qi * tq + tq + tk - 1) // tk
    init = (jnp.full((tq, 1), NEG, F32), jnp.zeros((tq, 1), F32), jnp.zeros((tq, HEAD_DIM), F32))
    m, l, acc = lax.fori_loop(0, n_kv, body, init)
    o_ref[...] = (acc / l * g_ref[...]).astype(o_ref.dtype)


def fox_attention(q16, kv16, c, gate, b, t, nh):
    tq = min(256, t)
    tk = tq
    nq = t // tq
    ccol = jnp.transpose(c, (0, 2, 1)).reshape(b, nh, t, 1)
    crow = jnp.transpose(c, (0, 2, 1)).reshape(b, nh, t // tk, tk)
    kern = functools.partial(_fox_attn_kernel, tq=tq, tk=tk, scale=HEAD_DIM ** -0.5)
    return pl.pallas_call(
        kern,
        grid=(b, nh, nq),
        in_specs=[pl.BlockSpec((tq, HEAD_DIM), lambda bi, h, i: (bi * nq + i, h)),
                  pl.BlockSpec((t, HEAD_DIM), lambda bi, h, i: (bi, h)),
                  pl.BlockSpec((t, HEAD_DIM), lambda bi, h, i: (bi, nh + h)),
                  pl.BlockSpec((None, None, tq, 1), lambda bi, h, i: (bi, h, i, 0)),
                  pl.BlockSpec((None, None, t // tk, tk), lambda bi, h, i: (bi, h, 0, 0)),
                  pl.BlockSpec((tq, HEAD_DIM), lambda bi, h, i: (bi * nq + i, h))],
        out_specs=pl.BlockSpec((tq, HEAD_DIM), lambda bi, h, i: (bi * nq + i, h)),
        out_shape=jax.ShapeDtypeStruct(q16.shape, BF16),
        compiler_params=_cparams(("parallel", "parallel", "arbitrary"), 8 << 20),
        name="fox_attention",
    )(q16, kv16, kv16, ccol, crow, gate)


def _fox_proj(h, w_in, b_f, qn, kn, tm):
    d = h.shape[1]
    nh = d // HEAD_DIM
    tn = 512
    nb = d // tn
    qn_v, kn_v = _vec(qn), _vec(kn)

    def q_epi(accs, vecs, tiles, j):
        return [_head_rms(accs[0], vecs[0])]

    def kv_epi(accs, vecs, tiles, j):
        y = jnp.where(j < nb, _head_rms(accs[0], vecs[0]), accs[0])
        return [y, y]

    def gate_epi(accs, vecs, tiles, j):
        return [jax.nn.sigmoid(accs[0])]

    def lf_epi(accs, vecs, tiles, j):
        return [_log_sigmoid(accs[0] + vecs[0])]

    def tiled(v):
        return jnp.tile(v, (1, 2 * d // HEAD_DIM))

    q16 = matmul(h, [(w_in, 0)], [(d, BF16, 0)], q_epi, tm=tm, tn=tn, nj=nb, vecs=[(tiled(qn_v), 0)], name="fox_q")[0]
    kv, kv16 = matmul(h, [(w_in, nb)], [(2 * d, F32, 0), (2 * d, BF16, 0)], kv_epi, tm=tm, tn=tn, nj=2 * nb,
                      vecs=[(tiled(kn_v), 0)], name="fox_kv")
    gate = matmul(h, [(w_in, 3 * nb)], [(d, F32, 0)], gate_epi, tm=tm, tn=tn, nj=nb, name="fox_gate")[0]
    lf = matmul(h, [(w_in, 4 * d // LANES)], [(LANES, F32, 0)], lf_epi, tm=tm, tn=LANES, nj=1,
                vecs=[(_vec(b_f, LANES), 0)], name="fox_logf")[0][:, :nh]
    return q16, kv, kv16, gate, lf


def _fox_scores(q, k, cq, ck):
    s = jnp.einsum('bqhd,bkhd->bhqk', q, k).astype(F32) * HEAD_DIM ** -0.5
    return s + jnp.swapaxes(cq, 1, 2)[..., :, None] - jnp.swapaxes(ck, 1, 2)[..., None, :]


def fox_layer(xp, xs, hp, hs, b, t, n_s, kv_pool, lf_pool, page_table, w_in, b_f, qn, kn, w_out):
    n_p, d = hp.shape
    nh = d // HEAD_DIM
    tmp = _row_tile(n_p, 1024)
    q16, kvp, kv16, gate, lf = _fox_proj(hp, w_in, b_f, qn, kn, tmp)
    lfp = lf.reshape(b, t, nh)
    c = jnp.cumsum(lfp, axis=1)
    og = fox_attention(q16, kv16, c, gate, b, t, nh)
    xp = matmul(og, [(w_out, 0)], [(d, F32, 0)], _resid_epi, tm=tmp, tn=512, nj=d // 512, tiles=[(xp, 0)], name="fox_out")[0]
    sr = hs.shape[0]
    q16s, kvs, _, gates, lfs = _fox_proj(hs, w_in, b_f, qn, kn, sr)
    q = q16s[:n_s].astype(F32).reshape(n_s, 1, nh, HEAD_DIM)
    k = kvs[:n_s, :d].reshape(n_s, 1, nh, HEAD_DIM)
    v = kvs[:n_s, d:].reshape(n_s, 1, nh, HEAD_DIM)
    logf = lfs[:n_s].reshape(n_s, 1, nh)
    g = kv_pool[page_table]
    kv_past = g.reshape((g.shape[0], g.shape[1] * g.shape[2]) + g.shape[3:])
    g = lf_pool[page_table]
    lf_past = g.reshape((g.shape[0], g.shape[1] * g.shape[2]) + g.shape[3:]).astype(F32)
    p_len = kv_past.shape[1]
    c_past = jnp.cumsum(lf_past, axis=1)
    c_new = c_past[:, -1:] + jnp.cumsum(logf, axis=1)
    s_past = _fox_scores(q, kv_past[:, :, 0], c_new, c_past)
    s_new = _fox_scores(q, k, c_new, c_new)
    p = jax.nn.softmax(jnp.concatenate([s_past, s_new], axis=-1), axis=-1)
    o = (jnp.einsum('bhqk,bkhd->bqhd', p[..., :p_len], kv_past[:, :, 1])
         + jnp.einsum('bhqk,bkhd->bqhd', p[..., p_len:], v))
    ogs = (o.reshape(n_s, d) * gates[:n_s]).astype(BF16)
    ogs = jnp.pad(ogs, ((0, sr - n_s), (0, 0)))
    xs = matmul(ogs, [(w_out, 0)], [(d, F32, 0)], _resid_epi, tm=sr, tn=512, nj=d // 512, tiles=[(xs, 0)], name="fox_out")[0]
    outs = (kvp.reshape(1, b, t, 2, nh, HEAD_DIM), lfp.reshape(1, b, t, nh),
            kvs[:n_s].reshape(1, n_s, 1, 2, nh, HEAD_DIM), logf.reshape(1, n_s, 1, nh))
    return xp, xs, outs


def _hg_proj(h, w_in, lb, tm):
    d = h.shape[1]
    tn = 512
    nb = d // tn

    def q_epi(accs, vecs, tiles, j):
        a = accs[0]
        return [a * jax.nn.sigmoid(a)]

    def f_epi(accs, vecs, tiles, j):
        f = vecs[0] + (1.0 - vecs[0]) * jax.nn.sigmoid(accs[0])
        return [1.0 - f, jnp.log(f)]

    def id_epi(accs, vecs, tiles, j):
        return [accs[0]]

    def gate_epi(accs, vecs, tiles, j):
        return [jax.nn.sigmoid(accs[0])]

    q = matmul(h, [(w_in, 0)], [(d, F32, 0)], q_epi, tm=tm, tn=tn, nj=nb, name="hg_q")[0]
    k, logf = matmul(h, [(w_in, nb)], [(d, F32, 0), (d, F32, 0)], f_epi, tm=tm, tn=tn, nj=nb,
                     vecs=[(_vec(lb), 0)], name="hg_f")
    v = matmul(h, [(w_in, 2 * nb)], [(d, F32, 0)], id_epi, tm=tm, tn=tn, nj=nb, name="hg_v")[0]
    gate = matmul(h, [(w_in, 3 * nb)], [(d, F32, 0)], gate_epi, tm=tm, tn=tn, nj=nb, name="hg_gate")[0]
    return q, k, v, logf, gate


def _gla_chunk(S, q, k, v, logf):
    C = q.shape[1]
    b = jnp.cumsum(logf, axis=1)
    o_inter = jnp.einsum('bchk,bhkv->bchv', q * jnp.exp(b), S)
    causal = jnp.tril(jnp.ones((C, C), dtype=bool))[None, :, :, None, None]
    dec = jnp.where(causal, jnp.exp(jnp.minimum(b[:, :, None] - b[:, None, :], 0.0)), 0.0)
    att = jnp.einsum('bthk,bshk,btshk->bhts', q, k, dec)
    o_intra = jnp.einsum('bhts,bshv->bthv', att, v)
    b_last = b[:, -1]
    S_new = jnp.exp(b_last)[..., None] * S + jnp.einsum('bshk,bshv->bhkv', k * jnp.exp(b_last[:, None] - b), v)
    return S_new, o_inter + o_intra


def _hg_post(o, gate, on):
    y = o * lax.rsqrt(jnp.mean(o * o, axis=-1, keepdims=True) + EPS) * on
    return (y.reshape(gate.shape) * gate).astype(BF16)


def hgrn_layer(xp, xs, hp, hs, b, t, n_s, state, w_in, lb, on, w_out):
    n_p, d = hp.shape
    nh = d // HEAD_DIM
    tmp = _row_tile(n_p, 1024)
    q, k, v, logf, gate = _hg_proj(hp, w_in, lb, tmp)
    nc = t // HG_CHUNK

    def chunks(a_):
        return jnp.moveaxis(a_.reshape(b, nc, HG_CHUNK, nh, HEAD_DIM), 1, 0)

    s0 = jnp.zeros((b, nh, HEAD_DIM, HEAD_DIM), F32)
    s_fin, o = lax.scan(lambda S, xs_: _gla_chunk(S, *xs_), s0, (chunks(q), chunks(k), chunks(v), chunks(logf)))
    o = jnp.moveaxis(o, 0, 1).reshape(n_p, nh, HEAD_DIM)
    og = _hg_post(o, gate, on)
    xp = matmul(og, [(w_out, 0)], [(d, F32, 0)], _resid_epi, tm=tmp, tn=512, nj=d // 512, tiles=[(xp, 0)], name="hg_out")[0]
    sr = hs.shape[0]
    q, k, v, logf, gate = _hg_proj(hs, w_in, lb, sr)

    def one(a_):
        return a_[:n_s].reshape(n_s, 1, nh, HEAD_DIM)

    s_new, o = _gla_chunk(state.astype(F32), one(q), one(k), one(v), one(logf))
    ogs = _hg_post(o.reshape(n_s, nh, HEAD_DIM), gate[:n_s], on)
    ogs = jnp.pad(ogs, ((0, sr - n_s), (0, 0)))
    xs = matmul(ogs, [(w_out, 0)], [(d, F32, 0)], _resid_epi, tm=sr, tn=512, nj=d // 512, tiles=[(xs, 0)], name="hg_out")[0]
    return xp, xs, (s_fin[None], s_new[None])


def _s5_params(a_re, a_im, log_dt, b_re, b_im, c_re, c_im):
    A = lax.complex(a_re.astype(F32), a_im.astype(F32))
    dt = jnp.exp(log_dt.astype(F32))[:, None]
    Ab = jnp.exp(A * dt)
    Bb = ((Ab - 1.0) / A)[..., None] * lax.complex(b_re.astype(F32), b_im.astype(F32))
    Cc = lax.complex(c_re.astype(F32), c_im.astype(F32))
    return Ab, Bb, Cc


def _s5_chunk(h_prev, u, Ab, Bb, Cc):
    bu = jnp.einsum('gpc,blgc->blgp', Bb, u.astype(jnp.complex64))
    a = jnp.broadcast_to(Ab, bu.shape)

    def comb(x, y):
        return (x[0] * y[0], y[0] * x[1] + y[1])

    acum, xloc = lax.associative_scan(comb, (a, bu), axis=1)
    x = xloc + acum * h_prev[:, None]
    y = jnp.einsum('gcp,blgp->blgc', Cc, x).real
    return x[:, -1], y


def _glu_epi(accs, vecs, tiles, j):
    return [accs[0] * jax.nn.sigmoid(accs[1]) + tiles[0]]


def s5_layer(xp, xs, up, us, b, t, n_s, st, a_re, a_im, log_dt, b_re, b_im, c_re, c_im, dvec, w_glu):
    n_p, d = up.shape
    ng = d // S5_CH
    Ab, Bb, Cc = _s5_params(a_re, a_im, log_dt, b_re, b_im, c_re, c_im)
    u = up.reshape(b, t, ng, S5_CH)
    lc = min(S5_CHUNK, t)
    nc = t // lc
    uc = jnp.moveaxis(u.reshape(b, nc, lc, ng, S5_CH), 1, 0)
    h0 = jnp.zeros((b, ng, Ab.shape[-1]), jnp.complex64)
    h_fin, y = lax.scan(lambda hs_, uu: _s5_chunk(hs_, uu, Ab, Bb, Cc), h0, uc)
    y = jnp.moveaxis(y, 0, 1).reshape(n_p, d)
    z = jax.nn.gelu(y + dvec.astype(F32) * up).astype(BF16)
    tmp = _row_tile(n_p, 1024)
    nb = d // 256
    xp = matmul(z, [(w_glu, 0), (w_glu, nb)], [(d, F32, 0)], _glu_epi, tm=tmp, tn=256, nj=nb, tiles=[(xp, 0)], name="s5_glu")[0]
    sp = jnp.stack([h_fin.real, h_fin.imag], axis=-1)
    sr = us.shape[0]
    u_s = us[:n_s].reshape(n_s, 1, ng, S5_CH)
    h_prev = lax.complex(st[..., 0].astype(F32), st[..., 1].astype(F32))
    h_new, y = _s5_chunk(h_prev, u_s, Ab, Bb, Cc)
    z = jax.nn.gelu(y.reshape(n_s, d) + dvec.astype(F32) * us[:n_s]).astype(BF16)
    z = jnp.pad(z, ((0, sr - n_s), (0, 0)))
    xs = matmul(z, [(w_glu, 0), (w_glu, nb)], [(d, F32, 0)], _glu_epi, tm=sr, tn=256, nj=nb, tiles=[(xs, 0)], name="s5_glu")[0]
    ss = jnp.stack([h_new.real, h_new.imag], axis=-1)
    return xp, xs, (sp[None], ss[None])


def _t5_bucket(dist):
    n = jnp.maximum(dist, 0)
    exact = N_BUCKETS // 2
    lg = jnp.log(jnp.maximum(n, exact).astype(F32) / exact) / math.log(MAX_DIST / exact)
    large = jnp.minimum(exact + (lg * (N_BUCKETS - exact)).astype(jnp.int32), N_BUCKETS - 1)
    return jnp.where(n < exact, n, large)


def _rms(x, g):
    return x * lax.rsqrt(jnp.mean(x * x, axis=-1, keepdims=True) + EPS) * g


def _nsa_proj(h, w_in, qn, kn, tm, nkv):
    d = h.shape[1]
    kvw = nkv * HEAD_DIM
    tn = 512
    nb = d // tn
    assert kvw == tn

    def q_epi(accs, vecs, tiles, j):
        return [_head_rms(accs[0], vecs[0])]

    def rows_epi(accs, vecs, tiles, j):
        return [jnp.where(j == 2, _head_rms(accs[0], vecs[0]), accs[0])]

    def win_epi(accs, vecs, tiles, j):
        return [jnp.where(j == 0, _head_rms(accs[0], vecs[0]), accs[0])]

    def gate_epi(accs, vecs, tiles, j):
        return [jax.nn.sigmoid(accs[0])]

    q16 = matmul(h, [(w_in, 0)], [(d, BF16, 0)], q_epi, tm=tm, tn=tn, nj=nb, vecs=[(jnp.tile(_vec(qn), (1, d // HEAD_DIM)), 0)], name="nsa_q")[0]
    rows = matmul(h, [(w_in, nb)], [(4 * kvw, F32, 0)], rows_epi, tm=tm, tn=tn, nj=4,
                  vecs=[(jnp.tile(_vec(kn[1]), (1, 4 * nkv)), 0)], name="nsa_rows")[0]
    wrows = matmul(h, [(w_in, nb + 4)], [(2 * kvw, F32, 0)], win_epi, tm=tm, tn=tn, nj=2,
                   vecs=[(jnp.tile(_vec(kn[2]), (1, 2 * nkv)), 0)], name="nsa_wrows")[0]
    nhq = d // HEAD_DIM
    gates = matmul(h, [(w_in, (d + 6 * kvw) // LANES)], [(LANES, F32, 0)], gate_epi, tm=tm, tn=LANES, nj=1, name="nsa_gates")[0][:, :3 * nhq]
    return q16, gates, rows, wrows


def _bias_heads(table, dist, nkv, grp):
    bb = table[_t5_bucket(dist)].astype(F32).reshape(dist.shape + (nkv, grp))
    return jnp.transpose(bb, (2, 3, 0, 1))


def _bias_group(table, dist, nkv, grp):
    tg = jnp.transpose(table.reshape(N_BUCKETS, nkv, grp), (1, 0, 2))
    hi = jnp.arange(nkv)[None, :, None, None]
    return jnp.moveaxis(tg[hi, _t5_bucket(dist)].astype(F32), -1, 2)


def _nsa_compress(rows, pos, w, w1, w2):
    B, T = rows.shape[0], rows.shape[1]
    blk = rows.reshape(B, T // CMP_BLK, CMP_BLK, rows.shape[2], HEAD_DIM) + pos[:, None, :]
    pooled = jnp.einsum('bnjhd,j->bnhd', blk, w)
    return jax.nn.silu(pooled @ w1) @ w2


def _nsa_cmp_slc(q, rows, q_pos, cmp_pos, cmp_w, cmp_w1, cmp_w2, kn_cmp, table):
    B, Tq = q.shape[0], q.shape[1]
    nkv = rows.shape[3]
    nh = q.shape[2]
    grp = nh // nkv
    T = rows.shape[1]
    nb = -(-T // CMP_BLK)
    rows = jnp.pad(rows, ((0, 0), (0, nb * CMP_BLK - T), (0, 0), (0, 0), (0, 0)))
    kc = _rms(_nsa_compress(rows[:, :, 0], cmp_pos[0], cmp_w[0], cmp_w1[0], cmp_w2[0]), kn_cmp)
    vc = _nsa_compress(rows[:, :, 1], cmp_pos[1], cmp_w[1], cmp_w1[1], cmp_w2[1])
    qg = q.reshape(B, Tq, nkv, grp, HEAD_DIM)
    scale = HEAD_DIM ** -0.5
    blk_idx = jnp.arange(nb)
    dist_c = q_pos[:, None] - (blk_idx * CMP_BLK + CMP_BLK - 1)[None, :]
    vis = dist_c >= 0
    s_c = jnp.einsum('bqhgd,bnhd->bhgqn', qg, kc).astype(F32) * scale + _bias_heads(table, dist_c, nkv, grp)
    p_c = jax.nn.softmax(jnp.where(vis, s_c, NEG), axis=-1) * vis
    o_cmp = jnp.einsum('bhgqn,bnhd->bqhgd', p_c, vc).reshape(B, Tq, nh, HEAD_DIM)
    imp = p_c.sum(axis=2)
    cur = (q_pos // SEL_BLK)[:, None]
    cand = blk_idx[None, :] <= cur
    forced = (blk_idx[None, :] == 0) | (blk_idx[None, :] == cur) | (blk_idx[None, :] == cur - 1)
    score = jnp.where(cand, jnp.where(forced, FORCE, imp), -1.0)
    n_sel = min(N_SEL, nb)
    top_v, top_i = lax.top_k(score, n_sel)
    valid = top_v >= 0.0
    ks = jnp.moveaxis(rows[:, :, 2], 2, 1)
    vs = jnp.moveaxis(rows[:, :, 3], 2, 1)
    qblk = 32 if Tq % 32 == 0 else Tq
    nqb = Tq // qblk
    bi = jnp.arange(B)[:, None, None, None]
    hi = jnp.arange(nkv)[None, :, None, None]

    def sel_block(args):
        qb, ib, vb, pb = args
        kpos = (ib[..., None] * SEL_BLK + jnp.arange(SEL_BLK)).reshape(B, nkv, qblk, n_sel * SEL_BLK)
        kmask = jnp.repeat(vb, SEL_BLK, axis=-1)
        kg = ks[bi, hi, kpos]
        vg = vs[bi, hi, kpos]
        dist = pb[None, None, :, None] - kpos
        mask = (kmask & (dist >= 0))[:, :, None]
        s = jnp.einsum('bqhgd,bhqnd->bhgqn', qb, kg).astype(F32) * scale + _bias_group(table, dist, nkv, grp)
        p = jax.nn.softmax(jnp.where(mask, s, NEG), axis=-1)
        return jnp.einsum('bhgqn,bhqnd->bqhgd', p, vg)

    qs = jnp.moveaxis(qg.reshape(B, nqb, qblk, nkv, grp, HEAD_DIM), 1, 0)
    i_s = jnp.moveaxis(top_i.reshape(B, nkv, nqb, qblk, n_sel), 2, 0)
    v_s = jnp.moveaxis(valid.reshape(B, nkv, nqb, qblk, n_sel), 2, 0)
    o_slc = lax.map(sel_block, (qs, i_s, v_s, q_pos.reshape(nqb, qblk)))
    o_slc = jnp.moveaxis(o_slc, 0, 1).reshape(B, Tq, nh, HEAD_DIM)
    return o_cmp, o_slc


def _win_attend(q, kw, vw, q_pos, k_pos, table):
    B, Q = q.shape[0], q.shape[1]
    nkv = kw.shape[2]
    grp = q.shape[2] // nkv
    qg = q.reshape(B, Q, nkv, grp, HEAD_DIM)
    dist = q_pos[:, None] - k_pos[None, :]
    mask = (dist >= 0) & (dist <= WINDOW) & (k_pos[None, :] >= 0)
    s = jnp.einsum('bqhgd,bkhd->bhgqk', qg, kw).astype(F32) * HEAD_DIM ** -0.5 + _bias_heads(table, dist, nkv, grp)
    p = jax.nn.softmax(jnp.where(mask, s, NEG), axis=-1)
    return jnp.einsum('bhgqk,bkhd->bqhgd', p, vw).reshape(B, Q, q.shape[2], HEAD_DIM)


def _nsa_combine(gates, o_cmp, o_slc, o_win):
    n = gates.shape[0]
    nh = o_cmp.shape[-2]
    g = gates.reshape(n, 3, nh)
    o = (g[:, 0, :, None] * o_cmp.reshape(n, nh, HEAD_DIM) + g[:, 1, :, None] * o_slc.reshape(n, nh, HEAD_DIM)
         + g[:, 2, :, None] * o_win.reshape(n, nh, HEAD_DIM))
    return o.reshape(n, nh * HEAD_DIM).astype(BF16)


def nsa_layer(xp, xs, hp, hs, b, t, n_s, pool, win_buf, page_table, w_in, qn, kn, cmp_pos, cmp_w, cmp_w1, cmp_w2, w_out, table):
    n_p, d = hp.shape
    nh = d // HEAD_DIM
    nkv = pool.shape[-2]
    tmp = _row_tile(n_p, 1024)
    q16, gates, rows, wrows = _nsa_proj(hp, w_in, qn, kn, tmp, nkv)
    q = q16.astype(F32).reshape(b, t, nh, HEAD_DIM)
    rows5 = rows.reshape(b, t, 4, nkv, HEAD_DIM)
    wrows5 = wrows.reshape(b, t, 2, nkv, HEAD_DIM)
    o_cmp, o_slc = _nsa_cmp_slc(q, rows5, jnp.arange(t), cmp_pos, cmp_w, cmp_w1, cmp_w2, kn[0], table)
    padded = jnp.pad(wrows5, ((0, 0), (WINDOW, 0), (0, 0), (0, 0), (0, 0)))
    qb_ = min(128, t)
    nqb = t // qb_

    def win_block(args):
        i, qq = args
        s0 = i * qb_
        kv = lax.dynamic_slice_in_dim(padded, s0, WINDOW + qb_, axis=1)
        return _win_attend(qq, kv[:, :, 0], kv[:, :, 1], s0 + jnp.arange(qb_), s0 - WINDOW + jnp.arange(WINDOW + qb_), table)

    qb = jnp.moveaxis(q.reshape(b, nqb, qb_, nh, HEAD_DIM), 1, 0)
    o_win = jnp.moveaxis(lax.map(win_block, (jnp.arange(nqb), qb)), 0, 1).reshape(b, t, nh, HEAD_DIM)
    og = _nsa_combine(gates, o_cmp, o_slc, o_win)
    xp = matmul(og, [(w_out, 0)], [(d, F32, 0)], _resid_epi, tm=tmp, tn=512, nj=d // 512, tiles=[(xp, 0)], name="nsa_out")[0]
    lbw = min(WINDOW, t)
    out_p = (rows5[None], wrows5[:, t - lbw:][None])
    sr = hs.shape[0]
    q16, gates, rows, wrows = _nsa_proj(hs, w_in, qn, kn, sr, nkv)
    q = q16[:n_s].astype(F32).reshape(n_s, 1, nh, HEAD_DIM)
    rows_new = rows[:n_s].reshape(n_s, 1, 4, nkv, HEAD_DIM)
    wrows_new = wrows[:n_s].reshape(n_s, 1, 2, nkv, HEAD_DIM)
    g = pool[page_table]
    past = g.reshape((g.shape[0], g.shape[1] * g.shape[2]) + g.shape[3:])
    p_len = past.shape[1]
    q_pos = p_len + jnp.arange(1)
    rows_all = jnp.concatenate([past, rows_new], axis=1)
    o_cmp, o_slc = _nsa_cmp_slc(q, rows_all, q_pos, cmp_pos, cmp_w, cmp_w1, cmp_w2, kn[0], table)
    lb_ = win_buf.shape[1]
    kw = jnp.concatenate([win_buf, wrows_new], axis=1)
    o_win = _win_attend(q, kw[:, :, 0], kw[:, :, 1], q_pos, p_len - lb_ + jnp.arange(lb_ + 1), table)
    ogs = _nsa_combine(gates[:n_s], o_cmp, o_slc, o_win)
    ogs = jnp.pad(ogs, ((0, sr - n_s), (0, 0)))
    xs = matmul(ogs, [(w_out, 0)], [(d, F32, 0)], _resid_epi, tm=sr, tn=512, nj=d // 512, tiles=[(xs, 0)], name="nsa_out")[0]
    out_s = (rows_new[None], kw[:, -lb_:][None])
    return xp, xs, out_p + out_s


def kernel(x_prompt, x_sample, cache_fox_kv, cache_fox_logf, state_hgrn, state_s5, cache_nsa_kv, state_nsa_win, page_table, norm_mix, norm_ffn, fox_w_in, fox_b_f, fox_q_norm, fox_k_norm, fox_w_out, hg_w_in, hg_lb, hg_o_norm, hg_w_out, s5_a_re, s5_a_im, s5_log_dt, s5_b_re, s5_b_im, s5_c_re, s5_c_im, s5_d, s5_w_glu, nsa_w_in, nsa_q_norm, nsa_k_norm, nsa_cmp_pos, nsa_cmp_w, nsa_cmp_w1, nsa_cmp_w2, nsa_w_out, rel_bias, ffn_w1, ffn_w3, ffn_w2, moe_router, moe_router_b, moe_w1, moe_w3, moe_w2):
    b, t, d = x_prompt.shape
    n_s = x_sample.shape[0]
    assert x_sample.shape[1] == 1
    depth = norm_mix.shape[0]
    xp = x_prompt.reshape(b * t, d)
    xs = jnp.pad(x_sample.reshape(n_s, d), ((0, SAMPLE_ROWS - n_s), (0, 0)))
    sm = jax.nn.softmax(hg_lb.astype(F32), axis=0)
    lower_bounds = jnp.cumsum(sm, axis=0) - sm[0]
    res = {}
    for i in range(depth):
        j = i // 4
        kind = i % 4
        if kind == 2:
            hp = rms_norm(xp, norm_mix[i], F32)
            hs = rms_norm(xs, norm_mix[i], F32)
        else:
            hp = rms_norm(xp, norm_mix[i])
            hs = rms_norm(xs, norm_mix[i])
        if kind == 0:
            xp, xs, o = fox_layer(xp, xs, hp, hs, b, t, n_s, cache_fox_kv[j], cache_fox_logf[j], page_table,
                                  fox_w_in[j], fox_b_f[j], fox_q_norm[j], fox_k_norm[j], fox_w_out[j])
        elif kind == 1:
            xp, xs, o = hgrn_layer(xp, xs, hp, hs, b, t, n_s, state_hgrn[j], hg_w_in[j], lower_bounds[i],
                                   hg_o_norm[j], hg_w_out[j])
        elif kind == 2:
            xp, xs, o = s5_layer(xp, xs, hp, hs, b, t, n_s, state_s5[j], s5_a_re[j], s5_a_im[j], s5_log_dt[j],
                                 s5_b_re[j], s5_b_im[j], s5_c_re[j], s5_c_im[j], s5_d[j], s5_w_glu[j])
        else:
            xp, xs, o = nsa_layer(xp, xs, hp, hs, b, t, n_s, cache_nsa_kv[j], state_nsa_win[j], page_table,
                                  nsa_w_in[j], nsa_q_norm[j], nsa_k_norm[j], nsa_cmp_pos[j], nsa_cmp_w[j],
                                  nsa_cmp_w1[j], nsa_cmp_w2[j], nsa_w_out[j], rel_bias)
        res[kind] = o
        hp = rms_norm(xp, norm_ffn[i])
        hs = rms_norm(xs, norm_ffn[i])
        f = i // 2
        if i % 2 == 0:
            xp, xs = dense_ffn(xp, xs, hp, hs, ffn_w1[f], ffn_w3[f], ffn_w2[f])
        else:
            xp, xs = moe_ffn(xp, xs, hp, hs, n_s, moe_router[f], moe_router_b[f], moe_w1[f], moe_w3[f], moe_w2[f])
    return (xp.reshape(b, t, d), xs[:n_s].reshape(n_s, 1, d)) + res[0] + res[1] + res[2] + res[3]
```

```python
import functools
import math

import numpy as np
import jax
import jax.numpy as jnp
from jax import lax
from jax.experimental import pallas as pl
from jax.experimental.pallas import tpu as pltpu

F32 = jnp.float32
BF16 = jnp.bfloat16
EPS = 1e-6
NEG = -1e30
FORCE = 1e9

HEAD_DIM = 128
LANES = 128
V7X_VMEM_CAP_MB = 60
SAMPLE_ROWS = 16

HG_CHUNK = 16
S5_CH = 16
S5_CHUNK = 128
CMP_BLK = 64
SEL_BLK = 64
N_SEL = 16
WINDOW = 512
N_BUCKETS = 32
MAX_DIST = 128
TOP_K = 2
MOE_ROWS = 512


def _cparams(sem, vmem_bytes):
    mb = min(V7X_VMEM_CAP_MB, max(16, int(vmem_bytes / (1 << 20)) + 8))
    return pltpu.CompilerParams(dimension_semantics=sem, vmem_limit_bytes=mb << 20)


def _rms_kernel(x_ref, g_ref, o_ref):
    x = x_ref[...]
    y = x * lax.rsqrt(jnp.mean(x * x, axis=-1, keepdims=True) + EPS)
    o_ref[...] = (y * g_ref[...]).astype(o_ref.dtype)


def rms_norm(x, g, out_dtype=BF16):
    n, d = x.shape
    tm = min(n, 512)
    return pl.pallas_call(
        _rms_kernel,
        grid=(n // tm,),
        in_specs=[pl.BlockSpec((tm, d), lambda i: (i, 0)),
                  pl.BlockSpec((1, d), lambda i: (0, 0))],
        out_specs=pl.BlockSpec((tm, d), lambda i: (i, 0)),
        out_shape=jax.ShapeDtypeStruct((n, d), out_dtype),
        compiler_params=_cparams(("parallel",), 2 * tm * d * (4 + 4)),
        name="rms_norm",
    )(x, g.reshape(1, d).astype(F32))


def _mm_kernel(*refs, nw, nv, nt, no, ni, has_e, epilogue):
    pos = 0
    eid_ref = None
    if has_e:
        eid_ref = refs[0]
        pos = 1
    lhs_ref = refs[pos]
    pos += 1
    w_refs = refs[pos:pos + nw]
    pos += nw
    vec_refs = refs[pos:pos + nv]
    pos += nv
    tile_refs = refs[pos:pos + nt]
    pos += nt
    out_refs = refs[pos:pos + no]
    wc_ref = refs[-1]
    j = pl.program_id(0)
    i = pl.program_id(1)
    if has_e:
        prev = eid_ref[jnp.maximum(i - 1, 0)]
        recast = jnp.logical_or(i == 0, eid_ref[i] != prev)
        active = i < eid_ref[ni]
    else:
        recast = i == 0
        active = None

    @pl.when(recast)
    def _():
        for t in range(nw):
            wc_ref[t] = w_refs[t][...].astype(BF16)

    def compute():
        x = lhs_ref[...]
        accs = [jnp.dot(x, wc_ref[t], preferred_element_type=F32) for t in range(nw)]
        res = epilogue(accs, [v[...] for v in vec_refs], [t[...] for t in tile_refs], j)
        for o, r in zip(out_refs, res):
            o[...] = r.astype(o.dtype)

    if has_e:
        pl.when(active)(compute)

        @pl.when(jnp.logical_not(active))
        def _():
            for o in out_refs:
                o[...] = jnp.zeros(o.shape, o.dtype)
    else:
        compute()


def matmul(lhs, ws, outs, epilogue, *, tm, tn, nj, tk=None, kb=0, vecs=(), tiles=(), eids=None, name="mm"):
    n = lhs.shape[0]
    assert n % tm == 0, (n, tm)
    ni = n // tm
    kdim = ws[0][0].shape[-2]
    tk = tk or kdim
    assert kdim % tk == 0
    has_e = eids is not None
    nw, nv, nt, no = len(ws), len(vecs), len(tiles), len(outs)

    def wspec(arr, off):
        if arr.ndim == 3:
            return pl.BlockSpec((None, tk, tn), lambda j, i, e: (e[i], kb, j + off))
        if has_e:
            return pl.BlockSpec((tk, tn), lambda j, i, e: (kb, j + off))
        return pl.BlockSpec((tk, tn), lambda j, i: (kb, j + off))

    def fix(f):
        return (lambda j, i, e: f(j, i)) if has_e else f

    in_specs = [pl.BlockSpec((tm, tk), fix(lambda j, i: (i, kb)))]
    in_specs += [wspec(a, off) for a, off in ws]
    in_specs += [pl.BlockSpec((1, tn), fix(functools.partial(lambda j, i, off: (0, j + off), off=off))) for _, off in vecs]
    in_specs += [pl.BlockSpec((tm, tn), fix(functools.partial(lambda j, i, off: (i, j + off), off=off))) for _, off in tiles]
    out_specs = [pl.BlockSpec((tm, tn), fix(functools.partial(lambda j, i, off: (i, j + off), off=off))) for _, _, off in outs]
    out_shape = [jax.ShapeDtypeStruct((n, c), dt) for c, dt, _ in outs]
    vmem = (2 * tm * tk * 2 + nw * (2 * tk * tn * 4 + tk * tn * 2)
            + sum(2 * tm * tn * jnp.dtype(dt).itemsize for _, dt, _ in outs)
            + nt * 2 * tm * tn * 4 + (nw + 1) * tm * tn * 4)
    kern = functools.partial(_mm_kernel, nw=nw, nv=nv, nt=nt, no=no, ni=ni, has_e=has_e, epilogue=epilogue)
    scratch = [pltpu.VMEM((nw, tk, tn), BF16)]
    args = [lhs] + [a for a, _ in ws] + [a for a, _ in vecs] + [a for a, _ in tiles]
    if has_e:
        grid_spec = pltpu.PrefetchScalarGridSpec(num_scalar_prefetch=1, grid=(nj, ni), in_specs=in_specs,
                                                 out_specs=out_specs, scratch_shapes=scratch)
        args = [eids] + args
    else:
        grid_spec = pl.GridSpec(grid=(nj, ni), in_specs=in_specs, out_specs=out_specs, scratch_shapes=scratch)
    return pl.pallas_call(kern, grid_spec=grid_spec, out_shape=out_shape,
                          compiler_params=_cparams(("arbitrary", "arbitrary"), vmem), name=name)(*args)


def _row_tile(n, pref):
    return pref if n % pref == 0 else n


def _head_rms(x, g):
    parts = []
    for h in range(x.shape[1] // HEAD_DIM):
        xh = x[:, h * HEAD_DIM:(h + 1) * HEAD_DIM]
        gh = g[:, h * HEAD_DIM:(h + 1) * HEAD_DIM]
        parts.append(xh * lax.rsqrt(jnp.mean(xh * xh, axis=-1, keepdims=True) + EPS) * gh)
    return parts[0] if len(parts) == 1 else jnp.concatenate(parts, axis=-1)


def _vec(v, width=None):
    v = v.reshape(1, -1).astype(F32)
    if width is not None and v.shape[1] < width:
        v = jnp.pad(v, ((0, 0), (0, width - v.shape[1])))
    return v


def _log_sigmoid(x):
    return jnp.minimum(x, 0.0) - jnp.log(1.0 + jnp.exp(-jnp.abs(x)))


def _swiglu_epi(accs, vecs, tiles, j):
    a, b = accs
    return [a * jax.nn.sigmoid(a) * b]


def _resid_epi(accs, vecs, tiles, j):
    return [accs[0] + tiles[0]]


def ffn_up(h, w1, w3, eids=None, tm=1024):
    dff = w1.shape[-1]
    tn = 256
    assert dff % tn == 0
    return matmul(h, [(w1, 0), (w3, 0)], [(dff, BF16, 0)], _swiglu_epi, tm=tm, tn=tn, nj=dff // tn,
                  eids=eids, name="ffn_up")[0]


def ffn_down(hmid, w2, resid, eids=None, tm=512):
    dff, d = w2.shape[-2], w2.shape[-1]
    halves = 2 if dff % (2 * LANES) == 0 and dff > 4096 else 1
    tk = dff // halves
    tn = min(512, d)
    out = resid
    for kb in range(halves):
        out = matmul(hmid, [(w2, 0)], [(d, F32, 0)], _resid_epi, tm=tm, tn=tn, nj=d // tn, tk=tk, kb=kb,
                     tiles=[(out, 0)], eids=eids, name="ffn_down")[0]
    return out


def dense_ffn(xp, xs, hp, hs, w1, w3, w2):
    mp = ffn_up(hp, w1, w3, tm=_row_tile(hp.shape[0], 1024))
    xp = ffn_down(mp, w2, xp, tm=_row_tile(hp.shape[0], 512))
    ms = ffn_up(hs, w1, w3, tm=hs.shape[0])
    xs = ffn_down(ms, w2, xs, tm=hs.shape[0])
    return xp, xs


def _router_epi(accs, vecs, tiles, j):
    return [accs[0] + vecs[0]]


def moe_ffn(xp, xs, hp, hs, n_s, router, router_b, w1, w3, w2):
    n_p, d = hp.shape
    n_e = router.shape[-1]
    h_all = jnp.concatenate([hp, hs[:n_s]], axis=0)
    n = n_p + n_s
    rw = jnp.pad(router, ((0, 0), (0, LANES - n_e)))
    lp = matmul(hp, [(rw, 0)], [(LANES, F32, 0)], _router_epi, tm=_row_tile(n_p, 1024), tn=LANES, nj=1,
                vecs=[(_vec(router_b, LANES), 0)], name="router")[0]
    ls = matmul(hs, [(rw, 0)], [(LANES, F32, 0)], _router_epi, tm=hs.shape[0], tn=LANES, nj=1,
                vecs=[(_vec(router_b, LANES), 0)], name="router")[0]
    logits = jnp.concatenate([lp[:, :n_e], ls[:n_s, :n_e]], axis=0)
    top_l, top_e = lax.top_k(logits, TOP_K)
    gate = jax.nn.softmax(top_l, axis=-1)
    a = n * TOP_K
    bm = MOE_ROWS
    e_flat = top_e.reshape(-1)
    tok = jnp.arange(a) // TOP_K
    order = jnp.argsort(e_flat)
    e_sorted = e_flat[order]
    tok_sorted = tok[order]
    counts = jnp.bincount(e_flat, length=n_e)
    padded = (counts + bm - 1) // bm * bm
    pad_end = jnp.cumsum(padded)
    grp_start = jnp.cumsum(counts) - counts
    dest = (pad_end - padded)[e_sorted] + jnp.arange(a) - grp_start[e_sorted]
    n_blocks = -(-a // bm) + n_e
    blk_e = jnp.minimum(jnp.searchsorted(pad_end, jnp.arange(n_blocks) * bm, side='right'), n_e - 1)
    n_act = pad_end[-1] // bm
    eids = jnp.concatenate([blk_e, n_act[None]]).astype(jnp.int32)
    buf = jnp.zeros((n_blocks * bm, d), h_all.dtype).at[dest].set(h_all[tok_sorted])
    mid = ffn_up(buf, w1, w3, eids=eids, tm=bm)
    zero = jnp.zeros((n_blocks * bm, d), F32)
    out = ffn_down(mid, w2, zero, eids=eids, tm=bm)
    pos = jnp.zeros((a,), jnp.int32).at[order].set(dest.astype(jnp.int32)).reshape(n, TOP_K)
    y = out[pos[:, 0]] * gate[:, 0:1] + out[pos[:, 1]] * gate[:, 1:2]
    xp = xp + y[:n_p]
    xs = xs.at[:n_s].add(y[n_p:])
    return xp, xs


def _fox_attn_kernel(q_ref, k_ref, v_ref, ccol_ref, crow_ref, g_ref, o_ref, *, tq, tk, scale):
    qi = pl.program_id(2)
    q = q_ref[...]
    cq = ccol_ref[...]
    qpos = qi * tq + lax.broadcasted_iota(jnp.int32, (tq, tk), 0)
    koff = lax.broadcasted_iota(jnp.int32, (tq, tk), 1)

    def body(j, carry):
        m, l, acc = carry
        start = pl.multiple_of(j * tk, tk)
        k = k_ref[pl.ds(start, tk), :]
        v = v_ref[pl.ds(start, tk), :]
        s = lax.dot_general(q, k, (((1,), (1,)), ((), ())), preferred_element_type=F32) * scale
        s = s + cq - crow_ref[pl.ds(j, 1), :]
        s = jnp.where(qpos >= koff + j * tk, s, NEG)
        m_new = jnp.maximum(m, jnp.max(s, axis=-1, keepdims=True))
        alpha = jnp.exp(m - m_new)
        p = jnp.exp(s - m_new)
        l = alpha * l + jnp.sum(p, axis=-1, keepdims=True)
        acc = alpha * acc + jnp.dot(p.astype(BF16), v, preferred_element_type=F32)
        return m_new, l, acc

    n_kv = (qi * tq + tq + tk - 1) // tk
    init = (jnp.full((tq, 1), NEG, F32), jnp.zeros((tq, 1), F32), jnp.zeros((tq, HEAD_DIM), F32))
    m, l, acc = lax.fori_loop(0, n_kv, body, init)
    o_ref[...] = (acc / l * g_ref[...]).astype(o_ref.dtype)


def fox_attention(q16, kv16, c, gate, b, t, nh):
    tq = min(256, t)
    tk = tq
    nq = t // tq
    ccol = jnp.transpose(c, (0, 2, 1)).reshape(b, nh, t, 1)
    crow = jnp.transpose(c, (0, 2, 1)).reshape(b, nh, t // tk, tk)
    kern = functools.partial(_fox_attn_kernel, tq=tq, tk=tk, scale=HEAD_DIM ** -0.5)
    return pl.pallas_call(
        kern,
        grid=(b, nh, nq),
        in_specs=[pl.BlockSpec((tq, HEAD_DIM), lambda bi, h, i: (bi * nq + i, h)),
                  pl.BlockSpec((t, HEAD_DIM), lambda bi, h, i: (bi, h)),
                  pl.BlockSpec((t, HEAD_DIM), lambda bi, h, i: (bi, nh + h)),
                  pl.BlockSpec((None, None, tq, 1), lambda bi, h, i: (bi, h, i, 0)),
                  pl.BlockSpec((None, None, t // tk, tk), lambda bi, h, i: (bi, h, 0, 0)),
                  pl.BlockSpec((tq, HEAD_DIM), lambda bi, h, i: (bi * nq + i, h))],
        out_specs=pl.BlockSpec((tq, HEAD_DIM), lambda bi, h, i: (bi * nq + i, h)),
        out_shape=jax.ShapeDtypeStruct(q16.shape, BF16),
        compiler_params=_cparams(("parallel", "parallel", "arbitrary"), 8 << 20),
        name="fox_attention",
    )(q16, kv16, kv16, ccol, crow, gate)


def _fox_proj(h, w_in, b_f, qn, kn, tm):
    d = h.shape[1]
    nh = d // HEAD_DIM
    tn = 512
    nb = d // tn
    qn_v, kn_v = _vec(qn), _vec(kn)

    def q_epi(accs, vecs, tiles, j):
        return [_head_rms(accs[0], vecs[0])]

    def kv_epi(accs, vecs, tiles, j):
        y = jnp.where(j < nb, _head_rms(accs[0], vecs[0]), accs[0])
        return [y, y]

    def gate_epi(accs, vecs, tiles, j):
        return [jax.nn.sigmoid(accs[0])]

    def lf_epi(accs, vecs, tiles, j):
        return [_log_sigmoid(accs[0] + vecs[0])]

    def tiled(v):
        return jnp.tile(v, (1, 2 * d // HEAD_DIM))

    q16 = matmul(h, [(w_in, 0)], [(d, BF16, 0)], q_epi, tm=tm, tn=tn, nj=nb, vecs=[(tiled(qn_v), 0)], name="fox_q")[0]
    kv, kv16 = matmul(h, [(w_in, nb)], [(2 * d, F32, 0), (2 * d, BF16, 0)], kv_epi, tm=tm, tn=tn, nj=2 * nb,
                      vecs=[(tiled(kn_v), 0)], name="fox_kv")
    gate = matmul(h, [(w_in, 3 * nb)], [(d, F32, 0)], gate_epi, tm=tm, tn=tn, nj=nb, name="fox_gate")[0]
    lf = matmul(h, [(w_in, 4 * d // LANES)], [(LANES, F32, 0)], lf_epi, tm=tm, tn=LANES, nj=1,
                vecs=[(_vec(b_f, LANES), 0)], name="fox_logf")[0][:, :nh]
    return q16, kv, kv16, gate, lf


def _fox_scores(q, k, cq, ck):
    s = jnp.einsum('bqhd,bkhd->bhqk', q, k).astype(F32) * HEAD_DIM ** -0.5
    return s + jnp.swapaxes(cq, 1, 2)[..., :, None] - jnp.swapaxes(ck, 1, 2)[..., None, :]


def fox_layer(xp, xs, hp, hs, b, t, n_s, kv_pool, lf_pool, page_table, w_in, b_f, qn, kn, w_out):
    n_p, d = hp.shape
    nh = d // HEAD_DIM
    tmp = _row_tile(n_p, 1024)
    q16, kvp, kv16, gate, lf = _fox_proj(hp, w_in, b_f, qn, kn, tmp)
    lfp = lf.reshape(b, t, nh)
    c = jnp.cumsum(lfp, axis=1)
    og = fox_attention(q16, kv16, c, gate, b, t, nh)
    xp = matmul(og, [(w_out, 0)], [(d, F32, 0)], _resid_epi, tm=tmp, tn=512, nj=d // 512, tiles=[(xp, 0)], name="fox_out")[0]
    sr = hs.shape[0]
    q16s, kvs, _, gates, lfs = _fox_proj(hs, w_in, b_f, qn, kn, sr)
    q = q16s[:n_s].astype(F32).reshape(n_s, 1, nh, HEAD_DIM)
    k = kvs[:n_s, :d].reshape(n_s, 1, nh, HEAD_DIM)
    v = kvs[:n_s, d:].reshape(n_s, 1, nh, HEAD_DIM)
    logf = lfs[:n_s].reshape(n_s, 1, nh)
    g = kv_pool[page_table]
    kv_past = g.reshape((g.shape[0], g.shape[1] * g.shape[2]) + g.shape[3:])
    g = lf_pool[page_table]
    lf_past = g.reshape((g.shape[0], g.shape[1] * g.shape[2]) + g.shape[3:]).astype(F32)
    p_len = kv_past.shape[1]
    c_past = jnp.cumsum(lf_past, axis=1)
    c_new = c_past[:, -1:] + jnp.cumsum(logf, axis=1)
    s_past = _fox_scores(q, kv_past[:, :, 0], c_new, c_past)
    s_new = _fox_scores(q, k, c_new, c_new)
    p = jax.nn.softmax(jnp.concatenate([s_past, s_new], axis=-1), axis=-1)
    o = (jnp.einsum('bhqk,bkhd->bqhd', p[..., :p_len], kv_past[:, :, 1])
         + jnp.einsum('bhqk,bkhd->bqhd', p[..., p_len:], v))
    ogs = (o.reshape(n_s, d) * gates[:n_s]).astype(BF16)
    ogs = jnp.pad(ogs, ((0, sr - n_s), (0, 0)))
    xs = matmul(ogs, [(w_out, 0)], [(d, F32, 0)], _resid_epi, tm=sr, tn=512, nj=d // 512, tiles=[(xs, 0)], name="fox_out")[0]
    outs = (kvp.reshape(1, b, t, 2, nh, HEAD_DIM), lfp.reshape(1, b, t, nh),
            kvs[:n_s].reshape(1, n_s, 1, 2, nh, HEAD_DIM), logf.reshape(1, n_s, 1, nh))
    return xp, xs, outs


def _hg_proj(h, w_in, lb, tm):
    d = h.shape[1]
    tn = 512
    nb = d // tn

    def q_epi(accs, vecs, tiles, j):
        a = accs[0]
        return [a * jax.nn.sigmoid(a)]

    def f_epi(accs, vecs, tiles, j):
        f = vecs[0] + (1.0 - vecs[0]) * jax.nn.sigmoid(accs[0])
        return [1.0 - f, jnp.log(f)]

    def id_epi(accs, vecs, tiles, j):
        return [accs[0]]

    def gate_epi(accs, vecs, tiles, j):
        return [jax.nn.sigmoid(accs[0])]

    q = matmul(h, [(w_in, 0)], [(d, F32, 0)], q_epi, tm=tm, tn=tn, nj=nb, name="hg_q")[0]
    k, logf = matmul(h, [(w_in, nb)], [(d, F32, 0), (d, F32, 0)], f_epi, tm=tm, tn=tn, nj=nb,
                     vecs=[(_vec(lb), 0)], name="hg_f")
    v = matmul(h, [(w_in, 2 * nb)], [(d, F32, 0)], id_epi, tm=tm, tn=tn, nj=nb, name="hg_v")[0]
    gate = matmul(h, [(w_in, 3 * nb)], [(d, F32, 0)], gate_epi, tm=tm, tn=tn, nj=nb, name="hg_gate")[0]
    return q, k, v, logf, gate


def _hgrn_kernel(q_ref, k_ref, v_ref, lf_ref, gate_ref, on_ref, s0_ref, o_ref, sf_ref, st_s, *, hb, nchunk, chunk):
    tb = pl.program_id(2)

    @pl.when(tb == 0)
    def _():
        st_s[...] = s0_ref[...]

    lf = lf_ref[...]
    row = lax.broadcasted_iota(jnp.int32, lf.shape, 0) % chunk
    b = lf
    sh = 1
    while sh < chunk:
        b = b + jnp.where(row >= sh, pltpu.roll(b, sh, 0), 0.0)
        sh *= 2
    k = k_ref[...]
    v = v_ref[...]
    qd = q_ref[...] * jnp.exp(b)
    kd = k * jnp.exp(-b)
    tri = lax.broadcasted_iota(jnp.int32, (chunk, chunk), 0) >= lax.broadcasted_iota(jnp.int32, (chunk, chunk), 1)
    on = on_ref[...]
    for h in range(hb):
        st = st_s[h]
        cs = slice(h * HEAD_DIM, (h + 1) * HEAD_DIM)
        for c in range(nchunk):
            rs = slice(c * chunk, (c + 1) * chunk)
            bc = b[rs, cs]
            bl = bc[chunk - 1:chunk, :]
            qdc = qd[rs, cs].astype(BF16)
            kdc = kd[rs, cs].astype(BF16)
            vc = v[rs, cs].astype(BF16)
            k2 = (k[rs, cs] * jnp.exp(bl - bc)).astype(BF16)
            att = lax.dot_general(qdc, kdc, (((1,), (1,)), ((), ())), preferred_element_type=F32)
            att = jnp.where(tri, att, 0.0)
            o = lax.dot_general(qdc, st.astype(BF16), (((1,), (1,)), ((), ())), preferred_element_type=F32)
            o = o + jnp.dot(att.astype(BF16), vc, preferred_element_type=F32)
            st = st * jnp.exp(bl) + lax.dot_general(vc, k2, (((0,), (0,)), ((), ())), preferred_element_type=F32)
            y = o * lax.rsqrt(jnp.mean(o * o, axis=-1, keepdims=True) + EPS) * on
            o_ref[rs, cs] = (y * gate_ref[rs, cs]).astype(o_ref.dtype)
        st_s[h] = st

    @pl.when(tb == pl.num_programs(2) - 1)
    def _():
        sf_ref[...] = st_s[...]


def hgrn_scan(q, k, v, logf, gate, on, s0t, b, t):
    d = q.shape[1]
    nh = d // HEAD_DIM
    hb = 2
    tb = min(256, t)
    ntb = t // tb
    kern = functools.partial(_hgrn_kernel, hb=hb, nchunk=tb // HG_CHUNK, chunk=HG_CHUNK)
    tile = pl.BlockSpec((tb, hb * HEAD_DIM), lambda bi, h, i: (bi * ntb + i, h))
    st_spec = pl.BlockSpec((None, hb, HEAD_DIM, HEAD_DIM), lambda bi, h, i: (bi, h, 0, 0))
    return pl.pallas_call(
        kern,
        grid=(b, nh // hb, ntb),
        in_specs=[tile, tile, tile, tile, tile, pl.BlockSpec((1, HEAD_DIM), lambda bi, h, i: (0, 0)), st_spec],
        out_specs=[tile, st_spec],
        out_shape=[jax.ShapeDtypeStruct(q.shape, BF16), jax.ShapeDtypeStruct(s0t.shape, F32)],
        scratch_shapes=[pltpu.VMEM((hb, HEAD_DIM, HEAD_DIM), F32)],
        compiler_params=_cparams(("parallel", "parallel", "arbitrary"), 16 << 20),
        name="hgrn_scan",
    )(q, k, v, logf, gate, on.reshape(1, HEAD_DIM).astype(F32), s0t)


def hgrn_layer(xp, xs, hp, hs, b, t, n_s, state, w_in, lb, on, w_out):
    n_p, d = hp.shape
    nh = d // HEAD_DIM
    tmp = _row_tile(n_p, 1024)
    q, k, v, logf, gate = _hg_proj(hp, w_in, lb, tmp)
    s0 = jnp.zeros((b, nh, HEAD_DIM, HEAD_DIM), F32)
    og, sft = hgrn_scan(q, k, v, logf, gate, on, s0, b, t)
    xp = matmul(og, [(w_out, 0)], [(d, F32, 0)], _resid_epi, tm=tmp, tn=512, nj=d // 512, tiles=[(xp, 0)], name="hg_out")[0]
    sr = hs.shape[0]
    proj = _hg_proj(hs, w_in, lb, sr)

    def spread(a_):
        return jnp.zeros((n_s, HG_CHUNK, d), a_.dtype).at[:, 0].set(a_[:n_s]).reshape(n_s * HG_CHUNK, d)

    qs, ks, vs, lfs, gs = [spread(a_) for a_ in proj]
    ogs, sst = hgrn_scan(qs, ks, vs, lfs, gs, on, jnp.swapaxes(state.astype(F32), -1, -2), n_s, HG_CHUNK)
    ogs = jnp.pad(ogs.reshape(n_s, HG_CHUNK, d)[:, 0], ((0, sr - n_s), (0, 0)))
    xs = matmul(ogs, [(w_out, 0)], [(d, F32, 0)], _resid_epi, tm=sr, tn=512, nj=d // 512, tiles=[(xs, 0)], name="hg_out")[0]
    return xp, xs, (jnp.swapaxes(sft, -1, -2)[None], jnp.swapaxes(sst, -1, -2)[None])


def _s5_params(a_re, a_im, log_dt, b_re, b_im, c_re, c_im):
    A = lax.complex(a_re.astype(F32), a_im.astype(F32))
    dt = jnp.exp(log_dt.astype(F32))[:, None]
    Ab = jnp.exp(A * dt)
    Bb = ((Ab - 1.0) / A)[..., None] * lax.complex(b_re.astype(F32), b_im.astype(F32))
    Cc = lax.complex(c_re.astype(F32), c_im.astype(F32))
    return Ab, Bb, Cc


def _s5_mats(Ab, Bb, Cc, L):
    hp = lax.Precision.HIGHEST
    g, p = Ab.shape
    c = Bb.shape[-1]
    pw = jnp.cumprod(jnp.concatenate([jnp.ones((1, g, p), Ab.dtype), jnp.broadcast_to(Ab, (L, g, p))], axis=0), axis=0)
    kt = jnp.einsum('gcp,tgp,gpd->tgcd', Cc, pw[:L], Bb, precision=hp).real
    lag = jnp.arange(L)[None, :] - jnp.arange(L)[:, None]
    tm = jnp.where((lag >= 0)[:, :, None, None, None], kt[jnp.maximum(lag, 0)], 0.0)
    tm = jnp.transpose(tm, (2, 0, 4, 1, 3)).reshape(g, L * c, L * c)
    rc = pw[L - 1 - jnp.arange(L)][:, :, :, None] * Bb[None]
    rc = jnp.transpose(rc, (1, 0, 3, 2)).reshape(g, L * c, p)
    oc = jnp.transpose(Cc, (0, 2, 1))[:, :, None, :] * jnp.transpose(pw[1:L + 1], (1, 2, 0))[:, :, :, None]
    oc = oc.reshape(g, p, L * c)
    ab = pw[L]
    return tm, rc.real, rc.imag, oc.real, -oc.imag, ab.real, ab.imag


def _s5_kernel(u_ref, t_ref, rr_ref, ri_ref, or_ref, oi_ref, ar_ref, ai_ref, x0r_ref, x0i_ref,
               y_ref, fr_ref, fi_ref, vr_s, vi_s, xr_s, xi_s, *, gb, nch, bp, cdt):
    for g in range(gb):
        u = u_ref[g].astype(cdt)
        vr_s[g] = jnp.dot(u, rr_ref[g], preferred_element_type=F32)
        vi_s[g] = jnp.dot(u, ri_ref[g], preferred_element_type=F32)
    ar = ar_ref[...]
    ai = ai_ref[...]

    def step(n, carry):
        xr, xi = carry
        r0 = pl.multiple_of(n * bp, bp)
        xr_s[:, pl.ds(r0, bp), :] = xr
        xi_s[:, pl.ds(r0, bp), :] = xi
        vr = vr_s[:, pl.ds(r0, bp), :]
        vi = vi_s[:, pl.ds(r0, bp), :]
        return ar * xr - ai * xi + vr, ar * xi + ai * xr + vi

    xr, xi = lax.fori_loop(0, nch, step, (x0r_ref[...], x0i_ref[...]))
    fr_ref[...] = xr
    fi_ref[...] = xi
    for g in range(gb):
        u = u_ref[g].astype(cdt)
        y = jnp.dot(u, t_ref[g], preferred_element_type=F32)
        y = y + jnp.dot(xr_s[g].astype(cdt), or_ref[g], preferred_element_type=F32)
        y = y + jnp.dot(xi_s[g].astype(cdt), oi_ref[g], preferred_element_type=F32)
        y_ref[g] = y


def s5_scan(u, x0, Ab, Bb, Cc, L, cdt):
    bq, t, ng, c = u.shape
    p = Ab.shape[-1]
    bp = -(-bq // 8) * 8
    nch = t // L
    w = L * c
    rows = nch * bp
    mats = _s5_mats(Ab, Bb, Cc, L)
    tm, rr, ri, orr, oi = [m.astype(cdt) for m in mats[:5]]
    ar, ai = [m.reshape(ng, 1, p) for m in mats[5:]]
    up = jnp.pad(u, ((0, bp - bq), (0, 0), (0, 0), (0, 0)))
    ug = jnp.transpose(up.reshape(bp, nch, L, ng, c), (3, 1, 0, 2, 4)).reshape(ng, rows, w)
    x0p = jnp.pad(x0.astype(F32), ((0, bp - bq), (0, 0), (0, 0), (0, 0)))
    x0r = jnp.transpose(x0p[..., 0], (1, 0, 2))
    x0i = jnp.transpose(x0p[..., 1], (1, 0, 2))
    gb = 4
    kern = functools.partial(_s5_kernel, gb=gb, nch=nch, bp=bp, cdt=cdt)

    def spec(*shape):
        return pl.BlockSpec((gb,) + shape, lambda i: (i,) + (0,) * len(shape))

    y, fr, fi = pl.pallas_call(
        kern,
        grid=(ng // gb,),
        in_specs=[spec(rows, w), spec(w, w), spec(w, p), spec(w, p), spec(p, w), spec(p, w),
                  spec(1, p), spec(1, p), spec(bp, p), spec(bp, p)],
        out_specs=[spec(rows, w), spec(bp, p), spec(bp, p)],
        out_shape=[jax.ShapeDtypeStruct((ng, rows, w), F32), jax.ShapeDtypeStruct((ng, bp, p), F32),
                   jax.ShapeDtypeStruct((ng, bp, p), F32)],
        scratch_shapes=[pltpu.VMEM((gb, rows, p), F32)] * 4,
        compiler_params=_cparams(("parallel",), gb * rows * (4 * w * 4 + 4 * LANES * 4)),
        name="s5_scan",
    )(ug, tm, rr, ri, orr, oi, ar, ai, x0r, x0i)
    y = jnp.transpose(y.reshape(ng, nch, bp, L, c), (2, 1, 3, 0, 4)).reshape(bp, t, ng, c)[:bq]
    xf = jnp.stack([jnp.transpose(fr, (1, 0, 2)), jnp.transpose(fi, (1, 0, 2))], axis=-1)[:bq]
    return y, xf


def _s5_act_kernel(y_ref, u_ref, d_ref, o_ref):
    o_ref[...] = jax.nn.gelu(y_ref[...] + d_ref[...] * u_ref[...]).astype(o_ref.dtype)


def s5_act(y, u, dvec):
    n, d = y.shape
    tm = min(n, 512)
    tile = pl.BlockSpec((tm, d), lambda i: (i, 0))
    return pl.pallas_call(
        _s5_act_kernel, grid=(n // tm,),
        in_specs=[tile, tile, pl.BlockSpec((1, d), lambda i: (0, 0))], out_specs=tile,
        out_shape=jax.ShapeDtypeStruct((n, d), BF16),
        compiler_params=_cparams(("parallel",), 2 * tm * d * 10), name="s5_act",
    )(y, u, dvec.reshape(1, d).astype(F32))


def _glu_epi(accs, vecs, tiles, j):
    return [accs[0] * jax.nn.sigmoid(accs[1]) + tiles[0]]


def s5_layer(xp, xs, up, us, b, t, n_s, st, a_re, a_im, log_dt, b_re, b_im, c_re, c_im, dvec, w_glu):
    n_p, d = up.shape
    ng = d // S5_CH
    Ab, Bb, Cc = _s5_params(a_re, a_im, log_dt, b_re, b_im, c_re, c_im)
    x0 = jnp.zeros((b, ng, Ab.shape[-1], 2), F32)
    y, sp = s5_scan(up.reshape(b, t, ng, S5_CH), x0, Ab, Bb, Cc, 16, BF16)
    z = s5_act(y.reshape(n_p, d), up, dvec)
    tmp = _row_tile(n_p, 1024)
    nb = d // 256
    xp = matmul(z, [(w_glu, 0), (w_glu, nb)], [(d, F32, 0)], _glu_epi, tm=tmp, tn=256, nj=nb, tiles=[(xp, 0)], name="s5_glu")[0]
    sr = us.shape[0]
    y, ss = s5_scan(us[:n_s].reshape(n_s, 1, ng, S5_CH), st, Ab, Bb, Cc, 1, F32)
    z = s5_act(jnp.pad(y.reshape(n_s, d), ((0, sr - n_s), (0, 0))), us, dvec)
    xs = matmul(z, [(w_glu, 0), (w_glu, nb)], [(d, F32, 0)], _glu_epi, tm=sr, tn=256, nj=nb, tiles=[(xs, 0)], name="s5_glu")[0]
    return xp, xs, (sp[None], ss[None])


def _t5_bucket(dist):
    n = jnp.maximum(dist, 0)
    exact = N_BUCKETS // 2
    lg = jnp.log(jnp.maximum(n, exact).astype(F32) / exact) / math.log(MAX_DIST / exact)
    large = jnp.minimum(exact + (lg * (N_BUCKETS - exact)).astype(jnp.int32), N_BUCKETS - 1)
    return jnp.where(n < exact, n, large)


def _rms(x, g):
    return x * lax.rsqrt(jnp.mean(x * x, axis=-1, keepdims=True) + EPS) * g


def _nsa_proj(h, w_in, qn, kn, tm, nkv):
    d = h.shape[1]
    kvw = nkv * HEAD_DIM
    tn = 512
    nb = d // tn
    assert kvw == tn

    def q_epi(accs, vecs, tiles, j):
        return [_head_rms(accs[0], vecs[0])]

    def rows_epi(accs, vecs, tiles, j):
        y = jnp.where(j == 2, _head_rms(accs[0], vecs[0]), accs[0])
        return [y, y]

    def win_epi(accs, vecs, tiles, j):
        y = jnp.where(j == 0, _head_rms(accs[0], vecs[0]), accs[0])
        return [y, y]

    def gate_epi(accs, vecs, tiles, j):
        return [jax.nn.sigmoid(accs[0])]

    q16 = matmul(h, [(w_in, 0)], [(d, BF16, 0)], q_epi, tm=tm, tn=tn, nj=nb, vecs=[(jnp.tile(_vec(qn), (1, d // HEAD_DIM)), 0)], name="nsa_q")[0]
    rows, rows16 = matmul(h, [(w_in, nb)], [(4 * kvw, F32, 0), (4 * kvw, BF16, 0)], rows_epi, tm=tm, tn=tn, nj=4,
                          vecs=[(jnp.tile(_vec(kn[1]), (1, 4 * nkv)), 0)], name="nsa_rows")
    wrows, wrows16 = matmul(h, [(w_in, nb + 4)], [(2 * kvw, F32, 0), (2 * kvw, BF16, 0)], win_epi, tm=tm, tn=tn, nj=2,
                            vecs=[(jnp.tile(_vec(kn[2]), (1, 2 * nkv)), 0)], name="nsa_wrows")
    nhq = d // HEAD_DIM
    gates = matmul(h, [(w_in, (d + 6 * kvw) // LANES)], [(LANES, F32, 0)], gate_epi, tm=tm, tn=LANES, nj=1, name="nsa_gates")[0][:, :3 * nhq]
    return q16, gates, rows, wrows, rows16, wrows16


def _bias_heads(table, dist, nkv, grp):
    bb = table[_t5_bucket(dist)].astype(F32).reshape(dist.shape + (nkv, grp))
    return jnp.transpose(bb, (2, 3, 0, 1))


def _bias_group(table, dist, nkv, grp):
    tg = jnp.transpose(table.reshape(N_BUCKETS, nkv, grp), (1, 0, 2))
    hi = jnp.arange(nkv)[None, :, None, None]
    return jnp.moveaxis(tg[hi, _t5_bucket(dist)].astype(F32), -1, 2)


def _nsa_compress(rows, pos, w, w1, w2):
    B, T = rows.shape[0], rows.shape[1]
    blk = rows.reshape(B, T // CMP_BLK, CMP_BLK, rows.shape[2], HEAD_DIM) + pos[:, None, :]
    pooled = jnp.einsum('bnjhd,j->bnhd', blk, w)
    return jax.nn.silu(pooled @ w1) @ w2


def _nsa_cmp_slc(q, rows, q_pos, cmp_pos, cmp_w, cmp_w1, cmp_w2, kn_cmp, table):
    B, Tq = q.shape[0], q.shape[1]
    nkv = rows.shape[3]
    nh = q.shape[2]
    grp = nh // nkv
    T = rows.shape[1]
    nb = -(-T // CMP_BLK)
    rows = jnp.pad(rows, ((0, 0), (0, nb * CMP_BLK - T), (0, 0), (0, 0), (0, 0)))
    kc = _rms(_nsa_compress(rows[:, :, 0], cmp_pos[0], cmp_w[0], cmp_w1[0], cmp_w2[0]), kn_cmp)
    vc = _nsa_compress(rows[:, :, 1], cmp_pos[1], cmp_w[1], cmp_w1[1], cmp_w2[1])
    qg = q.reshape(B, Tq, nkv, grp, HEAD_DIM)
    scale = HEAD_DIM ** -0.5
    blk_idx = jnp.arange(nb)
    dist_c = q_pos[:, None] - (blk_idx * CMP_BLK + CMP_BLK - 1)[None, :]
    vis = dist_c >= 0
    s_c = jnp.einsum('bqhgd,bnhd->bhgqn', qg, kc).astype(F32) * scale + _bias_heads(table, dist_c, nkv, grp)
    p_c = jax.nn.softmax(jnp.where(vis, s_c, NEG), axis=-1) * vis
    o_cmp = jnp.einsum('bhgqn,bnhd->bqhgd', p_c, vc).reshape(B, Tq, nh, HEAD_DIM)
    imp = p_c.sum(axis=2)
    cur = (q_pos // SEL_BLK)[:, None]
    cand = blk_idx[None, :] <= cur
    forced = (blk_idx[None, :] == 0) | (blk_idx[None, :] == cur) | (blk_idx[None, :] == cur - 1)
    score = jnp.where(cand, jnp.where(forced, FORCE, imp), -1.0)
    n_sel = min(N_SEL, nb)
    top_v, top_i = lax.top_k(score, n_sel)
    valid = top_v >= 0.0
    ks = jnp.moveaxis(rows[:, :, 2], 2, 1)
    vs = jnp.moveaxis(rows[:, :, 3], 2, 1)
    qblk = 32 if Tq % 32 == 0 else Tq
    nqb = Tq // qblk
    bi = jnp.arange(B)[:, None, None, None]
    hi = jnp.arange(nkv)[None, :, None, None]

    def sel_block(args):
        qb, ib, vb, pb = args
        kpos = (ib[..., None] * SEL_BLK + jnp.arange(SEL_BLK)).reshape(B, nkv, qblk, n_sel * SEL_BLK)
        kmask = jnp.repeat(vb, SEL_BLK, axis=-1)
        kg = ks[bi, hi, kpos]
        vg = vs[bi, hi, kpos]
        dist = pb[None, None, :, None] - kpos
        mask = (kmask & (dist >= 0))[:, :, None]
        s = jnp.einsum('bqhgd,bhqnd->bhgqn', qb, kg).astype(F32) * scale + _bias_group(table, dist, nkv, grp)
        p = jax.nn.softmax(jnp.where(mask, s, NEG), axis=-1)
        return jnp.einsum('bhgqn,bhqnd->bqhgd', p, vg)

    qs = jnp.moveaxis(qg.reshape(B, nqb, qblk, nkv, grp, HEAD_DIM), 1, 0)
    i_s = jnp.moveaxis(top_i.reshape(B, nkv, nqb, qblk, n_sel), 2, 0)
    v_s = jnp.moveaxis(valid.reshape(B, nkv, nqb, qblk, n_sel), 2, 0)
    o_slc = lax.map(sel_block, (qs, i_s, v_s, q_pos.reshape(nqb, qblk)))
    o_slc = jnp.moveaxis(o_slc, 0, 1).reshape(B, Tq, nh, HEAD_DIM)
    return o_cmp, o_slc


def _win_attend(q, kw, vw, q_pos, k_pos, table):
    B, Q = q.shape[0], q.shape[1]
    nkv = kw.shape[2]
    grp = q.shape[2] // nkv
    qg = q.reshape(B, Q, nkv, grp, HEAD_DIM)
    dist = q_pos[:, None] - k_pos[None, :]
    mask = (dist >= 0) & (dist <= WINDOW) & (k_pos[None, :] >= 0)
    s = jnp.einsum('bqhgd,bkhd->bhgqk', qg, kw).astype(F32) * HEAD_DIM ** -0.5 + _bias_heads(table, dist, nkv, grp)
    p = jax.nn.softmax(jnp.where(mask, s, NEG), axis=-1)
    return jnp.einsum('bhgqk,bkhd->bqhgd', p, vw).reshape(B, Q, q.shape[2], HEAD_DIM)


def _nsa_combine(gates, o_cmp, o_slc, o_win):
    n = gates.shape[0]
    nh = o_cmp.shape[-2]
    g = gates.reshape(n, 3, nh)
    o = (g[:, 0, :, None] * o_cmp.reshape(n, nh, HEAD_DIM) + g[:, 1, :, None] * o_slc.reshape(n, nh, HEAD_DIM)
         + g[:, 2, :, None] * o_win.reshape(n, nh, HEAD_DIM))
    return o.reshape(n, nh * HEAD_DIM).astype(BF16)


def _bucket_table():
    n = np.arange(MAX_DIST + 1)
    exact = N_BUCKETS // 2
    lg = np.log(np.maximum(n, exact).astype(np.float32) / np.float32(exact)) / np.float32(math.log(MAX_DIST / exact))
    large = np.minimum(exact + (lg * (N_BUCKETS - exact)).astype(np.int32), N_BUCKETS - 1)
    return np.where(n < exact, n, large).astype(np.int32)


def _dist_bias(table, dist):
    bt = table.astype(F32)[_bucket_table()]
    return jnp.moveaxis(bt[np.clip(dist, 0, MAX_DIST)], -1, 0)


def _nsa_compress_kernel(rows_ref, pk_ref, pv_ref, posk_ref, posv_ref, w1_ref, w2_ref, kn_ref, kc_ref, vc_ref, *, nkv):
    hp = lax.Precision.HIGHEST
    kvw = nkv * HEAD_DIM
    xk = rows_ref[:, :kvw] + posk_ref[...]
    xv = rows_ref[:, kvw:2 * kvw] + posv_ref[...]
    pooled = (jnp.dot(pk_ref[...], xk, preferred_element_type=F32, precision=hp),
              jnp.dot(pv_ref[...], xv, preferred_element_type=F32, precision=hp))
    for which, out in ((0, kc_ref), (1, vc_ref)):
        w1 = w1_ref[which].astype(BF16)
        w2 = w2_ref[which].astype(BF16)
        for h in range(nkv):
            x = pooled[which][:, h * HEAD_DIM:(h + 1) * HEAD_DIM].astype(BF16)
            a = jnp.dot(x, w1, preferred_element_type=F32)
            y = jnp.dot((a * jax.nn.sigmoid(a)).astype(BF16), w2, preferred_element_type=F32)
            if which == 0:
                y = y * lax.rsqrt(jnp.mean(y * y, axis=-1, keepdims=True) + EPS) * kn_ref[...]
            out[:, h * HEAD_DIM:(h + 1) * HEAD_DIM] = y


def nsa_compress(rows, b, t, nkv, cmp_pos, cmp_w, cmp_w1, cmp_w2, kn_cmp):
    nb = t // CMP_BLK
    kvw = nkv * HEAD_DIM
    eye = jnp.eye(nb, dtype=F32)
    pk = jnp.kron(eye, cmp_w[0].astype(F32)[None, :])
    pv = jnp.kron(eye, cmp_w[1].astype(F32)[None, :])
    posk = jnp.tile(cmp_pos[0].astype(F32), (nb, nkv))
    posv = jnp.tile(cmp_pos[1].astype(F32), (nb, nkv))

    def full(shape):
        return pl.BlockSpec(shape, lambda bi: (0,) * len(shape))

    return pl.pallas_call(
        functools.partial(_nsa_compress_kernel, nkv=nkv),
        grid=(b,),
        in_specs=[pl.BlockSpec((t, 2 * kvw), lambda bi: (bi, 0)), full((nb, t)), full((nb, t)), full((t, kvw)),
                  full((t, kvw)), full((2, HEAD_DIM, HEAD_DIM)), full((2, HEAD_DIM, HEAD_DIM)), full((1, HEAD_DIM))],
        out_specs=[pl.BlockSpec((None, nb, kvw), lambda bi: (bi, 0, 0))] * 2,
        out_shape=[jax.ShapeDtypeStruct((b, nb, kvw), F32)] * 2,
        compiler_params=_cparams(("parallel",), 2 * t * 2 * kvw * 4 + 4 * t * kvw * 4),
        name="nsa_compress",
    )(rows, pk, pv, posk, posv, cmp_w1.astype(F32), cmp_w2.astype(F32), kn_cmp.reshape(1, HEAD_DIM).astype(F32))


def _nsa_cmp_attn_kernel(q_ref, kc_ref, vc_ref, bias_ref, g_ref, o_ref, sel_ref, *, tq, nb, grp, n_sel, scale):
    i = pl.program_id(2)
    kc = kc_ref[...].astype(BF16)
    vc = vc_ref[...].astype(BF16)
    qpos = i * tq + lax.broadcasted_iota(jnp.int32, (tq, nb), 0)
    blk = lax.broadcasted_iota(jnp.int32, (tq, nb), 1)
    vis = qpos >= blk * CMP_BLK + (CMP_BLK - 1)
    visf = jnp.where(vis, 1.0, 0.0)
    gates = g_ref[...]
    imp = jnp.zeros((tq, nb), F32)
    for g in range(grp):
        q = q_ref[:, g * HEAD_DIM:(g + 1) * HEAD_DIM]
        s = lax.dot_general(q, kc, (((1,), (1,)), ((), ())), preferred_element_type=F32) * scale + bias_ref[g]
        s = jnp.where(vis, s, NEG)
        p = jnp.exp(s - jnp.max(s, axis=-1, keepdims=True))
        p = p / jnp.sum(p, axis=-1, keepdims=True) * visf
        imp = imp + p
        o = jnp.dot(p.astype(BF16), vc, preferred_element_type=F32)
        o_ref[:, g * HEAD_DIM:(g + 1) * HEAD_DIM] = gates[:, g:g + 1] * o
    cur = qpos // SEL_BLK
    forced = jnp.logical_or(blk == 0, jnp.logical_or(blk == cur, blk == cur - 1))
    score = jnp.where(blk <= cur, jnp.where(forced, FORCE, imp), -1.0)
    rank = jnp.zeros((tq, nb), F32)
    for n in range(nb):
        col = score[:, n:n + 1]
        beats = jnp.logical_or(col > score, jnp.logical_and(col == score, blk > n))
        rank = rank + jnp.where(beats, 1.0, 0.0)
    sel_ref[...] = jnp.where(jnp.logical_and(rank < n_sel, score >= 0.0), 1.0, 0.0)


def _nsa_slc_kernel(q_ref, k_ref, v_ref, sel_ref, tz_ref, g_ref, oin_ref, o_ref, *, tq, nb, grp, scale):
    i = pl.program_id(2)
    tk = tq
    sel = sel_ref[...].astype(BF16)
    qpos = i * tq + lax.broadcasted_iota(jnp.int32, (tq, tk), 0)
    koff = lax.broadcasted_iota(jnp.int32, (tq, tk), 1)
    n_iota = lax.broadcasted_iota(jnp.int32, (nb, tk), 0)
    c_blk = lax.broadcasted_iota(jnp.int32, (nb, tk), 1) // SEL_BLK
    gates = g_ref[...]
    for g in range(grp):
        q = q_ref[:, g * HEAD_DIM:(g + 1) * HEAD_DIM]

        def body(j, carry, q=q, g=g):
            m, l, acc = carry
            start = pl.multiple_of(j * tk, tk)
            k = k_ref[pl.ds(start, tk), :]
            v = v_ref[pl.ds(start, tk), :]
            e = jnp.where(n_iota == (tk // SEL_BLK) * j + c_blk, 1.0, 0.0).astype(BF16)
            chosen = jnp.dot(sel, e, preferred_element_type=F32) > 0.5
            mask = jnp.logical_and(chosen, qpos >= koff + j * tk)
            s = lax.dot_general(q, k, (((1,), (1,)), ((), ())), preferred_element_type=F32) * scale
            s = s + tz_ref[g, jnp.minimum(i - j, 2)]
            s = jnp.where(mask, s, NEG)
            m_new = jnp.maximum(m, jnp.max(s, axis=-1, keepdims=True))
            alpha = jnp.exp(m - m_new)
            p = jnp.exp(s - m_new)
            l = alpha * l + jnp.sum(p, axis=-1, keepdims=True)
            acc = alpha * acc + jnp.dot(p.astype(BF16), v, preferred_element_type=F32)
            return m_new, l, acc

        init = (jnp.full((tq, 1), NEG, F32), jnp.zeros((tq, 1), F32), jnp.zeros((tq, HEAD_DIM), F32))
        m, l, acc = lax.fori_loop(0, i + 1, body, init)
        cs = slice(g * HEAD_DIM, (g + 1) * HEAD_DIM)
        o_ref[:, cs] = oin_ref[:, cs] + gates[:, grp + g:grp + g + 1] * (acc / l)


def _nsa_win_kernel(q_ref, k_ref, v_ref, wz_ref, g_ref, oin_ref, o_ref, *, tq, grp, scale, window):
    i = pl.program_id(2)
    span = window + tq
    start = pl.multiple_of(i * tq, tq)
    kw = k_ref[pl.ds(start, span), :]
    vw = v_ref[pl.ds(start, span), :]
    r = lax.broadcasted_iota(jnp.int32, (tq, span), 0)
    c = lax.broadcasted_iota(jnp.int32, (tq, span), 1)
    dist = r + window - c
    mask = jnp.logical_and(jnp.logical_and(dist >= 0, dist <= window), i * tq - window + c >= 0)
    gates = g_ref[...]
    for g in range(grp):
        cs = slice(g * HEAD_DIM, (g + 1) * HEAD_DIM)
        s = lax.dot_general(q_ref[:, cs], kw, (((1,), (1,)), ((), ())), preferred_element_type=F32) * scale + wz_ref[g]
        s = jnp.where(mask, s, NEG)
        p = jnp.exp(s - jnp.max(s, axis=-1, keepdims=True))
        o = jnp.dot(p.astype(BF16), vw, preferred_element_type=F32) / jnp.sum(p, axis=-1, keepdims=True)
        o_ref[:, cs] = (oin_ref[:, cs] + gates[:, 2 * grp + g:2 * grp + g + 1] * o).astype(o_ref.dtype)


def nsa_attention(q16, rows16, wrows16, gates, kc, vc, table, b, t, nkv):
    n, d = q16.shape
    nh = d // HEAD_DIM
    grp = nh // nkv
    gw = grp * HEAD_DIM
    nb = t // CMP_BLK
    scale = HEAD_DIM ** -0.5
    sem = ("parallel", "parallel", "arbitrary")
    gk = jnp.transpose(gates.reshape(n, 3, nkv, grp), (2, 0, 1, 3)).reshape(nkv, n, 3 * grp)
    tq = min(256, t)
    nq = t // tq
    dist_c = np.arange(t)[:, None] - (np.arange(nb) * CMP_BLK + CMP_BLK - 1)[None, :]
    bias_c = _dist_bias(table, dist_c)
    qspec = pl.BlockSpec((tq, gw), lambda bi, h, i: (bi * nq + i, h))
    gspec = pl.BlockSpec((None, tq, 3 * grp), lambda bi, h, i: (h, bi * nq + i, 0))
    o1, sel = pl.pallas_call(
        functools.partial(_nsa_cmp_attn_kernel, tq=tq, nb=nb, grp=grp, n_sel=min(N_SEL, nb), scale=scale),
        grid=(b, nkv, nq),
        in_specs=[qspec,
                  pl.BlockSpec((None, nb, HEAD_DIM), lambda bi, h, i: (bi, 0, h)),
                  pl.BlockSpec((None, nb, HEAD_DIM), lambda bi, h, i: (bi, 0, h)),
                  pl.BlockSpec((grp, tq, nb), lambda bi, h, i: (h, i, 0)),
                  gspec],
        out_specs=[qspec, pl.BlockSpec((None, None, tq, nb), lambda bi, h, i: (bi, h, i, 0))],
        out_shape=[jax.ShapeDtypeStruct((n, d), F32), jax.ShapeDtypeStruct((b, nkv, t, nb), F32)],
        compiler_params=_cparams(sem, 16 << 20), name="nsa_cmp_attn",
    )(q16, kc, vc, bias_c, gk)
    tq = min(128, t)
    nq = t // tq
    r = np.arange(tq)
    tz = _dist_bias(table, np.stack([dd * tq + r[:, None] - r[None, :] for dd in range(3)]))
    qspec = pl.BlockSpec((tq, gw), lambda bi, h, i: (bi * nq + i, h))
    gspec = pl.BlockSpec((None, tq, 3 * grp), lambda bi, h, i: (h, bi * nq + i, 0))
    o2 = pl.pallas_call(
        functools.partial(_nsa_slc_kernel, tq=tq, nb=nb, grp=grp, scale=scale),
        grid=(b, nkv, nq),
        in_specs=[qspec,
                  pl.BlockSpec((t, HEAD_DIM), lambda bi, h, i: (bi, 2 * nkv + h)),
                  pl.BlockSpec((t, HEAD_DIM), lambda bi, h, i: (bi, 3 * nkv + h)),
                  pl.BlockSpec((None, None, tq, nb), lambda bi, h, i: (bi, h, i, 0)),
                  pl.BlockSpec((grp, 3, tq, tq), lambda bi, h, i: (h, 0, 0, 0)),
                  gspec, qspec],
        out_specs=qspec,
        out_shape=jax.ShapeDtypeStruct((n, d), F32),
        compiler_params=_cparams(sem, 16 << 20), name="nsa_slc_attn",
    )(q16, rows16, rows16, sel, tz, gk, o1)
    span = WINDOW + tq
    wz = _dist_bias(table, r[:, None] + WINDOW - np.arange(span)[None, :])
    kvw = nkv * HEAD_DIM
    kwp = jnp.pad(wrows16.reshape(b, t, 2 * kvw), ((0, 0), (WINDOW, 0), (0, 0))).reshape(b * (t + WINDOW), 2 * kvw)
    og = pl.pallas_call(
        functools.partial(_nsa_win_kernel, tq=tq, grp=grp, scale=scale, window=WINDOW),
        grid=(b, nkv, nq),
        in_specs=[qspec,
                  pl.BlockSpec((t + WINDOW, HEAD_DIM), lambda bi, h, i: (bi, h)),
                  pl.BlockSpec((t + WINDOW, HEAD_DIM), lambda bi, h, i: (bi, nkv + h)),
                  pl.BlockSpec((grp, tq, span), lambda bi, h, i: (h, 0, 0)),
                  gspec, qspec],
        out_specs=qspec,
        out_shape=jax.ShapeDtypeStruct((n, d), BF16),
        compiler_params=_cparams(sem, 24 << 20), name="nsa_win_attn",
    )(q16, kwp, kwp, wz, gk, o2)
    return og


def nsa_layer(xp, xs, hp, hs, b, t, n_s, pool, win_buf, page_table, w_in, qn, kn, cmp_pos, cmp_w, cmp_w1, cmp_w2, w_out, table):
    n_p, d = hp.shape
    nh = d // HEAD_DIM
    nkv = pool.shape[-2]
    tmp = _row_tile(n_p, 1024)
    q16, gates, rows, wrows, rows16, wrows16 = _nsa_proj(hp, w_in, qn, kn, tmp, nkv)
    rows5 = rows.reshape(b, t, 4, nkv, HEAD_DIM)
    wrows5 = wrows.reshape(b, t, 2, nkv, HEAD_DIM)
    kc, vc = nsa_compress(rows, b, t, nkv, cmp_pos, cmp_w, cmp_w1, cmp_w2, kn[0])
    og = nsa_attention(q16, rows16, wrows16, gates, kc, vc, table, b, t, nkv)
    xp = matmul(og, [(w_out, 0)], [(d, F32, 0)], _resid_epi, tm=tmp, tn=512, nj=d // 512, tiles=[(xp, 0)], name="nsa_out")[0]
    lbw = min(WINDOW, t)
    out_p = (rows5[None], wrows5[:, t - lbw:][None])
    sr = hs.shape[0]
    q16, gates, rows, wrows, _, _ = _nsa_proj(hs, w_in, qn, kn, sr, nkv)
    q = q16[:n_s].astype(F32).reshape(n_s, 1, nh, HEAD_DIM)
    rows_new = rows[:n_s].reshape(n_s, 1, 4, nkv, HEAD_DIM)
    wrows_new = wrows[:n_s].reshape(n_s, 1, 2, nkv, HEAD_DIM)
    g = pool[page_table]
    past = g.reshape((g.shape[0], g.shape[1] * g.shape[2]) + g.shape[3:])
    p_len = past.shape[1]
    q_pos = p_len + jnp.arange(1)
    rows_all = jnp.concatenate([past, rows_new], axis=1)
    o_cmp, o_slc = _nsa_cmp_slc(q, rows_all, q_pos, cmp_pos, cmp_w, cmp_w1, cmp_w2, kn[0], table)
    lb_ = win_buf.shape[1]
    kw = jnp.concatenate([win_buf, wrows_new], axis=1)
    o_win = _win_attend(q, kw[:, :, 0], kw[:, :, 1], q_pos, p_len - lb_ + jnp.arange(lb_ + 1), table)
    ogs = _nsa_combine(gates[:n_s], o_cmp, o_slc, o_win)
    ogs = jnp.pad(ogs, ((0, sr - n_s), (0, 0)))
    xs = matmul(ogs, [(w_out, 0)], [(d, F32, 0)], _resid_epi, tm=sr, tn=512, nj=d // 512, tiles=[(xs, 0)], name="nsa_out")[0]
    out_s = (rows_new[None], kw[:, -lb_:][None])
    return xp, xs, out_p + out_s


def kernel(x_prompt, x_sample, cache_fox_kv, cache_fox_logf, state_hgrn, state_s5, cache_nsa_kv, state_nsa_win, page_table, norm_mix, norm_ffn, fox_w_in, fox_b_f, fox_q_norm, fox_k_norm, fox_w_out, hg_w_in, hg_lb, hg_o_norm, hg_w_out, s5_a_re, s5_a_im, s5_log_dt, s5_b_re, s5_b_im, s5_c_re, s5_c_im, s5_d, s5_w_glu, nsa_w_in, nsa_q_norm, nsa_k_norm, nsa_cmp_pos, nsa_cmp_w, nsa_cmp_w1, nsa_cmp_w2, nsa_w_out, rel_bias, ffn_w1, ffn_w3, ffn_w2, moe_router, moe_router_b, moe_w1, moe_w3, moe_w2):
    b, t, d = x_prompt.shape
    n_s = x_sample.shape[0]
    assert x_sample.shape[1] == 1
    depth = norm_mix.shape[0]
    xp = x_prompt.reshape(b * t, d)
    xs = jnp.pad(x_sample.reshape(n_s, d), ((0, SAMPLE_ROWS - n_s), (0, 0)))
    sm = jax.nn.softmax(hg_lb.astype(F32), axis=0)
    lower_bounds = jnp.cumsum(sm, axis=0) - sm[0]
    res = {}
    for i in range(depth):
        j = i // 4
        kind = i % 4
        if kind == 2:
            hp = rms_norm(xp, norm_mix[i], F32)
            hs = rms_norm(xs, norm_mix[i], F32)
        else:
            hp = rms_norm(xp, norm_mix[i])
            hs = rms_norm(xs, norm_mix[i])
        if kind == 0:
            xp, xs, o = fox_layer(xp, xs, hp, hs, b, t, n_s, cache_fox_kv[j], cache_fox_logf[j], page_table,
                                  fox_w_in[j], fox_b_f[j], fox_q_norm[j], fox_k_norm[j], fox_w_out[j])
        elif kind == 1:
            xp, xs, o = hgrn_layer(xp, xs, hp, hs, b, t, n_s, state_hgrn[j], hg_w_in[j], lower_bounds[i],
                                   hg_o_norm[j], hg_w_out[j])
        elif kind == 2:
            xp, xs, o = s5_layer(xp, xs, hp, hs, b, t, n_s, state_s5[j], s5_a_re[j], s5_a_im[j], s5_log_dt[j],
                                 s5_b_re[j], s5_b_im[j], s5_c_re[j], s5_c_im[j], s5_d[j], s5_w_glu[j])
        else:
            xp, xs, o = nsa_layer(xp, xs, hp, hs, b, t, n_s, cache_nsa_kv[j], state_nsa_win[j], page_table,
                                  nsa_w_in[j], nsa_q_norm[j], nsa_k_norm[j], nsa_cmp_pos[j], nsa_cmp_w[j],
                                  nsa_cmp_w1[j], nsa_cmp_w2[j], nsa_w_out[j], rel_bias)
        res[kind] = o
        hp = rms_norm(xp, norm_ffn[i])
        hs = rms_norm(xs, norm_ffn[i])
        f = i // 2
        if i % 2 == 0:
            xp, xs = dense_ffn(xp, xs, hp, hs, ffn_w1[f], ffn_w3[f], ffn_w2[f])
        else:
            xp, xs = moe_ffn(xp, xs, hp, hs, n_s, moe_router[f], moe_router_b[f], moe_w1[f], moe_w3[f], moe_w2[f])
    return (xp.reshape(b, t, d), xs[:n_s].reshape(n_s, 1, d)) + res[0] + res[1] + res[2] + res[3]
```

```python
import functools
import math

import numpy as np
import jax
import jax.numpy as jnp
from jax import lax
from jax.experimental import pallas as pl
from jax.experimental.pallas import tpu as pltpu

F32 = jnp.float32
BF16 = jnp.bfloat16
EPS = 1e-6
NEG = -1e30
FORCE = 1e9

HEAD_DIM = 128
LANES = 128
V7X_VMEM_CAP_MB = 60
SAMPLE_ROWS = 16

HG_CHUNK = 16
S5_CH = 16
S5_CHUNK = 128
CMP_BLK = 64
SEL_BLK = 64
N_SEL = 16
WINDOW = 512
N_BUCKETS = 32
MAX_DIST = 128
TOP_K = 2
MOE_ROWS = 512


def _cparams(sem, vmem_bytes):
    mb = min(V7X_VMEM_CAP_MB, max(16, int(vmem_bytes / (1 << 20)) + 8))
    return pltpu.CompilerParams(dimension_semantics=sem, vmem_limit_bytes=mb << 20)


def _rms_kernel(x_ref, g_ref, o_ref):
    x = x_ref[...]
    y = x * lax.rsqrt(jnp.mean(x * x, axis=-1, keepdims=True) + EPS)
    o_ref[...] = (y * g_ref[...]).astype(o_ref.dtype)


def rms_norm(x, g, out_dtype=BF16):
    n, d = x.shape
    tm = min(n, 512)
    return pl.pallas_call(
        _rms_kernel,
        grid=(n // tm,),
        in_specs=[pl.BlockSpec((tm, d), lambda i: (i, 0)),
                  pl.BlockSpec((1, d), lambda i: (0, 0))],
        out_specs=pl.BlockSpec((tm, d), lambda i: (i, 0)),
        out_shape=jax.ShapeDtypeStruct((n, d), out_dtype),
        compiler_params=_cparams(("parallel",), 2 * tm * d * (4 + 4)),
        name="rms_norm",
    )(x, g.reshape(1, d).astype(F32))


def _mm_kernel(*refs, nw, nv, nt, no, ni, has_e, epilogue):
    pos = 0
    eid_ref = None
    if has_e:
        eid_ref = refs[0]
        pos = 1
    lhs_ref = refs[pos]
    pos += 1
    w_refs = refs[pos:pos + nw]
    pos += nw
    vec_refs = refs[pos:pos + nv]
    pos += nv
    tile_refs = refs[pos:pos + nt]
    pos += nt
    out_refs = refs[pos:pos + no]
    wc_ref = refs[-1]
    j = pl.program_id(0)
    i = pl.program_id(1)
    if has_e:
        prev = eid_ref[jnp.maximum(i - 1, 0)]
        recast = jnp.logical_or(i == 0, eid_ref[i] != prev)
        active = i < eid_ref[ni]
    else:
        recast = i == 0
        active = None

    @pl.when(recast)
    def _():
        for t in range(nw):
            wc_ref[t] = w_refs[t][...].astype(BF16)

    def compute():
        x = lhs_ref[...]
        accs = [jnp.dot(x, wc_ref[t], preferred_element_type=F32) for t in range(nw)]
        res = epilogue(accs, [v[...] for v in vec_refs], [t[...] for t in tile_refs], j)
        for o, r in zip(out_refs, res):
            o[...] = r.astype(o.dtype)

    if has_e:
        pl.when(active)(compute)

        @pl.when(jnp.logical_not(active))
        def _():
            for o in out_refs:
                o[...] = jnp.zeros(o.shape, o.dtype)
    else:
        compute()


def matmul(lhs, ws, outs, epilogue, *, tm, tn, nj, tk=None, kb=0, vecs=(), tiles=(), eids=None, name="mm"):
    n = lhs.shape[0]
    assert n % tm == 0, (n, tm)
    ni = n // tm
    kdim = ws[0][0].shape[-2]
    tk = tk or kdim
    assert kdim % tk == 0
    has_e = eids is not None
    nw, nv, nt, no = len(ws), len(vecs), len(tiles), len(outs)

    def wspec(arr, off):
        if arr.ndim == 3:
            return pl.BlockSpec((None, tk, tn), lambda j, i, e: (e[i], kb, j + off))
        if has_e:
            return pl.BlockSpec((tk, tn), lambda j, i, e: (kb, j + off))
        return pl.BlockSpec((tk, tn), lambda j, i: (kb, j + off))

    def fix(f):
        return (lambda j, i, e: f(j, i)) if has_e else f

    def fix_in(f):
        return (lambda j, i, e: f(j, jnp.minimum(i, e[ni] - 1))) if has_e else f

    in_specs = [pl.BlockSpec((tm, tk), fix_in(lambda j, i: (i, kb)))]
    in_specs += [wspec(a, off) for a, off in ws]
    in_specs += [pl.BlockSpec((1, tn), fix(functools.partial(lambda j, i, off: (0, j + off), off=off))) for _, off in vecs]
    in_specs += [pl.BlockSpec((tm, tn), fix_in(functools.partial(lambda j, i, off: (i, j + off), off=off))) for _, off in tiles]
    out_specs = [pl.BlockSpec((tm, tn), fix(functools.partial(lambda j, i, off: (i, j + off), off=off))) for _, _, off in outs]
    out_shape = [jax.ShapeDtypeStruct((n, c), dt) for c, dt, _ in outs]
    vmem = (2 * tm * tk * 2 + nw * (2 * tk * tn * 4 + tk * tn * 2)
            + sum(2 * tm * tn * jnp.dtype(dt).itemsize for _, dt, _ in outs)
            + nt * 2 * tm * tn * 4 + (nw + 1) * tm * tn * 4)
    kern = functools.partial(_mm_kernel, nw=nw, nv=nv, nt=nt, no=no, ni=ni, has_e=has_e, epilogue=epilogue)
    scratch = [pltpu.VMEM((nw, tk, tn), BF16)]
    args = [lhs] + [a for a, _ in ws] + [a for a, _ in vecs] + [a for a, _ in tiles]
    if has_e:
        grid_spec = pltpu.PrefetchScalarGridSpec(num_scalar_prefetch=1, grid=(nj, ni), in_specs=in_specs,
                                                 out_specs=out_specs, scratch_shapes=scratch)
        args = [eids] + args
    else:
        grid_spec = pl.GridSpec(grid=(nj, ni), in_specs=in_specs, out_specs=out_specs, scratch_shapes=scratch)
    return pl.pallas_call(kern, grid_spec=grid_spec, out_shape=out_shape,
                          compiler_params=_cparams(("arbitrary", "arbitrary"), vmem), name=name)(*args)


def _row_tile(n, pref):
    return pref if n % pref == 0 else n


def _head_rms(x, g):
    parts = []
    for h in range(x.shape[1] // HEAD_DIM):
        xh = x[:, h * HEAD_DIM:(h + 1) * HEAD_DIM]
        gh = g[:, h * HEAD_DIM:(h + 1) * HEAD_DIM]
        parts.append(xh * lax.rsqrt(jnp.mean(xh * xh, axis=-1, keepdims=True) + EPS) * gh)
    return parts[0] if len(parts) == 1 else jnp.concatenate(parts, axis=-1)


def _vec(v, width=None):
    v = v.reshape(1, -1).astype(F32)
    if width is not None and v.shape[1] < width:
        v = jnp.pad(v, ((0, 0), (0, width - v.shape[1])))
    return v


def _valid_lanes(x, n):
    return jnp.where(lax.broadcasted_iota(jnp.int32, x.shape, 1) < n, x, 0.0)


def _log_sigmoid(x):
    return jnp.minimum(x, 0.0) - jnp.log(1.0 + jnp.exp(-jnp.abs(x)))


def _swiglu_epi(accs, vecs, tiles, j):
    a, b = accs
    return [a * jax.nn.sigmoid(a) * b]


def _resid_epi(accs, vecs, tiles, j):
    return [accs[0] + tiles[0]]


def _plain_epi(accs, vecs, tiles, j):
    return [accs[0]]


def ffn_up(h, w1, w3, eids=None, tm=1024):
    dff = w1.shape[-1]
    tn = 256
    assert dff % tn == 0
    return matmul(h, [(w1, 0), (w3, 0)], [(dff, BF16, 0)], _swiglu_epi, tm=tm, tn=tn, nj=dff // tn,
                  eids=eids, name="ffn_up")[0]


def ffn_down(hmid, w2, resid, eids=None, tm=512):
    dff, d = w2.shape[-2], w2.shape[-1]
    halves = 2 if dff % (2 * LANES) == 0 and dff > 4096 else 1
    tk = dff // halves
    tn = min(512, d)
    out = resid
    for kb in range(halves):
        out = matmul(hmid, [(w2, 0)], [(d, F32, 0)], _resid_epi if out is not None else _plain_epi, tm=tm, tn=tn,
                     nj=d // tn, tk=tk, kb=kb, tiles=[(out, 0)] if out is not None else [], eids=eids, name="ffn_down")[0]
    return out


def dense_ffn(xp, xs, hp, hs, w1, w3, w2):
    mp = ffn_up(hp, w1, w3, tm=_row_tile(hp.shape[0], 1024))
    xp = ffn_down(mp, w2, xp, tm=_row_tile(hp.shape[0], 512))
    ms = ffn_up(hs, w1, w3, tm=hs.shape[0])
    xs = ffn_down(ms, w2, xs, tm=hs.shape[0])
    return xp, xs


def _router_epi(accs, vecs, tiles, j):
    return [accs[0] + vecs[0]]


def moe_ffn(xp, xs, hp, hs, n_s, router, router_b, w1, w3, w2):
    n_p, d = hp.shape
    n_e = router.shape[-1]
    h_all = jnp.concatenate([hp, hs[:n_s]], axis=0)
    n = n_p + n_s
    rw = jnp.pad(router, ((0, 0), (0, LANES - n_e)))
    lp = matmul(hp, [(rw, 0)], [(LANES, F32, 0)], _router_epi, tm=_row_tile(n_p, 1024), tn=LANES, nj=1,
                vecs=[(_vec(router_b, LANES), 0)], name="router")[0]
    ls = matmul(hs, [(rw, 0)], [(LANES, F32, 0)], _router_epi, tm=hs.shape[0], tn=LANES, nj=1,
                vecs=[(_vec(router_b, LANES), 0)], name="router")[0]
    logits = jnp.concatenate([lp[:, :n_e], ls[:n_s, :n_e]], axis=0)
    top_l, top_e = lax.top_k(logits, TOP_K)
    gate = jax.nn.softmax(top_l, axis=-1)
    a = n * TOP_K
    bm = MOE_ROWS
    e_flat = top_e.reshape(-1)
    tok = jnp.arange(a) // TOP_K
    order = jnp.argsort(e_flat)
    e_sorted = e_flat[order]
    tok_sorted = tok[order]
    counts = jnp.bincount(e_flat, length=n_e)
    padded = (counts + bm - 1) // bm * bm
    pad_end = jnp.cumsum(padded)
    grp_start = jnp.cumsum(counts) - counts
    dest = (pad_end - padded)[e_sorted] + jnp.arange(a) - grp_start[e_sorted]
    n_blocks = -(-a // bm) + n_e
    blk_e = jnp.minimum(jnp.searchsorted(pad_end, jnp.arange(n_blocks) * bm, side='right'), n_e - 1)
    n_act = pad_end[-1] // bm
    eids = jnp.concatenate([blk_e, n_act[None]]).astype(jnp.int32)
    src = jnp.full((n_blocks * bm,), -1, jnp.int32).at[dest].set(tok_sorted.astype(jnp.int32))
    pos = jnp.zeros((a,), jnp.int32).at[order].set(dest.astype(jnp.int32)).reshape(n, TOP_K)
    buf = moe_dispatch(h_all, src, bm)
    mid = ffn_up(buf, w1, w3, eids=eids, tm=bm)
    out = ffn_down(mid, w2, None, eids=eids, tm=bm)
    xp = moe_combine(xp, out, pos[:n_p], gate[:n_p])
    sr = xs.shape[0]
    xs = moe_combine(xs, out, jnp.pad(pos[n_p:], ((0, sr - n_s), (0, 0))), jnp.pad(gate[n_p:], ((0, sr - n_s), (0, 0))))
    return xp, xs


def _moe_dispatch_kernel(src_ref, h_ref, o_ref, sem, *, bm):
    base = pl.program_id(0) * bm

    def row_copy(r, s):
        return pltpu.make_async_copy(h_ref.at[pl.ds(s, 1), :], o_ref.at[pl.ds(r, 1), :], sem)

    def start(r, c):
        s = src_ref[base + r]

        @pl.when(s >= 0)
        def _():
            row_copy(r, s).start()

        @pl.when(s < 0)
        def _():
            o_ref[pl.ds(r, 1), :] = jnp.zeros((1, o_ref.shape[1]), o_ref.dtype)
        return c

    def wait(r, c):
        s = src_ref[base + r]

        @pl.when(s >= 0)
        def _():
            row_copy(r, s).wait()
        return c

    lax.fori_loop(0, bm, start, 0)
    lax.fori_loop(0, bm, wait, 0)


def moe_dispatch(h, src, bm):
    n, d = h.shape
    r = src.shape[0]
    h32 = lax.bitcast_convert_type(h.reshape(n, d // 2, 2), jnp.uint32)
    out = pl.pallas_call(
        functools.partial(_moe_dispatch_kernel, bm=bm),
        grid_spec=pltpu.PrefetchScalarGridSpec(
            num_scalar_prefetch=1, grid=(r // bm,),
            in_specs=[pl.BlockSpec(memory_space=pl.ANY)],
            out_specs=pl.BlockSpec((bm, d // 2), lambda i, s: (i, 0)),
            scratch_shapes=[pltpu.SemaphoreType.DMA(())]),
        out_shape=jax.ShapeDtypeStruct((r, d // 2), jnp.uint32),
        compiler_params=_cparams(("arbitrary",), 2 * bm * d * 2), name="moe_dispatch",
    )(src, h32)
    return lax.bitcast_convert_type(out, BF16).reshape(r, d)


def _moe_combine_kernel(pos_ref, x_ref, g_ref, y_ref, o_ref, buf, sem, *, tm, top_k):
    base = pl.program_id(0) * tm

    def row_copy(r, k):
        return pltpu.make_async_copy(y_ref.at[pl.ds(pos_ref[(base + r) * top_k + k], 1), :],
                                     buf.at[k, pl.ds(r, 1), :], sem)

    def start(r, c):
        for k in range(top_k):
            row_copy(r, k).start()
        return c

    def wait(r, c):
        for k in range(top_k):
            row_copy(r, k).wait()
        return c

    lax.fori_loop(0, tm, start, 0)
    lax.fori_loop(0, tm, wait, 0)
    g = g_ref[...]
    acc = x_ref[...]
    for k in range(top_k):
        acc = acc + g[:, k:k + 1] * buf[k]
    o_ref[...] = acc


def moe_combine(x, y, pos, gate):
    n, d = x.shape
    top_k = pos.shape[1]
    tm = min(n, 256)
    tile = pl.BlockSpec((tm, d), lambda i, p: (i, 0))
    return pl.pallas_call(
        functools.partial(_moe_combine_kernel, tm=tm, top_k=top_k),
        grid_spec=pltpu.PrefetchScalarGridSpec(
            num_scalar_prefetch=1, grid=(n // tm,),
            in_specs=[tile, pl.BlockSpec((tm, top_k), lambda i, p: (i, 0)), pl.BlockSpec(memory_space=pl.ANY)],
            out_specs=tile,
            scratch_shapes=[pltpu.VMEM((top_k, tm, d), F32), pltpu.SemaphoreType.DMA(())]),
        out_shape=jax.ShapeDtypeStruct((n, d), F32),
        compiler_params=_cparams(("arbitrary",), (4 + top_k) * tm * d * 4), name="moe_combine",
    )(pos.reshape(-1).astype(jnp.int32), x, gate.astype(F32), y)


def _cumsum_kernel(x_ref, o_ref):
    c = x_ref[...]
    row = lax.broadcasted_iota(jnp.int32, c.shape, 0)
    sh = 1
    while sh < c.shape[0]:
        c = c + jnp.where(row >= sh, pltpu.roll(c, sh, 0), 0.0)
        sh *= 2
    o_ref[...] = c


def seq_cumsum(x):
    b, t, h = x.shape
    spec = pl.BlockSpec((None, t, h), lambda i: (i, 0, 0))
    return pl.pallas_call(_cumsum_kernel, grid=(b,), in_specs=[spec], out_specs=spec,
                          out_shape=jax.ShapeDtypeStruct(x.shape, F32),
                          compiler_params=_cparams(("parallel",), 8 * t * LANES * 4), name="seq_cumsum")(x)


def _fox_attn_kernel(q_ref, k_ref, v_ref, ccol_ref, crow_ref, g_ref, o_ref, *, tq, tk, scale):
    qi = pl.program_id(2)
    q = q_ref[...]
    cq = ccol_ref[...]
    qpos = qi * tq + lax.broadcasted_iota(jnp.int32, (tq, tk), 0)
    koff = lax.broadcasted_iota(jnp.int32, (tq, tk), 1)

    def body(j, carry):
        m, l, acc = carry
        start = pl.multiple_of(j * tk, tk)
        k = k_ref[pl.ds(start, tk), :]
        v = v_ref[pl.ds(start, tk), :]
        s = lax.dot_general(q, k, (((1,), (1,)), ((), ())), preferred_element_type=F32) * scale
        s = s + cq - crow_ref[pl.ds(j, 1), :]
        s = jnp.where(qpos >= koff + j * tk, s, NEG)
        m_new = jnp.maximum(m, jnp.max(s, axis=-1, keepdims=True))
        alpha = jnp.exp(m - m_new)
        p = jnp.exp(s - m_new)
        l = alpha * l + jnp.sum(p, axis=-1, keepdims=True)
        acc = alpha * acc + jnp.dot(p.astype(BF16), v, preferred_element_type=F32)
        return m_new, l, acc

    n_kv = (qi * tq + tq + tk - 1) // tk
    init = (jnp.full((tq, 1), NEG, F32), jnp.zeros((tq, 1), F32), jnp.zeros((tq, HEAD_DIM), F32))
    m, l, acc = lax.fori_loop(0, n_kv, body, init)
    o_ref[...] = (acc / l * g_ref[...]).astype(o_ref.dtype)


def fox_attention(q16, kv16, c, gate, b, t, nh):
    tq = min(256, t)
    tk = tq
    nq = t // tq
    ccol = jnp.transpose(c, (0, 2, 1)).reshape(b, nh, t, 1)
    crow = jnp.transpose(c, (0, 2, 1)).reshape(b, nh, t // tk, tk)
    kern = functools.partial(_fox_attn_kernel, tq=tq, tk=tk, scale=HEAD_DIM ** -0.5)
    return pl.pallas_call(
        kern,
        grid=(b, nh, nq),
        in_specs=[pl.BlockSpec((tq, HEAD_DIM), lambda bi, h, i: (bi * nq + i, h)),
                  pl.BlockSpec((t, HEAD_DIM), lambda bi, h, i: (bi, h)),
                  pl.BlockSpec((t, HEAD_DIM), lambda bi, h, i: (bi, nh + h)),
                  pl.BlockSpec((None, None, tq, 1), lambda bi, h, i: (bi, h, i, 0)),
                  pl.BlockSpec((None, None, t // tk, tk), lambda bi, h, i: (bi, h, 0, 0)),
                  pl.BlockSpec((tq, HEAD_DIM), lambda bi, h, i: (bi * nq + i, h))],
        out_specs=pl.BlockSpec((tq, HEAD_DIM), lambda bi, h, i: (bi * nq + i, h)),
        out_shape=jax.ShapeDtypeStruct(q16.shape, BF16),
        compiler_params=_cparams(("parallel", "parallel", "arbitrary"), 8 << 20),
        name="fox_attention",
    )(q16, kv16, kv16, ccol, crow, gate)


def _fox_proj(h, w_in, b_f, qn, kn, tm):
    d = h.shape[1]
    nh = d // HEAD_DIM
    tn = 512
    nb = d // tn
    qn_v, kn_v = _vec(qn), _vec(kn)

    def q_epi(accs, vecs, tiles, j):
        return [_head_rms(accs[0], vecs[0])]

    def kv_epi(accs, vecs, tiles, j):
        y = jnp.where(j < nb, _head_rms(accs[0], vecs[0]), accs[0])
        return [y, y]

    def gate_epi(accs, vecs, tiles, j):
        return [jax.nn.sigmoid(accs[0])]

    def lf_epi(accs, vecs, tiles, j):
        return [_valid_lanes(_log_sigmoid(accs[0] + vecs[0]), nh)]

    def tiled(v):
        return jnp.tile(v, (1, 2 * d // HEAD_DIM))

    q16 = matmul(h, [(w_in, 0)], [(d, BF16, 0)], q_epi, tm=tm, tn=tn, nj=nb, vecs=[(tiled(qn_v), 0)], name="fox_q")[0]
    kv, kv16 = matmul(h, [(w_in, nb)], [(2 * d, F32, 0), (2 * d, BF16, 0)], kv_epi, tm=tm, tn=tn, nj=2 * nb,
                      vecs=[(tiled(kn_v), 0)], name="fox_kv")
    gate = matmul(h, [(w_in, 3 * nb)], [(d, F32, 0)], gate_epi, tm=tm, tn=tn, nj=nb, name="fox_gate")[0]
    lf = matmul(h, [(w_in, 4 * d // LANES)], [(LANES, F32, 0)], lf_epi, tm=tm, tn=LANES, nj=1,
                vecs=[(_vec(b_f, LANES), 0)], name="fox_logf")[0][:, :nh]
    return q16, kv, kv16, gate, lf


def _lane_column(x, lane, n):
    if x.shape[1] <= LANES:
        return x[:, n:n + 1]
    return jnp.sum(jnp.where(lane == n, x, 0.0), axis=-1, keepdims=True)


def _split_bf16(x):
    hi = x.astype(BF16)
    return hi, (x - hi.astype(F32)).astype(BF16)


def _fox_decode_kernel(pt_ref, q_ref, kv_ref, lf_ref, e_ref, et_ref, kn_ref, vn_ref, lfn_ref, g_ref, o_ref,
                       m_s, l_s, acc_s, c_s, *, nh, page, scale):
    p = pl.program_id(1)
    d = nh * HEAD_DIM

    @pl.when(p == 0)
    def _():
        m_s[...] = jnp.full(m_s.shape, NEG, F32)
        l_s[...] = jnp.zeros(l_s.shape, F32)
        acc_s[...] = jnp.zeros(acc_s.shape, F32)
        c_s[...] = jnp.zeros(c_s.shape, F32)

    def head_sums(x):
        hi, lo = _split_bf16(x)
        return (jnp.dot(hi, e_ref[...], preferred_element_type=F32)
                + jnp.dot(lo, e_ref[...], preferred_element_type=F32))

    def head_bcast(x):
        hi, lo = _split_bf16(jnp.broadcast_to(x, (16, nh)))
        y = (jnp.dot(hi, et_ref[...], preferred_element_type=F32)
             + jnp.dot(lo, et_ref[...], preferred_element_type=F32))
        return y[0:1, :]

    q = q_ref[...]
    qk = head_sums(kv_ref[:, :d] * q)
    c = lf_ref[...]
    row = lax.broadcasted_iota(jnp.int32, c.shape, 0)
    sh = 1
    while sh < page:
        c = c + jnp.where(row >= sh, pltpu.roll(c, sh, 0), 0.0)
        sh *= 2
    c = c + c_s[...]
    s = qk * scale - c
    m_old = m_s[...]
    m_new = jnp.maximum(m_old, jnp.max(s, axis=0, keepdims=True))
    alpha = jnp.exp(m_old - m_new)
    pm = jnp.exp(s - m_new)
    l_s[...] = alpha * l_s[...] + jnp.sum(pm, axis=0, keepdims=True)
    m_s[...] = m_new
    c_s[...] = c[page - 1:page, :]
    pv = jnp.dot(pm.astype(BF16), et_ref[...], preferred_element_type=F32) * kv_ref[:, d:]
    acc_s[...] = acc_s[...] * head_bcast(alpha) + jnp.sum(pv.reshape(page // 8, 8, d), axis=0)

    @pl.when(p == pl.num_programs(1) - 1)
    def _():
        c_new = c_s[...] + lfn_ref[...]
        s_n = head_sums(jnp.broadcast_to(kn_ref[...] * q, (16, d)))[0:1, :] * scale - c_new
        m_f = jnp.maximum(m_s[...], s_n)
        a_f = jnp.exp(m_s[...] - m_f)
        p_n = jnp.exp(s_n - m_f)
        l_f = a_f * l_s[...] + p_n
        pn_b = jnp.dot(jnp.broadcast_to(p_n, (16, nh)).astype(BF16), et_ref[...], preferred_element_type=F32)[0:1, :]
        acc = jnp.sum(acc_s[...], axis=0, keepdims=True) * head_bcast(a_f) + pn_b * vn_ref[...]
        o_ref[...] = acc / head_bcast(l_f) * g_ref[...]


def fox_decode(q, k_new, v_new, lf_new, gate, kv_pool, lf_pool, page_table):
    bq, d = q.shape
    nh = d // HEAD_DIM
    n_phys, page = kv_pool.shape[0], kv_pool.shape[1]
    n_pages = page_table.shape[1]
    head_of_lane = np.arange(d) // HEAD_DIM
    e = jnp.asarray(head_of_lane[:, None] == np.arange(nh)[None, :], BF16)
    et = jnp.asarray(np.arange(nh)[:, None] == head_of_lane[None, :], BF16)

    def row(shape):
        return pl.BlockSpec((None,) + shape, lambda b, p, pt: (b, 0, 0))

    def const(shape):
        return pl.BlockSpec(shape, lambda b, p, pt: (0, 0))

    grid_spec = pltpu.PrefetchScalarGridSpec(
        num_scalar_prefetch=1, grid=(bq, n_pages),
        in_specs=[row((1, d)),
                  pl.BlockSpec((None, page, 2 * d), lambda b, p, pt: (pt[b * n_pages + p], 0, 0)),
                  pl.BlockSpec((None, page, nh), lambda b, p, pt: (pt[b * n_pages + p], 0, 0)),
                  const((d, nh)), const((nh, d)), row((1, d)), row((1, d)), row((1, nh)), row((1, d))],
        out_specs=row((1, d)),
        scratch_shapes=[pltpu.VMEM((1, nh), F32), pltpu.VMEM((1, nh), F32), pltpu.VMEM((8, d), F32),
                        pltpu.VMEM((1, nh), F32)])
    out = pl.pallas_call(
        functools.partial(_fox_decode_kernel, nh=nh, page=page, scale=HEAD_DIM ** -0.5),
        grid_spec=grid_spec, out_shape=jax.ShapeDtypeStruct((bq, 1, d), F32),
        compiler_params=_cparams(("parallel", "arbitrary"), 2 * page * 2 * d * 4 + 6 * page * d * 4),
        name="fox_decode",
    )(page_table.reshape(-1).astype(jnp.int32), q.reshape(bq, 1, d), kv_pool.reshape(n_phys, page, 2 * d),
      lf_pool.reshape(n_phys, page, nh).astype(F32), e, et, k_new.reshape(bq, 1, d), v_new.reshape(bq, 1, d),
      lf_new.reshape(bq, 1, nh), gate.reshape(bq, 1, d))
    return out.reshape(bq, d)


def fox_layer(xp, xs, hp, hs, b, t, n_s, kv_pool, lf_pool, page_table, w_in, b_f, qn, kn, w_out):
    n_p, d = hp.shape
    nh = d // HEAD_DIM
    tmp = _row_tile(n_p, 1024)
    q16, kvp, kv16, gate, lf = _fox_proj(hp, w_in, b_f, qn, kn, tmp)
    lfp = lf.reshape(b, t, nh)
    c = seq_cumsum(lfp)
    og = fox_attention(q16, kv16, c, gate, b, t, nh)
    xp = matmul(og, [(w_out, 0)], [(d, F32, 0)], _resid_epi, tm=tmp, tn=512, nj=d // 512, tiles=[(xp, 0)], name="fox_out")[0]
    sr = hs.shape[0]
    q16s, kvs, _, gates, lfs = _fox_proj(hs, w_in, b_f, qn, kn, sr)
    logf = lfs[:n_s].reshape(n_s, 1, nh)
    ogs = fox_decode(q16s[:n_s].astype(F32), kvs[:n_s, :d], kvs[:n_s, d:], lfs[:n_s], gates[:n_s],
                     kv_pool, lf_pool, page_table).astype(BF16)
    ogs = jnp.pad(ogs, ((0, sr - n_s), (0, 0)))
    xs = matmul(ogs, [(w_out, 0)], [(d, F32, 0)], _resid_epi, tm=sr, tn=512, nj=d // 512, tiles=[(xs, 0)], name="fox_out")[0]
    outs = (kvp.reshape(1, b, t, 2, nh, HEAD_DIM), lfp.reshape(1, b, t, nh),
            kvs[:n_s].reshape(1, n_s, 1, 2, nh, HEAD_DIM), logf.reshape(1, n_s, 1, nh))
    return xp, xs, outs


def _hg_proj(h, w_in, lb, tm):
    d = h.shape[1]
    tn = 512
    nb = d // tn

    def q_epi(accs, vecs, tiles, j):
        a = accs[0]
        return [a * jax.nn.sigmoid(a)]

    def f_epi(accs, vecs, tiles, j):
        f = vecs[0] + (1.0 - vecs[0]) * jax.nn.sigmoid(accs[0])
        return [1.0 - f, jnp.log(f)]

    def id_epi(accs, vecs, tiles, j):
        return [accs[0]]

    def gate_epi(accs, vecs, tiles, j):
        return [jax.nn.sigmoid(accs[0])]

    q = matmul(h, [(w_in, 0)], [(d, F32, 0)], q_epi, tm=tm, tn=tn, nj=nb, name="hg_q")[0]
    k, logf = matmul(h, [(w_in, nb)], [(d, F32, 0), (d, F32, 0)], f_epi, tm=tm, tn=tn, nj=nb,
                     vecs=[(_vec(lb), 0)], name="hg_f")
    v = matmul(h, [(w_in, 2 * nb)], [(d, F32, 0)], id_epi, tm=tm, tn=tn, nj=nb, name="hg_v")[0]
    gate = matmul(h, [(w_in, 3 * nb)], [(d, F32, 0)], gate_epi, tm=tm, tn=tn, nj=nb, name="hg_gate")[0]
    return q, k, v, logf, gate


def _hgrn_kernel(q_ref, k_ref, v_ref, lf_ref, gate_ref, on_ref, s0_ref, o_ref, sf_ref, st_s, *, hb, nchunk, chunk):
    tb = pl.program_id(2)

    @pl.when(tb == 0)
    def _():
        st_s[...] = s0_ref[...]

    lf = lf_ref[...]
    row = lax.broadcasted_iota(jnp.int32, lf.shape, 0) % chunk
    b = lf
    sh = 1
    while sh < chunk:
        b = b + jnp.where(row >= sh, pltpu.roll(b, sh, 0), 0.0)
        sh *= 2
    k = k_ref[...]
    v = v_ref[...]
    qd = q_ref[...] * jnp.exp(b)
    kd = k * jnp.exp(-b)
    tri = lax.broadcasted_iota(jnp.int32, (chunk, chunk), 0) >= lax.broadcasted_iota(jnp.int32, (chunk, chunk), 1)
    on = on_ref[...]
    for h in range(hb):
        st = st_s[h]
        cs = slice(h * HEAD_DIM, (h + 1) * HEAD_DIM)
        for c in range(nchunk):
            rs = slice(c * chunk, (c + 1) * chunk)
            bc = b[rs, cs]
            bl = bc[chunk - 1:chunk, :]
            qdc = qd[rs, cs].astype(BF16)
            kdc = kd[rs, cs].astype(BF16)
            vc = v[rs, cs].astype(BF16)
            k2 = (k[rs, cs] * jnp.exp(bl - bc)).astype(BF16)
            att = lax.dot_general(qdc, kdc, (((1,), (1,)), ((), ())), preferred_element_type=F32)
            att = jnp.where(tri, att, 0.0)
            o = lax.dot_general(qdc, st.astype(BF16), (((1,), (1,)), ((), ())), preferred_element_type=F32)
            o = o + jnp.dot(att.astype(BF16), vc, preferred_element_type=F32)
            st = st * jnp.exp(bl) + lax.dot_general(vc, k2, (((0,), (0,)), ((), ())), preferred_element_type=F32)
            y = o * lax.rsqrt(jnp.mean(o * o, axis=-1, keepdims=True) + EPS) * on
            o_ref[rs, cs] = (y * gate_ref[rs, cs]).astype(o_ref.dtype)
        st_s[h] = st

    @pl.when(tb == pl.num_programs(2) - 1)
    def _():
        sf_ref[...] = st_s[...]


def hgrn_scan(q, k, v, logf, gate, on, s0t, b, t):
    d = q.shape[1]
    nh = d // HEAD_DIM
    hb = 2
    tb = min(256, t)
    ntb = t // tb
    kern = functools.partial(_hgrn_kernel, hb=hb, nchunk=tb // HG_CHUNK, chunk=HG_CHUNK)
    tile = pl.BlockSpec((tb, hb * HEAD_DIM), lambda bi, h, i: (bi * ntb + i, h))
    st_spec = pl.BlockSpec((None, hb, HEAD_DIM, HEAD_DIM), lambda bi, h, i: (bi, h, 0, 0))
    return pl.pallas_call(
        kern,
        grid=(b, nh // hb, ntb),
        in_specs=[tile, tile, tile, tile, tile, pl.BlockSpec((1, HEAD_DIM), lambda bi, h, i: (0, 0)), st_spec],
        out_specs=[tile, st_spec],
        out_shape=[jax.ShapeDtypeStruct(q.shape, BF16), jax.ShapeDtypeStruct(s0t.shape, F32)],
        scratch_shapes=[pltpu.VMEM((hb, HEAD_DIM, HEAD_DIM), F32)],
        compiler_params=_cparams(("parallel", "parallel", "arbitrary"), 16 << 20),
        name="hgrn_scan",
    )(q, k, v, logf, gate, on.reshape(1, HEAD_DIM).astype(F32), s0t)


def hgrn_layer(xp, xs, hp, hs, b, t, n_s, state, w_in, lb, on, w_out):
    n_p, d = hp.shape
    nh = d // HEAD_DIM
    tmp = _row_tile(n_p, 1024)
    q, k, v, logf, gate = _hg_proj(hp, w_in, lb, tmp)
    s0 = jnp.zeros((b, nh, HEAD_DIM, HEAD_DIM), F32)
    og, sft = hgrn_scan(q, k, v, logf, gate, on, s0, b, t)
    xp = matmul(og, [(w_out, 0)], [(d, F32, 0)], _resid_epi, tm=tmp, tn=512, nj=d // 512, tiles=[(xp, 0)], name="hg_out")[0]
    sr = hs.shape[0]
    proj = _hg_proj(hs, w_in, lb, sr)

    def spread(a_):
        return jnp.zeros((n_s, HG_CHUNK, d), a_.dtype).at[:, 0].set(a_[:n_s]).reshape(n_s * HG_CHUNK, d)

    qs, ks, vs, lfs, gs = [spread(a_) for a_ in proj]
    ogs, sst = hgrn_scan(qs, ks, vs, lfs, gs, on, jnp.swapaxes(state.astype(F32), -1, -2), n_s, HG_CHUNK)
    ogs = jnp.pad(ogs.reshape(n_s, HG_CHUNK, d)[:, 0], ((0, sr - n_s), (0, 0)))
    xs = matmul(ogs, [(w_out, 0)], [(d, F32, 0)], _resid_epi, tm=sr, tn=512, nj=d // 512, tiles=[(xs, 0)], name="hg_out")[0]
    return xp, xs, (jnp.swapaxes(sft, -1, -2)[None], jnp.swapaxes(sst, -1, -2)[None])


def _s5_params(a_re, a_im, log_dt, b_re, b_im, c_re, c_im):
    A = lax.complex(a_re.astype(F32), a_im.astype(F32))
    dt = jnp.exp(log_dt.astype(F32))[:, None]
    Ab = jnp.exp(A * dt)
    Bb = ((Ab - 1.0) / A)[..., None] * lax.complex(b_re.astype(F32), b_im.astype(F32))
    Cc = lax.complex(c_re.astype(F32), c_im.astype(F32))
    return Ab, Bb, Cc


def _s5_mats(Ab, Bb, Cc, L):
    hp = lax.Precision.HIGHEST
    g, p = Ab.shape
    c = Bb.shape[-1]
    pw = jnp.cumprod(jnp.concatenate([jnp.ones((1, g, p), Ab.dtype), jnp.broadcast_to(Ab, (L, g, p))], axis=0), axis=0)
    kt = jnp.einsum('gcp,tgp,gpd->tgcd', Cc, pw[:L], Bb, precision=hp).real
    lag = jnp.arange(L)[None, :] - jnp.arange(L)[:, None]
    tm = jnp.where((lag >= 0)[:, :, None, None, None], kt[jnp.maximum(lag, 0)], 0.0)
    tm = jnp.transpose(tm, (2, 0, 4, 1, 3)).reshape(g, L * c, L * c)
    rc = pw[L - 1 - jnp.arange(L)][:, :, :, None] * Bb[None]
    rc = jnp.transpose(rc, (1, 0, 3, 2)).reshape(g, L * c, p)
    oc = jnp.transpose(Cc, (0, 2, 1))[:, :, None, :] * jnp.transpose(pw[1:L + 1], (1, 2, 0))[:, :, :, None]
    oc = oc.reshape(g, p, L * c)
    ab = pw[L]
    return tm, rc.real, rc.imag, oc.real, -oc.imag, ab.real, ab.imag


def _s5_kernel(u_ref, t_ref, rr_ref, ri_ref, or_ref, oi_ref, ar_ref, ai_ref, x0r_ref, x0i_ref,
               y_ref, fr_ref, fi_ref, vr_s, vi_s, xr_s, xi_s, *, gb, nch, bp, cdt):
    for g in range(gb):
        u = u_ref[g].astype(cdt)
        vr_s[g] = jnp.dot(u, rr_ref[g], preferred_element_type=F32)
        vi_s[g] = jnp.dot(u, ri_ref[g], preferred_element_type=F32)
    ar = ar_ref[...]
    ai = ai_ref[...]

    def step(n, carry):
        xr, xi = carry
        r0 = pl.multiple_of(n * bp, bp)
        xr_s[:, pl.ds(r0, bp), :] = xr
        xi_s[:, pl.ds(r0, bp), :] = xi
        vr = vr_s[:, pl.ds(r0, bp), :]
        vi = vi_s[:, pl.ds(r0, bp), :]
        return ar * xr - ai * xi + vr, ar * xi + ai * xr + vi

    xr, xi = lax.fori_loop(0, nch, step, (x0r_ref[...], x0i_ref[...]))
    fr_ref[...] = xr
    fi_ref[...] = xi
    for g in range(gb):
        u = u_ref[g].astype(cdt)
        y = jnp.dot(u, t_ref[g], preferred_element_type=F32)
        y = y + jnp.dot(xr_s[g].astype(cdt), or_ref[g], preferred_element_type=F32)
        y = y + jnp.dot(xi_s[g].astype(cdt), oi_ref[g], preferred_element_type=F32)
        y_ref[g] = y


def s5_scan(u, x0, Ab, Bb, Cc, L, cdt):
    bq, t, ng, c = u.shape
    p = Ab.shape[-1]
    bp = -(-bq // 8) * 8
    nch = t // L
    w = L * c
    rows = nch * bp
    mats = _s5_mats(Ab, Bb, Cc, L)
    tm, rr, ri, orr, oi = [m.astype(cdt) for m in mats[:5]]
    ar, ai = [m.reshape(ng, 1, p) for m in mats[5:]]
    up = jnp.pad(u, ((0, bp - bq), (0, 0), (0, 0), (0, 0)))
    ug = jnp.transpose(up.reshape(bp, nch, L, ng, c), (3, 1, 0, 2, 4)).reshape(ng, rows, w)
    x0p = jnp.pad(x0.astype(F32), ((0, bp - bq), (0, 0), (0, 0), (0, 0)))
    x0r = jnp.transpose(x0p[..., 0], (1, 0, 2))
    x0i = jnp.transpose(x0p[..., 1], (1, 0, 2))
    gb = 4
    kern = functools.partial(_s5_kernel, gb=gb, nch=nch, bp=bp, cdt=cdt)

    def spec(*shape):
        return pl.BlockSpec((gb,) + shape, lambda i: (i,) + (0,) * len(shape))

    y, fr, fi = pl.pallas_call(
        kern,
        grid=(ng // gb,),
        in_specs=[spec(rows, w), spec(w, w), spec(w, p), spec(w, p), spec(p, w), spec(p, w),
                  spec(1, p), spec(1, p), spec(bp, p), spec(bp, p)],
        out_specs=[spec(rows, w), spec(bp, p), spec(bp, p)],
        out_shape=[jax.ShapeDtypeStruct((ng, rows, w), F32), jax.ShapeDtypeStruct((ng, bp, p), F32),
                   jax.ShapeDtypeStruct((ng, bp, p), F32)],
        scratch_shapes=[pltpu.VMEM((gb, rows, p), F32)] * 4,
        compiler_params=_cparams(("parallel",), gb * rows * (4 * w * 4 + 4 * LANES * 4)),
        name="s5_scan",
    )(ug, tm, rr, ri, orr, oi, ar, ai, x0r, x0i)
    y = jnp.transpose(y.reshape(ng, nch, bp, L, c), (2, 1, 3, 0, 4)).reshape(bp, t, ng, c)[:bq]
    xf = jnp.stack([jnp.transpose(fr, (1, 0, 2)), jnp.transpose(fi, (1, 0, 2))], axis=-1)[:bq]
    return y, xf


def _s5_act_kernel(y_ref, u_ref, d_ref, o_ref):
    o_ref[...] = jax.nn.gelu(y_ref[...] + d_ref[...] * u_ref[...]).astype(o_ref.dtype)


def s5_act(y, u, dvec):
    n, d = y.shape
    tm = min(n, 512)
    tile = pl.BlockSpec((tm, d), lambda i: (i, 0))
    return pl.pallas_call(
        _s5_act_kernel, grid=(n // tm,),
        in_specs=[tile, tile, pl.BlockSpec((1, d), lambda i: (0, 0))], out_specs=tile,
        out_shape=jax.ShapeDtypeStruct((n, d), BF16),
        compiler_params=_cparams(("parallel",), 2 * tm * d * 10), name="s5_act",
    )(y, u, dvec.reshape(1, d).astype(F32))


def _glu_epi(accs, vecs, tiles, j):
    return [accs[0] * jax.nn.sigmoid(accs[1]) + tiles[0]]


def s5_layer(xp, xs, up, us, b, t, n_s, st, a_re, a_im, log_dt, b_re, b_im, c_re, c_im, dvec, w_glu):
    n_p, d = up.shape
    ng = d // S5_CH
    Ab, Bb, Cc = _s5_params(a_re, a_im, log_dt, b_re, b_im, c_re, c_im)
    x0 = jnp.zeros((b, ng, Ab.shape[-1], 2), F32)
    y, sp = s5_scan(up.reshape(b, t, ng, S5_CH), x0, Ab, Bb, Cc, 16, BF16)
    z = s5_act(y.reshape(n_p, d), up, dvec)
    tmp = _row_tile(n_p, 1024)
    nb = d // 256
    xp = matmul(z, [(w_glu, 0), (w_glu, nb)], [(d, F32, 0)], _glu_epi, tm=tmp, tn=256, nj=nb, tiles=[(xp, 0)], name="s5_glu")[0]
    sr = us.shape[0]
    y, ss = s5_scan(us[:n_s].reshape(n_s, 1, ng, S5_CH), st, Ab, Bb, Cc, 1, F32)
    z = s5_act(jnp.pad(y.reshape(n_s, d), ((0, sr - n_s), (0, 0))), us, dvec)
    xs = matmul(z, [(w_glu, 0), (w_glu, nb)], [(d, F32, 0)], _glu_epi, tm=sr, tn=256, nj=nb, tiles=[(xs, 0)], name="s5_glu")[0]
    return xp, xs, (sp[None], ss[None])


def _t5_bucket(dist):
    n = jnp.maximum(dist, 0)
    exact = N_BUCKETS // 2
    lg = jnp.log(jnp.maximum(n, exact).astype(F32) / exact) / math.log(MAX_DIST / exact)
    large = jnp.minimum(exact + (lg * (N_BUCKETS - exact)).astype(jnp.int32), N_BUCKETS - 1)
    return jnp.where(n < exact, n, large)


def _rms(x, g):
    return x * lax.rsqrt(jnp.mean(x * x, axis=-1, keepdims=True) + EPS) * g


def _nsa_proj(h, w_in, qn, kn, tm, nkv):
    d = h.shape[1]
    kvw = nkv * HEAD_DIM
    tn = 512
    nb = d // tn
    assert kvw == tn

    def q_epi(accs, vecs, tiles, j):
        return [_head_rms(accs[0], vecs[0])]

    def rows_epi(accs, vecs, tiles, j):
        y = jnp.where(j == 2, _head_rms(accs[0], vecs[0]), accs[0])
        return [y, y]

    def win_epi(accs, vecs, tiles, j):
        y = jnp.where(j == 0, _head_rms(accs[0], vecs[0]), accs[0])
        return [y, y]

    def gate_epi(accs, vecs, tiles, j):
        return [_valid_lanes(jax.nn.sigmoid(accs[0]), 3 * (d // HEAD_DIM))]

    q16 = matmul(h, [(w_in, 0)], [(d, BF16, 0)], q_epi, tm=tm, tn=tn, nj=nb, vecs=[(jnp.tile(_vec(qn), (1, d // HEAD_DIM)), 0)], name="nsa_q")[0]
    rows, rows16 = matmul(h, [(w_in, nb)], [(4 * kvw, F32, 0), (4 * kvw, BF16, 0)], rows_epi, tm=tm, tn=tn, nj=4,
                          vecs=[(jnp.tile(_vec(kn[1]), (1, 4 * nkv)), 0)], name="nsa_rows")
    wrows, wrows16 = matmul(h, [(w_in, nb + 4)], [(2 * kvw, F32, 0), (2 * kvw, BF16, 0)], win_epi, tm=tm, tn=tn, nj=2,
                            vecs=[(jnp.tile(_vec(kn[2]), (1, 2 * nkv)), 0)], name="nsa_wrows")
    nhq = d // HEAD_DIM
    gates = matmul(h, [(w_in, (d + 6 * kvw) // LANES)], [(LANES, F32, 0)], gate_epi, tm=tm, tn=LANES, nj=1, name="nsa_gates")[0][:, :3 * nhq]
    return q16, gates, rows, wrows, rows16, wrows16


def _bias_heads(table, dist, nkv, grp):
    bb = table[_t5_bucket(dist)].astype(F32).reshape(dist.shape + (nkv, grp))
    return jnp.transpose(bb, (2, 3, 0, 1))


def _bias_group(table, dist, nkv, grp):
    tg = jnp.transpose(table.reshape(N_BUCKETS, nkv, grp), (1, 0, 2))
    hi = jnp.arange(nkv)[None, :, None, None]
    return jnp.moveaxis(tg[hi, _t5_bucket(dist)].astype(F32), -1, 2)


def _nsa_compress(rows, pos, w, w1, w2):
    B, T = rows.shape[0], rows.shape[1]
    blk = rows.reshape(B, T // CMP_BLK, CMP_BLK, rows.shape[2], HEAD_DIM) + pos[:, None, :]
    pooled = jnp.einsum('bnjhd,j->bnhd', blk, w)
    return jax.nn.silu(pooled @ w1) @ w2


def _nsa_cmp_slc(q, rows, q_pos, cmp_pos, cmp_w, cmp_w1, cmp_w2, kn_cmp, table):
    B, Tq = q.shape[0], q.shape[1]
    nkv = rows.shape[3]
    nh = q.shape[2]
    grp = nh // nkv
    T = rows.shape[1]
    nb = -(-T // CMP_BLK)
    rows = jnp.pad(rows, ((0, 0), (0, nb * CMP_BLK - T), (0, 0), (0, 0), (0, 0)))
    kc = _rms(_nsa_compress(rows[:, :, 0], cmp_pos[0], cmp_w[0], cmp_w1[0], cmp_w2[0]), kn_cmp)
    vc = _nsa_compress(rows[:, :, 1], cmp_pos[1], cmp_w[1], cmp_w1[1], cmp_w2[1])
    qg = q.reshape(B, Tq, nkv, grp, HEAD_DIM)
    scale = HEAD_DIM ** -0.5
    blk_idx = jnp.arange(nb)
    dist_c = q_pos[:, None] - (blk_idx * CMP_BLK + CMP_BLK - 1)[None, :]
    vis = dist_c >= 0
    s_c = jnp.einsum('bqhgd,bnhd->bhgqn', qg, kc).astype(F32) * scale + _bias_heads(table, dist_c, nkv, grp)
    p_c = jax.nn.softmax(jnp.where(vis, s_c, NEG), axis=-1) * vis
    o_cmp = jnp.einsum('bhgqn,bnhd->bqhgd', p_c, vc).reshape(B, Tq, nh, HEAD_DIM)
    imp = p_c.sum(axis=2)
    cur = (q_pos // SEL_BLK)[:, None]
    cand = blk_idx[None, :] <= cur
    forced = (blk_idx[None, :] == 0) | (blk_idx[None, :] == cur) | (blk_idx[None, :] == cur - 1)
    score = jnp.where(cand, jnp.where(forced, FORCE, imp), -1.0)
    n_sel = min(N_SEL, nb)
    top_v, top_i = lax.top_k(score, n_sel)
    valid = top_v >= 0.0
    ks = jnp.moveaxis(rows[:, :, 2], 2, 1)
    vs = jnp.moveaxis(rows[:, :, 3], 2, 1)
    qblk = 32 if Tq % 32 == 0 else Tq
    nqb = Tq // qblk
    bi = jnp.arange(B)[:, None, None, None]
    hi = jnp.arange(nkv)[None, :, None, None]

    def sel_block(args):
        qb, ib, vb, pb = args
        kpos = (ib[..., None] * SEL_BLK + jnp.arange(SEL_BLK)).reshape(B, nkv, qblk, n_sel * SEL_BLK)
        kmask = jnp.repeat(vb, SEL_BLK, axis=-1)
        kg = ks[bi, hi, kpos]
        vg = vs[bi, hi, kpos]
        dist = pb[None, None, :, None] - kpos
        mask = (kmask & (dist >= 0))[:, :, None]
        s = jnp.einsum('bqhgd,bhqnd->bhgqn', qb, kg).astype(F32) * scale + _bias_group(table, dist, nkv, grp)
        p = jax.nn.softmax(jnp.where(mask, s, NEG), axis=-1)
        return jnp.einsum('bhgqn,bhqnd->bqhgd', p, vg)

    qs = jnp.moveaxis(qg.reshape(B, nqb, qblk, nkv, grp, HEAD_DIM), 1, 0)
    i_s = jnp.moveaxis(top_i.reshape(B, nkv, nqb, qblk, n_sel), 2, 0)
    v_s = jnp.moveaxis(valid.reshape(B, nkv, nqb, qblk, n_sel), 2, 0)
    o_slc = lax.map(sel_block, (qs, i_s, v_s, q_pos.reshape(nqb, qblk)))
    o_slc = jnp.moveaxis(o_slc, 0, 1).reshape(B, Tq, nh, HEAD_DIM)
    return o_cmp, o_slc


def _win_attend(q, kw, vw, q_pos, k_pos, table):
    B, Q = q.shape[0], q.shape[1]
    nkv = kw.shape[2]
    grp = q.shape[2] // nkv
    qg = q.reshape(B, Q, nkv, grp, HEAD_DIM)
    dist = q_pos[:, None] - k_pos[None, :]
    mask = (dist >= 0) & (dist <= WINDOW) & (k_pos[None, :] >= 0)
    s = jnp.einsum('bqhgd,bkhd->bhgqk', qg, kw).astype(F32) * HEAD_DIM ** -0.5 + _bias_heads(table, dist, nkv, grp)
    p = jax.nn.softmax(jnp.where(mask, s, NEG), axis=-1)
    return jnp.einsum('bhgqk,bkhd->bqhgd', p, vw).reshape(B, Q, q.shape[2], HEAD_DIM)


def _nsa_combine(gates, o_cmp, o_slc, o_win):
    n = gates.shape[0]
    nh = o_cmp.shape[-2]
    g = gates.reshape(n, 3, nh)
    o = (g[:, 0, :, None] * o_cmp.reshape(n, nh, HEAD_DIM) + g[:, 1, :, None] * o_slc.reshape(n, nh, HEAD_DIM)
         + g[:, 2, :, None] * o_win.reshape(n, nh, HEAD_DIM))
    return o.reshape(n, nh * HEAD_DIM).astype(BF16)


def _bucket_table():
    n = np.arange(MAX_DIST + 1)
    exact = N_BUCKETS // 2
    lg = np.log(np.maximum(n, exact).astype(np.float32) / np.float32(exact)) / np.float32(math.log(MAX_DIST / exact))
    large = np.minimum(exact + (lg * (N_BUCKETS - exact)).astype(np.int32), N_BUCKETS - 1)
    return np.where(n < exact, n, large).astype(np.int32)


def _dist_bias(table, dist):
    bt = table.astype(F32)[_bucket_table()]
    return jnp.moveaxis(bt[np.clip(dist, 0, MAX_DIST)], -1, 0)


def _nsa_compress_kernel(rows_ref, pk_ref, pv_ref, posk_ref, posv_ref, w1_ref, w2_ref, kn_ref, kc_ref, vc_ref, *, nkv):
    hp = lax.Precision.HIGHEST
    kvw = nkv * HEAD_DIM
    xk = rows_ref[:, :kvw] + posk_ref[...]
    xv = rows_ref[:, kvw:2 * kvw] + posv_ref[...]
    pooled = (jnp.dot(pk_ref[...], xk, preferred_element_type=F32, precision=hp),
              jnp.dot(pv_ref[...], xv, preferred_element_type=F32, precision=hp))
    _cmp_mlp(pooled, w1_ref, w2_ref, kn_ref, kc_ref, vc_ref, nkv)


def _cmp_mlp(pooled, w1_ref, w2_ref, kn_ref, kc_ref, vc_ref, nkv):
    for which, out in ((0, kc_ref), (1, vc_ref)):
        w1 = w1_ref[which].astype(BF16)
        w2 = w2_ref[which].astype(BF16)
        for h in range(nkv):
            x = pooled[which][:, h * HEAD_DIM:(h + 1) * HEAD_DIM].astype(BF16)
            a = jnp.dot(x, w1, preferred_element_type=F32)
            y = jnp.dot((a * jax.nn.sigmoid(a)).astype(BF16), w2, preferred_element_type=F32)
            if which == 0:
                y = y * lax.rsqrt(jnp.mean(y * y, axis=-1, keepdims=True) + EPS) * kn_ref[...]
            out[:, h * HEAD_DIM:(h + 1) * HEAD_DIM] = y


def nsa_compress(rows, b, t, nkv, cmp_pos, cmp_w, cmp_w1, cmp_w2, kn_cmp):
    nb = t // CMP_BLK
    kvw = nkv * HEAD_DIM
    eye = jnp.eye(nb, dtype=F32)
    pk = jnp.kron(eye, cmp_w[0].astype(F32)[None, :])
    pv = jnp.kron(eye, cmp_w[1].astype(F32)[None, :])
    posk = jnp.tile(cmp_pos[0].astype(F32), (nb, nkv))
    posv = jnp.tile(cmp_pos[1].astype(F32), (nb, nkv))

    def full(shape):
        return pl.BlockSpec(shape, lambda bi: (0,) * len(shape))

    return pl.pallas_call(
        functools.partial(_nsa_compress_kernel, nkv=nkv),
        grid=(b,),
        in_specs=[pl.BlockSpec((t, 2 * kvw), lambda bi: (bi, 0)), full((nb, t)), full((nb, t)), full((t, kvw)),
                  full((t, kvw)), full((2, HEAD_DIM, HEAD_DIM)), full((2, HEAD_DIM, HEAD_DIM)), full((1, HEAD_DIM))],
        out_specs=[pl.BlockSpec((None, nb, kvw), lambda bi: (bi, 0, 0))] * 2,
        out_shape=[jax.ShapeDtypeStruct((b, nb, kvw), F32)] * 2,
        compiler_params=_cparams(("parallel",), 2 * t * 2 * kvw * 4 + 4 * t * kvw * 4),
        name="nsa_compress",
    )(rows, pk, pv, posk, posv, cmp_w1.astype(F32), cmp_w2.astype(F32), kn_cmp.reshape(1, HEAD_DIM).astype(F32))


def _nsa_cmp_attn_kernel(q_ref, kc_ref, vc_ref, bias_ref, g_ref, o_ref, sel_ref, *, tq, nb, grp, n_sel, scale):
    i = pl.program_id(2)
    kc = kc_ref[...].astype(BF16)
    vc = vc_ref[...].astype(BF16)
    qpos = i * tq + lax.broadcasted_iota(jnp.int32, (tq, nb), 0)
    blk = lax.broadcasted_iota(jnp.int32, (tq, nb), 1)
    vis = qpos >= blk * CMP_BLK + (CMP_BLK - 1)
    visf = jnp.where(vis, 1.0, 0.0)
    gates = g_ref[...]
    imp = jnp.zeros((tq, nb), F32)
    for g in range(grp):
        q = q_ref[:, g * HEAD_DIM:(g + 1) * HEAD_DIM]
        s = lax.dot_general(q, kc, (((1,), (1,)), ((), ())), preferred_element_type=F32) * scale + bias_ref[g]
        s = jnp.where(vis, s, NEG)
        p = jnp.exp(s - jnp.max(s, axis=-1, keepdims=True))
        p = p / jnp.sum(p, axis=-1, keepdims=True) * visf
        imp = imp + p
        o = jnp.dot(p.astype(BF16), vc, preferred_element_type=F32)
        o_ref[:, g * HEAD_DIM:(g + 1) * HEAD_DIM] = gates[:, g:g + 1] * o
    cur = qpos // SEL_BLK
    forced = jnp.logical_or(blk == 0, jnp.logical_or(blk == cur, blk == cur - 1))
    score = jnp.where(blk <= cur, jnp.where(forced, FORCE, imp), -1.0)
    rank = jnp.zeros((tq, nb), F32)
    for n in range(nb):
        col = score[:, n:n + 1]
        beats = jnp.logical_or(col > score, jnp.logical_and(col == score, blk > n))
        rank = rank + jnp.where(beats, 1.0, 0.0)
    sel_ref[...] = jnp.where(jnp.logical_and(rank < n_sel, score >= 0.0), 1.0, 0.0)


def _nsa_slc_kernel(q_ref, k_ref, v_ref, sel_ref, tz_ref, g_ref, oin_ref, o_ref, *, tq, nb, grp, scale):
    i = pl.program_id(2)
    tk = tq
    sel = sel_ref[...].astype(BF16)
    qpos = i * tq + lax.broadcasted_iota(jnp.int32, (tq, tk), 0)
    koff = lax.broadcasted_iota(jnp.int32, (tq, tk), 1)
    n_iota = lax.broadcasted_iota(jnp.int32, (nb, tk), 0)
    c_blk = lax.broadcasted_iota(jnp.int32, (nb, tk), 1) // SEL_BLK
    gates = g_ref[...]
    for g in range(grp):
        q = q_ref[:, g * HEAD_DIM:(g + 1) * HEAD_DIM]

        def body(j, carry, q=q, g=g):
            m, l, acc = carry
            start = pl.multiple_of(j * tk, tk)
            k = k_ref[pl.ds(start, tk), :]
            v = v_ref[pl.ds(start, tk), :]
            e = jnp.where(n_iota == (tk // SEL_BLK) * j + c_blk, 1.0, 0.0).astype(BF16)
            chosen = jnp.dot(sel, e, preferred_element_type=F32) > 0.5
            mask = jnp.logical_and(chosen, qpos >= koff + j * tk)
            s = lax.dot_general(q, k, (((1,), (1,)), ((), ())), preferred_element_type=F32) * scale
            s = s + tz_ref[g, jnp.minimum(i - j, 2)]
            s = jnp.where(mask, s, NEG)
            m_new = jnp.maximum(m, jnp.max(s, axis=-1, keepdims=True))
            alpha = jnp.exp(m - m_new)
            p = jnp.exp(s - m_new)
            l = alpha * l + jnp.sum(p, axis=-1, keepdims=True)
            acc = alpha * acc + jnp.dot(p.astype(BF16), v, preferred_element_type=F32)
            return m_new, l, acc

        init = (jnp.full((tq, 1), NEG, F32), jnp.zeros((tq, 1), F32), jnp.zeros((tq, HEAD_DIM), F32))
        m, l, acc = lax.fori_loop(0, i + 1, body, init)
        cs = slice(g * HEAD_DIM, (g + 1) * HEAD_DIM)
        o_ref[:, cs] = oin_ref[:, cs] + gates[:, grp + g:grp + g + 1] * (acc / l)


def _nsa_win_kernel(q_ref, k_ref, v_ref, wz_ref, g_ref, oin_ref, o_ref, *, tq, grp, scale, window):
    i = pl.program_id(2)
    span = window + tq
    start = pl.multiple_of(i * tq, tq)
    kw = k_ref[pl.ds(start, span), :]
    vw = v_ref[pl.ds(start, span), :]
    r = lax.broadcasted_iota(jnp.int32, (tq, span), 0)
    c = lax.broadcasted_iota(jnp.int32, (tq, span), 1)
    dist = r + window - c
    mask = jnp.logical_and(jnp.logical_and(dist >= 0, dist <= window), i * tq - window + c >= 0)
    gates = g_ref[...]
    for g in range(grp):
        cs = slice(g * HEAD_DIM, (g + 1) * HEAD_DIM)
        s = lax.dot_general(q_ref[:, cs], kw, (((1,), (1,)), ((), ())), preferred_element_type=F32) * scale + wz_ref[g]
        s = jnp.where(mask, s, NEG)
        p = jnp.exp(s - jnp.max(s, axis=-1, keepdims=True))
        o = jnp.dot(p.astype(BF16), vw, preferred_element_type=F32) / jnp.sum(p, axis=-1, keepdims=True)
        o_ref[:, cs] = (oin_ref[:, cs] + gates[:, 2 * grp + g:2 * grp + g + 1] * o).astype(o_ref.dtype)


def nsa_attention(q16, rows16, wrows16, gates, kc, vc, table, b, t, nkv):
    n, d = q16.shape
    nh = d // HEAD_DIM
    grp = nh // nkv
    gw = grp * HEAD_DIM
    nb = t // CMP_BLK
    scale = HEAD_DIM ** -0.5
    sem = ("parallel", "parallel", "arbitrary")
    gk = jnp.transpose(gates.reshape(n, 3, nkv, grp), (2, 0, 1, 3)).reshape(nkv, n, 3 * grp)
    tq = min(256, t)
    nq = t // tq
    dist_c = np.arange(t)[:, None] - (np.arange(nb) * CMP_BLK + CMP_BLK - 1)[None, :]
    bias_c = _dist_bias(table, dist_c)
    qspec = pl.BlockSpec((tq, gw), lambda bi, h, i: (bi * nq + i, h))
    gspec = pl.BlockSpec((None, tq, 3 * grp), lambda bi, h, i: (h, bi * nq + i, 0))
    o1, sel = pl.pallas_call(
        functools.partial(_nsa_cmp_attn_kernel, tq=tq, nb=nb, grp=grp, n_sel=min(N_SEL, nb), scale=scale),
        grid=(b, nkv, nq),
        in_specs=[qspec,
                  pl.BlockSpec((None, nb, HEAD_DIM), lambda bi, h, i: (bi, 0, h)),
                  pl.BlockSpec((None, nb, HEAD_DIM), lambda bi, h, i: (bi, 0, h)),
                  pl.BlockSpec((grp, tq, nb), lambda bi, h, i: (h, i, 0)),
                  gspec],
        out_specs=[qspec, pl.BlockSpec((None, None, tq, nb), lambda bi, h, i: (bi, h, i, 0))],
        out_shape=[jax.ShapeDtypeStruct((n, d), F32), jax.ShapeDtypeStruct((b, nkv, t, nb), F32)],
        compiler_params=_cparams(sem, 16 << 20), name="nsa_cmp_attn",
    )(q16, kc, vc, bias_c, gk)
    assert tq >= MAX_DIST
    r = np.arange(tq)
    tz = _dist_bias(table, np.stack([dd * tq + r[:, None] - r[None, :] for dd in range(3)]))
    o2 = pl.pallas_call(
        functools.partial(_nsa_slc_kernel, tq=tq, nb=nb, grp=grp, scale=scale),
        grid=(b, nkv, nq),
        in_specs=[qspec,
                  pl.BlockSpec((t, HEAD_DIM), lambda bi, h, i: (bi, 2 * nkv + h)),
                  pl.BlockSpec((t, HEAD_DIM), lambda bi, h, i: (bi, 3 * nkv + h)),
                  pl.BlockSpec((None, None, tq, nb), lambda bi, h, i: (bi, h, i, 0)),
                  pl.BlockSpec((grp, 3, tq, tq), lambda bi, h, i: (h, 0, 0, 0)),
                  gspec, qspec],
        out_specs=qspec,
        out_shape=jax.ShapeDtypeStruct((n, d), F32),
        compiler_params=_cparams(sem, 32 << 20), name="nsa_slc_attn",
    )(q16, rows16, rows16, sel, tz, gk, o1)
    tq = min(128, t)
    nq = t // tq
    r = np.arange(tq)
    qspec = pl.BlockSpec((tq, gw), lambda bi, h, i: (bi * nq + i, h))
    gspec = pl.BlockSpec((None, tq, 3 * grp), lambda bi, h, i: (h, bi * nq + i, 0))
    span = WINDOW + tq
    wz = _dist_bias(table, r[:, None] + WINDOW - np.arange(span)[None, :])
    kvw = nkv * HEAD_DIM
    kwp = jnp.pad(wrows16.reshape(b, t, 2 * kvw), ((0, 0), (WINDOW, 0), (0, 0))).reshape(b * (t + WINDOW), 2 * kvw)
    og = pl.pallas_call(
        functools.partial(_nsa_win_kernel, tq=tq, grp=grp, scale=scale, window=WINDOW),
        grid=(b, nkv, nq),
        in_specs=[qspec,
                  pl.BlockSpec((t + WINDOW, HEAD_DIM), lambda bi, h, i: (bi, h)),
                  pl.BlockSpec((t + WINDOW, HEAD_DIM), lambda bi, h, i: (bi, nkv + h)),
                  pl.BlockSpec((grp, tq, span), lambda bi, h, i: (h, 0, 0)),
                  gspec, qspec],
        out_specs=qspec,
        out_shape=jax.ShapeDtypeStruct((n, d), BF16),
        compiler_params=_cparams(sem, 24 << 20), name="nsa_win_attn",
    )(q16, kwp, kwp, wz, gk, o2)
    return og


DEC_ROWS = 16
DEC_BLOCKS = 256


def _nsa_dec_compress_kernel(pt_ref, page_ref, new_ref, wk_ref, wv_ref, posk_ref, posv_ref, w1_ref, w2_ref, kn_ref,
                             kc_ref, vc_ref, ak_s, av_s, *, nkv, n_pages, page):
    p = pl.program_id(1)
    hp = lax.Precision.HIGHEST
    kvw = nkv * HEAD_DIM

    @pl.when(p == 0)
    def _():
        ak_s[...] = jnp.zeros(ak_s.shape, F32)
        av_s[...] = jnp.zeros(av_s.shape, F32)

    x = jnp.where(p == n_pages, new_ref[...], page_ref[...])
    n_iota = lax.broadcasted_iota(jnp.int32, (DEC_BLOCKS, page), 0)
    tok_blk = lax.broadcasted_iota(jnp.int32, (DEC_BLOCKS, page), 1) // CMP_BLK
    hit = n_iota == (page // CMP_BLK) * p + tok_blk
    pk = jnp.where(hit, wk_ref[...], 0.0)
    pv = jnp.where(hit, wv_ref[...], 0.0)
    ak_s[...] += jnp.dot(pk, x[:, :kvw] + posk_ref[...], preferred_element_type=F32, precision=hp)
    av_s[...] += jnp.dot(pv, x[:, kvw:] + posv_ref[...], preferred_element_type=F32, precision=hp)

    @pl.when(p == n_pages)
    def _():
        _cmp_mlp((ak_s[...], av_s[...]), w1_ref, w2_ref, kn_ref, kc_ref, vc_ref, nkv)


def _nsa_dec_cmp_kernel(q_ref, kc_ref, vc_ref, bias_ref, g_ref, o_ref, sel_ref, *, grp, qpos, n_sel, scale):
    q = q_ref[...]
    rows = q.shape[0]
    s = lax.dot_general(q, kc_ref[...].astype(BF16), (((1,), (1,)), ((), ())), preferred_element_type=F32) * scale
    s = s + bias_ref[...]
    blk = lax.broadcasted_iota(jnp.int32, s.shape, 1)
    vis = qpos >= blk * CMP_BLK + (CMP_BLK - 1)
    s = jnp.where(vis, s, NEG)
    p = jnp.exp(s - jnp.max(s, axis=-1, keepdims=True))
    p = p / jnp.sum(p, axis=-1, keepdims=True) * jnp.where(vis, 1.0, 0.0)
    o = jnp.dot(p.astype(BF16), vc_ref[...].astype(BF16), preferred_element_type=F32)
    o_ref[...] = g_ref[...][:, 0:1] * o
    head_row = lax.broadcasted_iota(jnp.int32, s.shape, 0) < grp
    imp = jnp.broadcast_to(jnp.sum(jnp.where(head_row, p, 0.0), axis=0, keepdims=True), s.shape)
    cur = qpos // SEL_BLK
    forced = jnp.logical_or(blk == 0, jnp.logical_or(blk == cur, blk == cur - 1))
    score = jnp.where(blk <= cur, jnp.where(forced, FORCE, imp), -1.0)
    rank = jnp.zeros(s.shape, F32)
    for n in range(cur + 1):
        col = _lane_column(score, blk, n)
        beats = jnp.logical_or(col > score, jnp.logical_and(col == score, blk > n))
        rank = rank + jnp.where(beats, 1.0, 0.0)
    sel_ref[...] = jnp.where(jnp.logical_and(rank < n_sel, score >= 0.0), 1.0, 0.0)


def _nsa_dec_slc_kernel(pt_ref, q_ref, k_ref, v_ref, sel_ref, blast_ref, bfar_ref, b0_ref, kn_ref, vn_ref, g_ref,
                        oin_ref, o_ref, m_s, l_s, acc_s, *, n_pages, page, scale):
    p = pl.program_id(2)

    @pl.when(p == 0)
    def _():
        m_s[...] = jnp.full(m_s.shape, NEG, F32)
        l_s[...] = jnp.zeros(l_s.shape, F32)
        acc_s[...] = jnp.zeros(acc_s.shape, F32)

    q = q_ref[...]
    sel = sel_ref[...]
    n_iota = lax.broadcasted_iota(jnp.int32, (DEC_BLOCKS, page), 0)
    tok_blk = lax.broadcasted_iota(jnp.int32, (DEC_BLOCKS, page), 1) // SEL_BLK
    e = jnp.where(n_iota == (page // SEL_BLK) * p + tok_blk, 1.0, 0.0).astype(BF16)
    chosen = jnp.dot(sel.astype(BF16), e, preferred_element_type=F32) > 0.5
    s = lax.dot_general(q, k_ref[...].astype(BF16), (((1,), (1,)), ((), ())), preferred_element_type=F32) * scale
    s = s + jnp.where(p == n_pages - 1, blast_ref[...], bfar_ref[...])
    s = jnp.where(chosen, s, NEG)
    m_old = m_s[...]
    m_new = jnp.maximum(m_old, jnp.max(s, axis=-1, keepdims=True))
    alpha = jnp.exp(m_old - m_new)
    pm = jnp.exp(s - m_new)
    l_s[...] = alpha * l_s[...] + jnp.sum(pm, axis=-1, keepdims=True)
    acc_s[...] = alpha * acc_s[...] + jnp.dot(pm.astype(BF16), v_ref[...].astype(BF16), preferred_element_type=F32)
    m_s[...] = m_new

    @pl.when(p == n_pages - 1)
    def _():
        nblk = n_pages * (page // SEL_BLK)
        s_n = jnp.sum(q.astype(F32) * kn_ref[...], axis=-1, keepdims=True) * scale + b0_ref[...]
        sel_new = _lane_column(sel, lax.broadcasted_iota(jnp.int32, sel.shape, 1), nblk)
        s_n = jnp.where(sel_new > 0.5, s_n, NEG)
        m_f = jnp.maximum(m_s[...], s_n)
        a_f = jnp.exp(m_s[...] - m_f)
        p_n = jnp.exp(s_n - m_f)
        l_f = a_f * l_s[...] + p_n
        acc = a_f * acc_s[...] + p_n.astype(BF16).astype(F32) * vn_ref[...]
        o_ref[...] = oin_ref[...] + g_ref[...][:, 1:2] * (acc / l_f)


def _nsa_dec_win_kernel(q_ref, k_ref, v_ref, bias_ref, b0_ref, kn_ref, vn_ref, g_ref, oin_ref, o_ref, *, scale):
    q = q_ref[...]
    s = lax.dot_general(q, k_ref[...].astype(BF16), (((1,), (1,)), ((), ())), preferred_element_type=F32) * scale
    s = s + bias_ref[...]
    s_n = jnp.sum(q.astype(F32) * kn_ref[...], axis=-1, keepdims=True) * scale + b0_ref[...]
    m = jnp.maximum(jnp.max(s, axis=-1, keepdims=True), s_n)
    pm = jnp.exp(s - m)
    p_n = jnp.exp(s_n - m)
    l = jnp.sum(pm, axis=-1, keepdims=True) + p_n
    o = jnp.dot(pm.astype(BF16), v_ref[...].astype(BF16), preferred_element_type=F32)
    o = (o + p_n.astype(BF16).astype(F32) * vn_ref[...]) / l
    o_ref[...] = oin_ref[...] + g_ref[...][:, 2:3] * o


def nsa_decode(q16, gates, rows_new, wrows_new, pool, win_buf, page_table, cmp_pos, cmp_w, cmp_w1, cmp_w2, kn_cmp, table):
    bq, d = q16.shape
    nkv = pool.shape[-2]
    nh = d // HEAD_DIM
    grp = nh // nkv
    kvw = nkv * HEAD_DIM
    n_phys, page = pool.shape[0], pool.shape[1]
    n_pages = page_table.shape[1]
    p_len = n_pages * page
    lb = win_buf.shape[1]
    assert grp <= DEC_ROWS and p_len // CMP_BLK + 1 <= DEC_BLOCKS and page % CMP_BLK == 0 and lb <= min(WINDOW, p_len)
    scale = HEAD_DIM ** -0.5
    pt = page_table.reshape(-1).astype(jnp.int32)
    pool3 = pool.reshape(n_phys, page, 4 * kvw)
    rpad = ((0, 0), (0, 0), (0, DEC_ROWS - grp), (0, 0))
    qg = jnp.pad(q16.reshape(bq, nkv, grp, HEAD_DIM), rpad)
    gg = jnp.pad(jnp.transpose(gates.reshape(bq, 3, nkv, grp), (0, 2, 3, 1)), rpad)

    def head_rows(bias):
        return jnp.pad(bias.reshape(nkv, grp, -1), ((0, 0), (0, DEC_ROWS - grp), (0, 0)))

    reps = page // CMP_BLK
    newpage = jnp.zeros((bq, page, 2 * kvw), F32).at[:, 0].set(rows_new[:, :2 * kvw])
    cspec = lambda shape: pl.BlockSpec(shape, lambda b, p, t_: (0,) * len(shape))
    kc, vc = pl.pallas_call(
        functools.partial(_nsa_dec_compress_kernel, nkv=nkv, n_pages=n_pages, page=page),
        grid_spec=pltpu.PrefetchScalarGridSpec(
            num_scalar_prefetch=1, grid=(bq, n_pages + 1),
            in_specs=[pl.BlockSpec((None, page, 2 * kvw), lambda b, p, t_: (t_[b * n_pages + jnp.minimum(p, n_pages - 1)], 0, 0)),
                      pl.BlockSpec((None, page, 2 * kvw), lambda b, p, t_: (b, 0, 0)),
                      cspec((1, page)), cspec((1, page)), cspec((page, kvw)), cspec((page, kvw)),
                      cspec((2, HEAD_DIM, HEAD_DIM)), cspec((2, HEAD_DIM, HEAD_DIM)), cspec((1, HEAD_DIM))],
            out_specs=[pl.BlockSpec((None, DEC_BLOCKS, kvw), lambda b, p, t_: (b, 0, 0))] * 2,
            scratch_shapes=[pltpu.VMEM((DEC_BLOCKS, kvw), F32)] * 2),
        out_shape=[jax.ShapeDtypeStruct((bq, DEC_BLOCKS, kvw), F32)] * 2,
        compiler_params=_cparams(("parallel", "arbitrary"), 16 << 20), name="nsa_dec_compress",
    )(pt, pool3, newpage, jnp.tile(cmp_w[0].astype(F32), reps)[None], jnp.tile(cmp_w[1].astype(F32), reps)[None],
      jnp.tile(cmp_pos[0].astype(F32), (reps, nkv)), jnp.tile(cmp_pos[1].astype(F32), (reps, nkv)),
      cmp_w1.astype(F32), cmp_w2.astype(F32), kn_cmp.reshape(1, HEAD_DIM).astype(F32))
    blk_end = np.arange(DEC_BLOCKS) * CMP_BLK + CMP_BLK - 1
    bias_c = head_rows(_dist_bias(table, p_len - blk_end))
    hspec = lambda n: pl.BlockSpec((None, None, DEC_ROWS, n), lambda b, h: (b, h, 0, 0))
    bspec = lambda n: pl.BlockSpec((None, DEC_ROWS, n), lambda b, h: (h, 0, 0))
    o1, sel = pl.pallas_call(
        functools.partial(_nsa_dec_cmp_kernel, grp=grp, qpos=p_len, n_sel=min(N_SEL, p_len // CMP_BLK + 1), scale=scale),
        grid=(bq, nkv),
        in_specs=[hspec(HEAD_DIM), pl.BlockSpec((None, DEC_BLOCKS, HEAD_DIM), lambda b, h: (b, 0, h)),
                  pl.BlockSpec((None, DEC_BLOCKS, HEAD_DIM), lambda b, h: (b, 0, h)), bspec(DEC_BLOCKS), hspec(3)],
        out_specs=[hspec(HEAD_DIM), hspec(DEC_BLOCKS)],
        out_shape=[jax.ShapeDtypeStruct((bq, nkv, DEC_ROWS, HEAD_DIM), F32),
                   jax.ShapeDtypeStruct((bq, nkv, DEC_ROWS, DEC_BLOCKS), F32)],
        compiler_params=_cparams(("parallel", "parallel"), 8 << 20), name="nsa_dec_cmp",
    )(qg, kc, vc, bias_c, gg)
    blast = head_rows(_dist_bias(table, page - np.arange(page)))
    bfar = head_rows(_dist_bias(table, np.array([MAX_DIST])))
    b0 = head_rows(_dist_bias(table, np.array([0])))
    hspec3 = lambda n: pl.BlockSpec((None, None, DEC_ROWS, n), lambda b, h, p, t_: (b, h, 0, 0))
    bspec3 = lambda n: pl.BlockSpec((None, DEC_ROWS, n), lambda b, h, p, t_: (h, 0, 0))
    nspec3 = pl.BlockSpec((None, None, 1, HEAD_DIM), lambda b, h, p, t_: (b, h, 0, 0))
    ksn = rows_new[:, 2 * kvw:3 * kvw].reshape(bq, nkv, 1, HEAD_DIM)
    vsn = rows_new[:, 3 * kvw:].reshape(bq, nkv, 1, HEAD_DIM)
    o2 = pl.pallas_call(
        functools.partial(_nsa_dec_slc_kernel, n_pages=n_pages, page=page, scale=scale),
        grid_spec=pltpu.PrefetchScalarGridSpec(
            num_scalar_prefetch=1, grid=(bq, nkv, n_pages),
            in_specs=[hspec3(HEAD_DIM),
                      pl.BlockSpec((None, page, HEAD_DIM), lambda b, h, p, t_: (t_[b * n_pages + p], 0, 2 * nkv + h)),
                      pl.BlockSpec((None, page, HEAD_DIM), lambda b, h, p, t_: (t_[b * n_pages + p], 0, 3 * nkv + h)),
                      hspec3(DEC_BLOCKS), bspec3(page), bspec3(1), bspec3(1), nspec3, nspec3, hspec3(3), hspec3(HEAD_DIM)],
            out_specs=hspec3(HEAD_DIM),
            scratch_shapes=[pltpu.VMEM((DEC_ROWS, 1), F32), pltpu.VMEM((DEC_ROWS, 1), F32), pltpu.VMEM((DEC_ROWS, HEAD_DIM), F32)]),
        out_shape=jax.ShapeDtypeStruct((bq, nkv, DEC_ROWS, HEAD_DIM), F32),
        compiler_params=_cparams(("parallel", "parallel", "arbitrary"), 8 << 20), name="nsa_dec_slc",
    )(pt, qg, pool3, pool3, sel, blast, bfar, b0, ksn, vsn, gg, o1)
    kpos_ok = np.arange(lb) >= lb - p_len
    assert kpos_ok.all()
    bias_w = head_rows(_dist_bias(table, lb - np.arange(lb)))
    win3 = win_buf.reshape(bq, lb, 2 * kvw)
    nspec2 = pl.BlockSpec((None, None, 1, HEAD_DIM), lambda b, h: (b, h, 0, 0))
    kwn = wrows_new[:, :kvw].reshape(bq, nkv, 1, HEAD_DIM)
    vwn = wrows_new[:, kvw:].reshape(bq, nkv, 1, HEAD_DIM)
    o3 = pl.pallas_call(
        functools.partial(_nsa_dec_win_kernel, scale=scale),
        grid=(bq, nkv),
        in_specs=[hspec(HEAD_DIM), pl.BlockSpec((None, lb, HEAD_DIM), lambda b, h: (b, 0, h)),
                  pl.BlockSpec((None, lb, HEAD_DIM), lambda b, h: (b, 0, nkv + h)), bspec(lb), bspec(1),
                  nspec2, nspec2, hspec(3), hspec(HEAD_DIM)],
        out_specs=hspec(HEAD_DIM),
        out_shape=jax.ShapeDtypeStruct((bq, nkv, DEC_ROWS, HEAD_DIM), F32),
        compiler_params=_cparams(("parallel", "parallel"), 8 << 20), name="nsa_dec_win",
    )(qg, win3, win3, bias_w, b0, kwn, vwn, gg, o2)
    return o3[:, :, :grp].reshape(bq, d).astype(BF16)


def nsa_layer(xp, xs, hp, hs, b, t, n_s, pool, win_buf, page_table, w_in, qn, kn, cmp_pos, cmp_w, cmp_w1, cmp_w2, w_out, table):
    n_p, d = hp.shape
    nh = d // HEAD_DIM
    nkv = pool.shape[-2]
    tmp = _row_tile(n_p, 1024)
    q16, gates, rows, wrows, rows16, wrows16 = _nsa_proj(hp, w_in, qn, kn, tmp, nkv)
    rows5 = rows.reshape(b, t, 4, nkv, HEAD_DIM)
    wrows5 = wrows.reshape(b, t, 2, nkv, HEAD_DIM)
    kc, vc = nsa_compress(rows, b, t, nkv, cmp_pos, cmp_w, cmp_w1, cmp_w2, kn[0])
    og = nsa_attention(q16, rows16, wrows16, gates, kc, vc, table, b, t, nkv)
    xp = matmul(og, [(w_out, 0)], [(d, F32, 0)], _resid_epi, tm=tmp, tn=512, nj=d // 512, tiles=[(xp, 0)], name="nsa_out")[0]
    lbw = min(WINDOW, t)
    out_p = (rows5[None], wrows5[:, t - lbw:][None])
    sr = hs.shape[0]
    q16, gates, rows, wrows, _, _ = _nsa_proj(hs, w_in, qn, kn, sr, nkv)
    rows_new = rows[:n_s].reshape(n_s, 1, 4, nkv, HEAD_DIM)
    wrows_new = wrows[:n_s].reshape(n_s, 1, 2, nkv, HEAD_DIM)
    lb_ = win_buf.shape[1]
    kw = jnp.concatenate([win_buf, wrows_new], axis=1)
    ogs = nsa_decode(q16[:n_s], gates[:n_s], rows[:n_s], wrows[:n_s], pool, win_buf, page_table,
                     cmp_pos, cmp_w, cmp_w1, cmp_w2, kn[0], table)
    ogs = jnp.pad(ogs, ((0, sr - n_s), (0, 0)))
    xs = matmul(ogs, [(w_out, 0)], [(d, F32, 0)], _resid_epi, tm=sr, tn=512, nj=d // 512, tiles=[(xs, 0)], name="nsa_out")[0]
    out_s = (rows_new[None], kw[:, -lb_:][None])
    return xp, xs, out_p + out_s


def kernel(x_prompt, x_sample, cache_fox_kv, cache_fox_logf, state_hgrn, state_s5, cache_nsa_kv, state_nsa_win, page_table, norm_mix, norm_ffn, fox_w_in, fox_b_f, fox_q_norm, fox_k_norm, fox_w_out, hg_w_in, hg_lb, hg_o_norm, hg_w_out, s5_a_re, s5_a_im, s5_log_dt, s5_b_re, s5_b_im, s5_c_re, s5_c_im, s5_d, s5_w_glu, nsa_w_in, nsa_q_norm, nsa_k_norm, nsa_cmp_pos, nsa_cmp_w, nsa_cmp_w1, nsa_cmp_w2, nsa_w_out, rel_bias, ffn_w1, ffn_w3, ffn_w2, moe_router, moe_router_b, moe_w1, moe_w3, moe_w2):
    b, t, d = x_prompt.shape
    n_s = x_sample.shape[0]
    assert x_sample.shape[1] == 1
    depth = norm_mix.shape[0]
    xp = x_prompt.reshape(b * t, d)
    xs = jnp.pad(x_sample.reshape(n_s, d), ((0, SAMPLE_ROWS - n_s), (0, 0)))
    sm = jax.nn.softmax(hg_lb.astype(F32), axis=0)
    lower_bounds = jnp.cumsum(sm, axis=0) - sm[0]
    res = {}
    for i in range(depth):
        j = i // 4
        kind = i % 4
        if kind == 2:
            hp = rms_norm(xp, norm_mix[i], F32)
            hs = rms_norm(xs, norm_mix[i], F32)
        else:
            hp = rms_norm(xp, norm_mix[i])
            hs = rms_norm(xs, norm_mix[i])
        if kind == 0:
            xp, xs, o = fox_layer(xp, xs, hp, hs, b, t, n_s, cache_fox_kv[j], cache_fox_logf[j], page_table,
                                  fox_w_in[j], fox_b_f[j], fox_q_norm[j], fox_k_norm[j], fox_w_out[j])
        elif kind == 1:
            xp, xs, o = hgrn_layer(xp, xs, hp, hs, b, t, n_s, state_hgrn[j], hg_w_in[j], lower_bounds[i],
                                   hg_o_norm[j], hg_w_out[j])
        elif kind == 2:
            xp, xs, o = s5_layer(xp, xs, hp, hs, b, t, n_s, state_s5[j], s5_a_re[j], s5_a_im[j], s5_log_dt[j],
                                 s5_b_re[j], s5_b_im[j], s5_c_re[j], s5_c_im[j], s5_d[j], s5_w_glu[j])
        else:
            xp, xs, o = nsa_layer(xp, xs, hp, hs, b, t, n_s, cache_nsa_kv[j], state_nsa_win[j], page_table,
                                  nsa_w_in[j], nsa_q_norm[j], nsa_k_norm[j], nsa_cmp_pos[j], nsa_cmp_w[j],
                                  nsa_cmp_w1[j], nsa_cmp_w2[j], nsa_w_out[j], rel_bias)
        res[kind] = o
        hp = rms_norm(xp, norm_ffn[i])
        hs = rms_norm(xs, norm_ffn[i])
        f = i // 2
        if i % 2 == 0:
            xp, xs = dense_ffn(xp, xs, hp, hs, ffn_w1[f], ffn_w3[f], ffn_w2[f])
        else:
            xp, xs = moe_ffn(xp, xs, hp, hs, n_s, moe_router[f], moe_router_b[f], moe_w1[f], moe_w3[f], moe_w2[f])
    return (xp.reshape(b, t, d), xs[:n_s].reshape(n_s, 1, d)) + res[0] + res[1] + res[2] + res[3]
```

```python
import functools
import math

import numpy as np
import jax
import jax.numpy as jnp
from jax import lax
from jax.experimental import pallas as pl
from jax.experimental.pallas import tpu as pltpu

F32 = jnp.float32
BF16 = jnp.bfloat16
EPS = 1e-6
NEG = -1e30
FORCE = 1e9

HEAD_DIM = 128
LANES = 128
V7X_VMEM_CAP_MB = 60
SAMPLE_ROWS = 16

HG_CHUNK = 16
S5_CH = 16
S5_CHUNK = 128
CMP_BLK = 64
SEL_BLK = 64
N_SEL = 16
WINDOW = 512
N_BUCKETS = 32
MAX_DIST = 128
TOP_K = 2
MOE_ROWS = 512


def _cparams(sem, vmem_bytes):
    mb = min(V7X_VMEM_CAP_MB, max(16, int(vmem_bytes / (1 << 20)) + 8))
    return pltpu.CompilerParams(dimension_semantics=sem, vmem_limit_bytes=mb << 20)


def _rms_kernel(x_ref, g_ref, o_ref):
    x = x_ref[...]
    y = x * lax.rsqrt(jnp.mean(x * x, axis=-1, keepdims=True) + EPS)
    o_ref[...] = (y * g_ref[...]).astype(o_ref.dtype)


def rms_norm(x, g, out_dtype=BF16):
    n, d = x.shape
    tm = min(n, 512)
    return pl.pallas_call(
        _rms_kernel,
        grid=(n // tm,),
        in_specs=[pl.BlockSpec((tm, d), lambda i: (i, 0)),
                  pl.BlockSpec((1, d), lambda i: (0, 0))],
        out_specs=pl.BlockSpec((tm, d), lambda i: (i, 0)),
        out_shape=jax.ShapeDtypeStruct((n, d), out_dtype),
        compiler_params=_cparams(("parallel",), 2 * tm * d * (4 + 4)),
        name="rms_norm",
    )(x, g.reshape(1, d).astype(F32))


def _mm_kernel(*refs, nw, nv, nt, no, ni, has_e, epilogue):
    pos = 0
    eid_ref = None
    if has_e:
        eid_ref = refs[0]
        pos = 1
    lhs_ref = refs[pos]
    pos += 1
    w_refs = refs[pos:pos + nw]
    pos += nw
    vec_refs = refs[pos:pos + nv]
    pos += nv
    tile_refs = refs[pos:pos + nt]
    pos += nt
    out_refs = refs[pos:pos + no]
    wc_ref = refs[-1]
    j = pl.program_id(0)
    i = pl.program_id(1)
    if has_e:
        prev = eid_ref[jnp.maximum(i - 1, 0)]
        recast = jnp.logical_or(i == 0, eid_ref[i] != prev)
        active = i < eid_ref[ni]
    else:
        recast = i == 0
        active = None

    @pl.when(recast)
    def _():
        for t in range(nw):
            wc_ref[t] = w_refs[t][...].astype(BF16)

    def compute():
        x = lhs_ref[...]
        accs = [jnp.dot(x, wc_ref[t], preferred_element_type=F32) for t in range(nw)]
        res = epilogue(accs, [v[...] for v in vec_refs], [t[...] for t in tile_refs], j)
        for o, r in zip(out_refs, res):
            o[...] = r.astype(o.dtype)

    if has_e:
        pl.when(active)(compute)

        @pl.when(jnp.logical_not(active))
        def _():
            for o in out_refs:
                o[...] = jnp.zeros(o.shape, o.dtype)
    else:
        compute()


def matmul(lhs, ws, outs, epilogue, *, tm, tn, nj, tk=None, kb=0, vecs=(), tiles=(), eids=None, name="mm"):
    n = lhs.shape[0]
    assert n % tm == 0, (n, tm)
    ni = n // tm
    kdim = ws[0][0].shape[-2]
    tk = tk or kdim
    assert kdim % tk == 0
    has_e = eids is not None
    nw, nv, nt, no = len(ws), len(vecs), len(tiles), len(outs)

    def wspec(arr, off):
        if arr.ndim == 3:
            return pl.BlockSpec((None, tk, tn), lambda j, i, e: (e[i], kb, j + off))
        if has_e:
            return pl.BlockSpec((tk, tn), lambda j, i, e: (kb, j + off))
        return pl.BlockSpec((tk, tn), lambda j, i: (kb, j + off))

    def fix(f):
        return (lambda j, i, e: f(j, i)) if has_e else f

    def fix_in(f):
        return (lambda j, i, e: f(j, jnp.minimum(i, e[ni] - 1))) if has_e else f

    in_specs = [pl.BlockSpec((tm, tk), fix_in(lambda j, i: (i, kb)))]
    in_specs += [wspec(a, off) for a, off in ws]
    in_specs += [pl.BlockSpec((1, tn), fix(functools.partial(lambda j, i, off: (0, j + off), off=off))) for _, off in vecs]
    in_specs += [pl.BlockSpec((tm, tn), fix_in(functools.partial(lambda j, i, off: (i, j + off), off=off))) for _, off in tiles]
    out_specs = [pl.BlockSpec((tm, tn), fix(functools.partial(lambda j, i, off: (i, j + off), off=off))) for _, _, off in outs]
    out_shape = [jax.ShapeDtypeStruct((n, c), dt) for c, dt, _ in outs]
    vmem = (2 * tm * tk * 2 + nw * (2 * tk * tn * 4 + tk * tn * 2)
            + sum(2 * tm * tn * jnp.dtype(dt).itemsize for _, dt, _ in outs)
            + nt * 2 * tm * tn * 4 + (nw + 1) * tm * tn * 4)
    kern = functools.partial(_mm_kernel, nw=nw, nv=nv, nt=nt, no=no, ni=ni, has_e=has_e, epilogue=epilogue)
    scratch = [pltpu.VMEM((nw, tk, tn), BF16)]
    args = [lhs] + [a for a, _ in ws] + [a for a, _ in vecs] + [a for a, _ in tiles]
    if has_e:
        grid_spec = pltpu.PrefetchScalarGridSpec(num_scalar_prefetch=1, grid=(nj, ni), in_specs=in_specs,
                                                 out_specs=out_specs, scratch_shapes=scratch)
        args = [eids] + args
    else:
        grid_spec = pl.GridSpec(grid=(nj, ni), in_specs=in_specs, out_specs=out_specs, scratch_shapes=scratch)
    return pl.pallas_call(kern, grid_spec=grid_spec, out_shape=out_shape,
                          compiler_params=_cparams(("arbitrary", "arbitrary"), vmem), name=name)(*args)


def _row_tile(n, pref):
    return pref if n % pref == 0 else n


def _head_rms(x, g):
    parts = []
    for h in range(x.shape[1] // HEAD_DIM):
        xh = x[:, h * HEAD_DIM:(h + 1) * HEAD_DIM]
        gh = g[:, h * HEAD_DIM:(h + 1) * HEAD_DIM]
        parts.append(xh * lax.rsqrt(jnp.mean(xh * xh, axis=-1, keepdims=True) + EPS) * gh)
    return parts[0] if len(parts) == 1 else jnp.concatenate(parts, axis=-1)


def _vec(v, width=None):
    v = v.reshape(1, -1).astype(F32)
    if width is not None and v.shape[1] < width:
        v = jnp.pad(v, ((0, 0), (0, width - v.shape[1])))
    return v


def _valid_lanes(x, n):
    return jnp.where(lax.broadcasted_iota(jnp.int32, x.shape, 1) < n, x, 0.0)


def _log_sigmoid(x):
    return jnp.minimum(x, 0.0) - jnp.log(1.0 + jnp.exp(-jnp.abs(x)))


def _swiglu_epi(accs, vecs, tiles, j):
    a, b = accs
    return [a * jax.nn.sigmoid(a) * b]


def _resid_epi(accs, vecs, tiles, j):
    return [accs[0] + tiles[0]]


def _plain_epi(accs, vecs, tiles, j):
    return [accs[0]]


def ffn_up(h, w1, w3, eids=None, tm=1024):
    dff = w1.shape[-1]
    tn = 256
    assert dff % tn == 0
    return matmul(h, [(w1, 0), (w3, 0)], [(dff, BF16, 0)], _swiglu_epi, tm=tm, tn=tn, nj=dff // tn,
                  eids=eids, name="ffn_up")[0]


def ffn_down(hmid, w2, resid, eids=None, tm=512):
    dff, d = w2.shape[-2], w2.shape[-1]
    halves = 2 if dff % (2 * LANES) == 0 and dff > 4096 else 1
    tk = dff // halves
    tn = min(512, d)
    out = resid
    for kb in range(halves):
        out = matmul(hmid, [(w2, 0)], [(d, F32, 0)], _resid_epi if out is not None else _plain_epi, tm=tm, tn=tn,
                     nj=d // tn, tk=tk, kb=kb, tiles=[(out, 0)] if out is not None else [], eids=eids, name="ffn_down")[0]
    return out


def dense_ffn(xp, xs, hp, hs, w1, w3, w2):
    mp = ffn_up(hp, w1, w3, tm=_row_tile(hp.shape[0], 1024))
    xp = ffn_down(mp, w2, xp, tm=_row_tile(hp.shape[0], 512))
    ms = ffn_up(hs, w1, w3, tm=hs.shape[0])
    xs = ffn_down(ms, w2, xs, tm=hs.shape[0])
    return xp, xs


def _router_epi(accs, vecs, tiles, j):
    return [accs[0] + vecs[0]]


def moe_ffn(xp, xs, hp, hs, n_s, router, router_b, w1, w3, w2):
    n_p, d = hp.shape
    n_e = router.shape[-1]
    h_all = jnp.concatenate([hp, hs[:n_s]], axis=0)
    n = n_p + n_s
    rw = jnp.pad(router, ((0, 0), (0, LANES - n_e)))
    lp = matmul(hp, [(rw, 0)], [(LANES, F32, 0)], _router_epi, tm=_row_tile(n_p, 1024), tn=LANES, nj=1,
                vecs=[(_vec(router_b, LANES), 0)], name="router")[0]
    ls = matmul(hs, [(rw, 0)], [(LANES, F32, 0)], _router_epi, tm=hs.shape[0], tn=LANES, nj=1,
                vecs=[(_vec(router_b, LANES), 0)], name="router")[0]
    logits = jnp.concatenate([lp[:, :n_e], ls[:n_s, :n_e]], axis=0)
    top_l, top_e = lax.top_k(logits, TOP_K)
    gate = jax.nn.softmax(top_l, axis=-1)
    a = n * TOP_K
    bm = MOE_ROWS
    e_flat = top_e.reshape(-1)
    tok = jnp.arange(a) // TOP_K
    order = jnp.argsort(e_flat)
    e_sorted = e_flat[order]
    tok_sorted = tok[order]
    counts = jnp.bincount(e_flat, length=n_e)
    padded = (counts + bm - 1) // bm * bm
    pad_end = jnp.cumsum(padded)
    grp_start = jnp.cumsum(counts) - counts
    dest = (pad_end - padded)[e_sorted] + jnp.arange(a) - grp_start[e_sorted]
    n_blocks = -(-a // bm) + n_e
    blk_e = jnp.minimum(jnp.searchsorted(pad_end, jnp.arange(n_blocks) * bm, side='right'), n_e - 1)
    n_act = pad_end[-1] // bm
    eids = jnp.concatenate([blk_e, n_act[None]]).astype(jnp.int32)
    src = jnp.full((n_blocks * bm,), -1, jnp.int32).at[dest].set(tok_sorted.astype(jnp.int32))
    pos = jnp.zeros((a,), jnp.int32).at[order].set(dest.astype(jnp.int32)).reshape(n, TOP_K)
    buf = moe_dispatch(h_all, src, bm)
    mid = ffn_up(buf, w1, w3, eids=eids, tm=bm)
    out = ffn_down(mid, w2, None, eids=eids, tm=bm)
    xp = moe_combine(xp, out, pos[:n_p], gate[:n_p])
    sr = xs.shape[0]
    xs = moe_combine(xs, out, jnp.pad(pos[n_p:], ((0, sr - n_s), (0, 0))), jnp.pad(gate[n_p:], ((0, sr - n_s), (0, 0))))
    return xp, xs


def _moe_dispatch_kernel(src_ref, h_ref, o_ref, sem, *, bm):
    base = pl.program_id(0) * bm

    def row_copy(r, s):
        return pltpu.make_async_copy(h_ref.at[pl.ds(s, 1), :], o_ref.at[pl.ds(r, 1), :], sem)

    def start(r, c):
        s = src_ref[base + r]

        @pl.when(s >= 0)
        def _():
            row_copy(r, s).start()

        @pl.when(s < 0)
        def _():
            o_ref[pl.ds(r, 1), :] = jnp.zeros((1, o_ref.shape[1]), o_ref.dtype)
        return c

    def wait(r, c):
        s = src_ref[base + r]

        @pl.when(s >= 0)
        def _():
            row_copy(r, s).wait()
        return c

    lax.fori_loop(0, bm, start, 0)
    lax.fori_loop(0, bm, wait, 0)


def moe_dispatch(h, src, bm):
    n, d = h.shape
    r = src.shape[0]
    h32 = lax.bitcast_convert_type(h.reshape(n, d // 2, 2), jnp.uint32)
    out = pl.pallas_call(
        functools.partial(_moe_dispatch_kernel, bm=bm),
        grid_spec=pltpu.PrefetchScalarGridSpec(
            num_scalar_prefetch=1, grid=(r // bm,),
            in_specs=[pl.BlockSpec(memory_space=pl.ANY)],
            out_specs=pl.BlockSpec((bm, d // 2), lambda i, s: (i, 0)),
            scratch_shapes=[pltpu.SemaphoreType.DMA(())]),
        out_shape=jax.ShapeDtypeStruct((r, d // 2), jnp.uint32),
        compiler_params=_cparams(("arbitrary",), 2 * bm * d * 2), name="moe_dispatch",
    )(src, h32)
    return lax.bitcast_convert_type(out, BF16).reshape(r, d)


def _moe_combine_kernel(pos_ref, x_ref, g_ref, y_ref, o_ref, buf, sem, *, tm, top_k):
    base = pl.program_id(0) * tm

    def row_copy(r, k):
        return pltpu.make_async_copy(y_ref.at[pl.ds(pos_ref[(base + r) * top_k + k], 1), :],
                                     buf.at[k, pl.ds(r, 1), :], sem)

    def start(r, c):
        for k in range(top_k):
            row_copy(r, k).start()
        return c

    def wait(r, c):
        for k in range(top_k):
            row_copy(r, k).wait()
        return c

    lax.fori_loop(0, tm, start, 0)
    lax.fori_loop(0, tm, wait, 0)
    g = g_ref[...]
    acc = x_ref[...]
    for k in range(top_k):
        acc = acc + g[:, k:k + 1] * buf[k]
    o_ref[...] = acc


def moe_combine(x, y, pos, gate):
    n, d = x.shape
    top_k = pos.shape[1]
    tm = min(n, 256)
    tile = pl.BlockSpec((tm, d), lambda i, p: (i, 0))
    return pl.pallas_call(
        functools.partial(_moe_combine_kernel, tm=tm, top_k=top_k),
        grid_spec=pltpu.PrefetchScalarGridSpec(
            num_scalar_prefetch=1, grid=(n // tm,),
            in_specs=[tile, pl.BlockSpec((tm, top_k), lambda i, p: (i, 0)), pl.BlockSpec(memory_space=pl.ANY)],
            out_specs=tile,
            scratch_shapes=[pltpu.VMEM((top_k, tm, d), F32), pltpu.SemaphoreType.DMA(())]),
        out_shape=jax.ShapeDtypeStruct((n, d), F32),
        compiler_params=_cparams(("arbitrary",), (4 + top_k) * tm * d * 4), name="moe_combine",
    )(pos.reshape(-1).astype(jnp.int32), x, gate.astype(F32), y)


def _cumsum_kernel(x_ref, o_ref):
    c = x_ref[...]
    row = lax.broadcasted_iota(jnp.int32, c.shape, 0)
    sh = 1
    while sh < c.shape[0]:
        c = c + jnp.where(row >= sh, pltpu.roll(c, sh, 0), 0.0)
        sh *= 2
    o_ref[...] = c


def seq_cumsum(x):
    b, t, h = x.shape
    spec = pl.BlockSpec((None, t, h), lambda i: (i, 0, 0))
    return pl.pallas_call(_cumsum_kernel, grid=(b,), in_specs=[spec], out_specs=spec,
                          out_shape=jax.ShapeDtypeStruct(x.shape, F32),
                          compiler_params=_cparams(("parallel",), 8 * t * LANES * 4), name="seq_cumsum")(x)


def _fox_attn_kernel(q_ref, k_ref, v_ref, ccol_ref, crow_ref, g_ref, o_ref, *, tq, tk, scale):
    qi = pl.program_id(2)
    q = q_ref[...]
    cq = ccol_ref[...]
    qpos = qi * tq + lax.broadcasted_iota(jnp.int32, (tq, tk), 0)
    koff = lax.broadcasted_iota(jnp.int32, (tq, tk), 1)

    def body(j, carry):
        m, l, acc = carry
        start = pl.multiple_of(j * tk, tk)
        k = k_ref[pl.ds(start, tk), :]
        v = v_ref[pl.ds(start, tk), :]
        s = lax.dot_general(q, k, (((1,), (1,)), ((), ())), preferred_element_type=F32) * scale
        s = s + cq - crow_ref[pl.ds(j, 1), :]
        s = jnp.where(qpos >= koff + j * tk, s, NEG)
        m_new = jnp.maximum(m, jnp.max(s, axis=-1, keepdims=True))
        alpha = jnp.exp(m - m_new)
        p = jnp.exp(s - m_new)
        l = alpha * l + jnp.sum(p, axis=-1, keepdims=True)
        acc = alpha * acc + jnp.dot(p.astype(BF16), v, preferred_element_type=F32)
        return m_new, l, acc

    n_kv = (qi * tq + tq + tk - 1) // tk
    init = (jnp.full((tq, 1), NEG, F32), jnp.zeros((tq, 1), F32), jnp.zeros((tq, HEAD_DIM), F32))
    m, l, acc = lax.fori_loop(0, n_kv, body, init)
    o_ref[...] = (acc / l * g_ref[...]).astype(o_ref.dtype)


def fox_attention(q16, kv16, c, gate, b, t, nh):
    tq = min(256, t)
    tk = tq
    nq = t // tq
    ccol = jnp.transpose(c, (0, 2, 1)).reshape(b, nh, t, 1)
    crow = jnp.transpose(c, (0, 2, 1)).reshape(b, nh, t // tk, tk)
    kern = functools.partial(_fox_attn_kernel, tq=tq, tk=tk, scale=HEAD_DIM ** -0.5)
    return pl.pallas_call(
        kern,
        grid=(b, nh, nq),
        in_specs=[pl.BlockSpec((tq, HEAD_DIM), lambda bi, h, i: (bi * nq + i, h)),
                  pl.BlockSpec((t, HEAD_DIM), lambda bi, h, i: (bi, h)),
                  pl.BlockSpec((t, HEAD_DIM), lambda bi, h, i: (bi, nh + h)),
                  pl.BlockSpec((None, None, tq, 1), lambda bi, h, i: (bi, h, i, 0)),
                  pl.BlockSpec((None, None, t // tk, tk), lambda bi, h, i: (bi, h, 0, 0)),
                  pl.BlockSpec((tq, HEAD_DIM), lambda bi, h, i: (bi * nq + i, h))],
        out_specs=pl.BlockSpec((tq, HEAD_DIM), lambda bi, h, i: (bi * nq + i, h)),
        out_shape=jax.ShapeDtypeStruct(q16.shape, BF16),
        compiler_params=_cparams(("parallel", "parallel", "arbitrary"), 8 << 20),
        name="fox_attention",
    )(q16, kv16, kv16, ccol, crow, gate)


def _fox_proj(h, w_in, b_f, qn, kn, tm):
    d = h.shape[1]
    nh = d // HEAD_DIM
    tn = 512
    nb = d // tn
    qn_v, kn_v = _vec(qn), _vec(kn)

    def q_epi(accs, vecs, tiles, j):
        return [_head_rms(accs[0], vecs[0])]

    def kv_epi(accs, vecs, tiles, j):
        y = jnp.where(j < nb, _head_rms(accs[0], vecs[0]), accs[0])
        return [y, y]

    def gate_epi(accs, vecs, tiles, j):
        return [jax.nn.sigmoid(accs[0])]

    def lf_epi(accs, vecs, tiles, j):
        return [_valid_lanes(_log_sigmoid(accs[0] + vecs[0]), nh)]

    def tiled(v):
        return jnp.tile(v, (1, 2 * d // HEAD_DIM))

    q16 = matmul(h, [(w_in, 0)], [(d, BF16, 0)], q_epi, tm=tm, tn=tn, nj=nb, vecs=[(tiled(qn_v), 0)], name="fox_q")[0]
    kv, kv16 = matmul(h, [(w_in, nb)], [(2 * d, F32, 0), (2 * d, BF16, 0)], kv_epi, tm=tm, tn=tn, nj=2 * nb,
                      vecs=[(tiled(kn_v), 0)], name="fox_kv")
    gate = matmul(h, [(w_in, 3 * nb)], [(d, F32, 0)], gate_epi, tm=tm, tn=tn, nj=nb, name="fox_gate")[0]
    lf = matmul(h, [(w_in, 4 * d // LANES)], [(LANES, F32, 0)], lf_epi, tm=tm, tn=LANES, nj=1,
                vecs=[(_vec(b_f, LANES), 0)], name="fox_logf")[0][:, :nh]
    return q16, kv, kv16, gate, lf


def _lane_column(x, lane, n):
    if x.shape[1] <= LANES:
        return x[:, n:n + 1]
    return jnp.sum(jnp.where(lane == n, x, 0.0), axis=-1, keepdims=True)


def _fox_decode3_kernel(pt_ref, q_ref, kv_ref, lft_ref, kn_ref, vn_ref, lfn_ref, g_ref, o_ref,
                        m_s, l_s, acc_s, c_s, *, nh, page, scale):
    p = pl.program_id(1)

    @pl.when(p == 0)
    def _():
        m_s[...] = jnp.full(m_s.shape, NEG, F32)
        l_s[...] = jnp.zeros(l_s.shape, F32)
        acc_s[...] = jnp.zeros(acc_s.shape, F32)
        c_s[...] = jnp.zeros(c_s.shape, F32)

    q = q_ref[...]
    x = kv_ref[...].reshape(page, 2 * nh, HEAD_DIM)
    qk = jnp.sum(x[:, :nh, :] * q[None], axis=-1, keepdims=True)
    ct = lft_ref[...]
    lane = lax.broadcasted_iota(jnp.int32, ct.shape, 1)
    sh = 1
    while sh < page:
        ct = ct + jnp.where(lane >= sh, pltpu.roll(ct, sh, 1), 0.0)
        sh *= 2
    ct = ct + c_s[...]
    c3 = jnp.stack([ct[:, t:t + 1] for t in range(page)], axis=0)
    s = qk * scale - c3
    m_old = m_s[...]
    m_new = jnp.maximum(m_old, jnp.max(s, axis=0))
    alpha = jnp.exp(m_old - m_new)
    pm = jnp.exp(s - m_new[None])
    l_s[...] = alpha * l_s[...] + jnp.sum(pm, axis=0)
    acc_s[...] = alpha * acc_s[...] + jnp.sum(pm * x[:, nh:, :], axis=0)
    m_s[...] = m_new
    c_s[...] = ct[:, page - 1:page]

    @pl.when(p == pl.num_programs(1) - 1)
    def _():
        c_new = c_s[...] + lfn_ref[...]
        s_n = jnp.sum(kn_ref[...] * q, axis=-1, keepdims=True) * scale - c_new
        m_f = jnp.maximum(m_s[...], s_n)
        a_f = jnp.exp(m_s[...] - m_f)
        p_n = jnp.exp(s_n - m_f)
        l_f = a_f * l_s[...] + p_n
        acc = a_f * acc_s[...] + p_n * vn_ref[...]
        o_ref[...] = acc / l_f * g_ref[...]


def fox_decode3(q, k_new, v_new, lf_new, gate, kv_cache, lf_pool, page_table, layer):
    bq, d = q.shape
    nh = d // HEAD_DIM
    n_phys, page = kv_cache.shape[1], kv_cache.shape[2]
    n_pages = page_table.shape[1]
    rows = page * 2 * nh
    kv2 = kv_cache.reshape(-1, HEAD_DIM)
    lft = jnp.transpose(lf_pool.astype(F32), (0, 2, 1))

    def per_seq(n):
        return pl.BlockSpec((None, nh, n), lambda b, p, pt: (b, 0, 0))

    grid_spec = pltpu.PrefetchScalarGridSpec(
        num_scalar_prefetch=1, grid=(bq, n_pages),
        in_specs=[per_seq(HEAD_DIM),
                  pl.BlockSpec((rows, HEAD_DIM), lambda b, p, pt: (layer * n_phys + pt[b * n_pages + p], 0)),
                  pl.BlockSpec((None, nh, page), lambda b, p, pt: (pt[b * n_pages + p], 0, 0)),
                  per_seq(HEAD_DIM), per_seq(HEAD_DIM), per_seq(1), per_seq(HEAD_DIM)],
        out_specs=per_seq(HEAD_DIM),
        scratch_shapes=[pltpu.VMEM((nh, 1), F32), pltpu.VMEM((nh, 1), F32), pltpu.VMEM((nh, HEAD_DIM), F32),
                        pltpu.VMEM((nh, 1), F32)])
    out = pl.pallas_call(
        functools.partial(_fox_decode3_kernel, nh=nh, page=page, scale=HEAD_DIM ** -0.5),
        grid_spec=grid_spec, out_shape=jax.ShapeDtypeStruct((bq, nh, HEAD_DIM), F32),
        compiler_params=_cparams(("parallel", "arbitrary"), 8 * rows * HEAD_DIM * 4),
        name="fox_decode",
    )(page_table.reshape(-1).astype(jnp.int32), q.reshape(bq, nh, HEAD_DIM), kv2, lft,
      k_new.reshape(bq, nh, HEAD_DIM), v_new.reshape(bq, nh, HEAD_DIM), lf_new.reshape(bq, nh, 1),
      gate.reshape(bq, nh, HEAD_DIM))
    return out.reshape(bq, d)


def fox_layer(xp, xs, hp, hs, b, t, n_s, kv_cache, layer, lf_pool, page_table, w_in, b_f, qn, kn, w_out):
    n_p, d = hp.shape
    nh = d // HEAD_DIM
    tmp = _row_tile(n_p, 1024)
    q16, kvp, kv16, gate, lf = _fox_proj(hp, w_in, b_f, qn, kn, tmp)
    lfp = lf.reshape(b, t, nh)
    c = seq_cumsum(lfp)
    og = fox_attention(q16, kv16, c, gate, b, t, nh)
    xp = matmul(og, [(w_out, 0)], [(d, F32, 0)], _resid_epi, tm=tmp, tn=512, nj=d // 512, tiles=[(xp, 0)], name="fox_out")[0]
    sr = hs.shape[0]
    q16s, kvs, _, gates, lfs = _fox_proj(hs, w_in, b_f, qn, kn, sr)
    logf = lfs[:n_s].reshape(n_s, 1, nh)
    ogs = fox_decode3(q16s[:n_s].astype(F32), kvs[:n_s, :d], kvs[:n_s, d:], lfs[:n_s], gates[:n_s],
                      kv_cache, lf_pool, page_table, layer).astype(BF16)
    ogs = jnp.pad(ogs, ((0, sr - n_s), (0, 0)))
    xs = matmul(ogs, [(w_out, 0)], [(d, F32, 0)], _resid_epi, tm=sr, tn=512, nj=d // 512, tiles=[(xs, 0)], name="fox_out")[0]
    outs = (kvp.reshape(1, b, t, 2, nh, HEAD_DIM), lfp.reshape(1, b, t, nh),
            kvs[:n_s].reshape(1, n_s, 1, 2, nh, HEAD_DIM), logf.reshape(1, n_s, 1, nh))
    return xp, xs, outs


def _hg_proj(h, w_in, lb, tm):
    d = h.shape[1]
    tn = 512
    nb = d // tn

    def q_epi(accs, vecs, tiles, j):
        a = accs[0]
        return [a * jax.nn.sigmoid(a)]

    def f_epi(accs, vecs, tiles, j):
        f = vecs[0] + (1.0 - vecs[0]) * jax.nn.sigmoid(accs[0])
        return [1.0 - f, jnp.log(f)]

    def id_epi(accs, vecs, tiles, j):
        return [accs[0]]

    def gate_epi(accs, vecs, tiles, j):
        return [jax.nn.sigmoid(accs[0])]

    q = matmul(h, [(w_in, 0)], [(d, F32, 0)], q_epi, tm=tm, tn=tn, nj=nb, name="hg_q")[0]
    k, logf = matmul(h, [(w_in, nb)], [(d, F32, 0), (d, F32, 0)], f_epi, tm=tm, tn=tn, nj=nb,
                     vecs=[(_vec(lb), 0)], name="hg_f")
    v = matmul(h, [(w_in, 2 * nb)], [(d, F32, 0)], id_epi, tm=tm, tn=tn, nj=nb, name="hg_v")[0]
    gate = matmul(h, [(w_in, 3 * nb)], [(d, F32, 0)], gate_epi, tm=tm, tn=tn, nj=nb, name="hg_gate")[0]
    return q, k, v, logf, gate


def _hgrn_kernel(q_ref, k_ref, v_ref, lf_ref, gate_ref, on_ref, s0_ref, o_ref, sf_ref, st_s, *, hb, nchunk, chunk):
    tb = pl.program_id(2)

    @pl.when(tb == 0)
    def _():
        st_s[...] = s0_ref[...]

    lf = lf_ref[...]
    row = lax.broadcasted_iota(jnp.int32, lf.shape, 0) % chunk
    b = lf
    sh = 1
    while sh < chunk:
        b = b + jnp.where(row >= sh, pltpu.roll(b, sh, 0), 0.0)
        sh *= 2
    k = k_ref[...]
    v = v_ref[...]
    qd = q_ref[...] * jnp.exp(b)
    kd = k * jnp.exp(-b)
    tri = lax.broadcasted_iota(jnp.int32, (chunk, chunk), 0) >= lax.broadcasted_iota(jnp.int32, (chunk, chunk), 1)
    on = on_ref[...]
    for h in range(hb):
        st = st_s[h]
        cs = slice(h * HEAD_DIM, (h + 1) * HEAD_DIM)
        for c in range(nchunk):
            rs = slice(c * chunk, (c + 1) * chunk)
            bc = b[rs, cs]
            bl = bc[chunk - 1:chunk, :]
            qdc = qd[rs, cs].astype(BF16)
            kdc = kd[rs, cs].astype(BF16)
            vc = v[rs, cs].astype(BF16)
            k2 = (k[rs, cs] * jnp.exp(bl - bc)).astype(BF16)
            att = lax.dot_general(qdc, kdc, (((1,), (1,)), ((), ())), preferred_element_type=F32)
            att = jnp.where(tri, att, 0.0)
            o = lax.dot_general(qdc, st.astype(BF16), (((1,), (1,)), ((), ())), preferred_element_type=F32)
            o = o + jnp.dot(att.astype(BF16), vc, preferred_element_type=F32)
            st = st * jnp.exp(bl) + lax.dot_general(vc, k2, (((0,), (0,)), ((), ())), preferred_element_type=F32)
            y = o * lax.rsqrt(jnp.mean(o * o, axis=-1, keepdims=True) + EPS) * on
            o_ref[rs, cs] = (y * gate_ref[rs, cs]).astype(o_ref.dtype)
        st_s[h] = st

    @pl.when(tb == pl.num_programs(2) - 1)
    def _():
        sf_ref[...] = st_s[...]


def hgrn_scan(q, k, v, logf, gate, on, s0t, b, t):
    d = q.shape[1]
    nh = d // HEAD_DIM
    hb = 2
    tb = min(256, t)
    ntb = t // tb
    kern = functools.partial(_hgrn_kernel, hb=hb, nchunk=tb // HG_CHUNK, chunk=HG_CHUNK)
    tile = pl.BlockSpec((tb, hb * HEAD_DIM), lambda bi, h, i: (bi * ntb + i, h))
    st_spec = pl.BlockSpec((None, hb, HEAD_DIM, HEAD_DIM), lambda bi, h, i: (bi, h, 0, 0))
    return pl.pallas_call(
        kern,
        grid=(b, nh // hb, ntb),
        in_specs=[tile, tile, tile, tile, tile, pl.BlockSpec((1, HEAD_DIM), lambda bi, h, i: (0, 0)), st_spec],
        out_specs=[tile, st_spec],
        out_shape=[jax.ShapeDtypeStruct(q.shape, BF16), jax.ShapeDtypeStruct(s0t.shape, F32)],
        scratch_shapes=[pltpu.VMEM((hb, HEAD_DIM, HEAD_DIM), F32)],
        compiler_params=_cparams(("parallel", "parallel", "arbitrary"), 16 << 20),
        name="hgrn_scan",
    )(q, k, v, logf, gate, on.reshape(1, HEAD_DIM).astype(F32), s0t)


def hgrn_layer(xp, xs, hp, hs, b, t, n_s, state, w_in, lb, on, w_out):
    n_p, d = hp.shape
    nh = d // HEAD_DIM
    tmp = _row_tile(n_p, 1024)
    q, k, v, logf, gate = _hg_proj(hp, w_in, lb, tmp)
    s0 = jnp.zeros((b, nh, HEAD_DIM, HEAD_DIM), F32)
    og, sft = hgrn_scan(q, k, v, logf, gate, on, s0, b, t)
    xp = matmul(og, [(w_out, 0)], [(d, F32, 0)], _resid_epi, tm=tmp, tn=512, nj=d // 512, tiles=[(xp, 0)], name="hg_out")[0]
    sr = hs.shape[0]
    proj = _hg_proj(hs, w_in, lb, sr)

    def spread(a_):
        return jnp.zeros((n_s, HG_CHUNK, d), a_.dtype).at[:, 0].set(a_[:n_s]).reshape(n_s * HG_CHUNK, d)

    qs, ks, vs, lfs, gs = [spread(a_) for a_ in proj]
    ogs, sst = hgrn_scan(qs, ks, vs, lfs, gs, on, jnp.swapaxes(state.astype(F32), -1, -2), n_s, HG_CHUNK)
    ogs = jnp.pad(ogs.reshape(n_s, HG_CHUNK, d)[:, 0], ((0, sr - n_s), (0, 0)))
    xs = matmul(ogs, [(w_out, 0)], [(d, F32, 0)], _resid_epi, tm=sr, tn=512, nj=d // 512, tiles=[(xs, 0)], name="hg_out")[0]
    return xp, xs, (jnp.swapaxes(sft, -1, -2)[None], jnp.swapaxes(sst, -1, -2)[None])


def _s5_params(a_re, a_im, log_dt, b_re, b_im, c_re, c_im):
    A = lax.complex(a_re.astype(F32), a_im.astype(F32))
    dt = jnp.exp(log_dt.astype(F32))[:, None]
    Ab = jnp.exp(A * dt)
    Bb = ((Ab - 1.0) / A)[..., None] * lax.complex(b_re.astype(F32), b_im.astype(F32))
    Cc = lax.complex(c_re.astype(F32), c_im.astype(F32))
    return Ab, Bb, Cc


def _s5_mats(Ab, Bb, Cc, L):
    hp = lax.Precision.HIGHEST
    g, p = Ab.shape
    c = Bb.shape[-1]
    pw = jnp.cumprod(jnp.concatenate([jnp.ones((1, g, p), Ab.dtype), jnp.broadcast_to(Ab, (L, g, p))], axis=0), axis=0)
    kt = jnp.einsum('gcp,tgp,gpd->tgcd', Cc, pw[:L], Bb, precision=hp).real
    lag = jnp.arange(L)[None, :] - jnp.arange(L)[:, None]
    tm = jnp.where((lag >= 0)[:, :, None, None, None], kt[jnp.maximum(lag, 0)], 0.0)
    tm = jnp.transpose(tm, (2, 0, 4, 1, 3)).reshape(g, L * c, L * c)
    rc = pw[L - 1 - jnp.arange(L)][:, :, :, None] * Bb[None]
    rc = jnp.transpose(rc, (1, 0, 3, 2)).reshape(g, L * c, p)
    oc = jnp.transpose(Cc, (0, 2, 1))[:, :, None, :] * jnp.transpose(pw[1:L + 1], (1, 2, 0))[:, :, :, None]
    oc = oc.reshape(g, p, L * c)
    ab = pw[L]
    return tm, rc.real, rc.imag, oc.real, -oc.imag, ab.real, ab.imag


def _s5_kernel(u_ref, t_ref, rr_ref, ri_ref, or_ref, oi_ref, ar_ref, ai_ref, x0r_ref, x0i_ref,
               y_ref, fr_ref, fi_ref, vr_s, vi_s, xr_s, xi_s, *, gb, nch, bp, cdt):
    for g in range(gb):
        u = u_ref[g].astype(cdt)
        vr_s[g] = jnp.dot(u, rr_ref[g], preferred_element_type=F32)
        vi_s[g] = jnp.dot(u, ri_ref[g], preferred_element_type=F32)
    ar = ar_ref[...]
    ai = ai_ref[...]

    def step(n, carry):
        xr, xi = carry
        r0 = pl.multiple_of(n * bp, bp)
        xr_s[:, pl.ds(r0, bp), :] = xr
        xi_s[:, pl.ds(r0, bp), :] = xi
        vr = vr_s[:, pl.ds(r0, bp), :]
        vi = vi_s[:, pl.ds(r0, bp), :]
        return ar * xr - ai * xi + vr, ar * xi + ai * xr + vi

    xr, xi = lax.fori_loop(0, nch, step, (x0r_ref[...], x0i_ref[...]))
    fr_ref[...] = xr
    fi_ref[...] = xi
    for g in range(gb):
        u = u_ref[g].astype(cdt)
        y = jnp.dot(u, t_ref[g], preferred_element_type=F32)
        y = y + jnp.dot(xr_s[g].astype(cdt), or_ref[g], preferred_element_type=F32)
        y = y + jnp.dot(xi_s[g].astype(cdt), oi_ref[g], preferred_element_type=F32)
        y_ref[g] = y


def s5_scan(u, x0, Ab, Bb, Cc, L, cdt):
    bq, t, ng, c = u.shape
    p = Ab.shape[-1]
    bp = -(-bq // 8) * 8
    nch = t // L
    w = L * c
    rows = nch * bp
    mats = _s5_mats(Ab, Bb, Cc, L)
    tm, rr, ri, orr, oi = [m.astype(cdt) for m in mats[:5]]
    ar, ai = [m.reshape(ng, 1, p) for m in mats[5:]]
    up = jnp.pad(u, ((0, bp - bq), (0, 0), (0, 0), (0, 0)))
    ug = jnp.transpose(up.reshape(bp, nch, L, ng, c), (3, 1, 0, 2, 4)).reshape(ng, rows, w)
    x0p = jnp.pad(x0.astype(F32), ((0, bp - bq), (0, 0), (0, 0), (0, 0)))
    x0r = jnp.transpose(x0p[..., 0], (1, 0, 2))
    x0i = jnp.transpose(x0p[..., 1], (1, 0, 2))
    gb = 4
    kern = functools.partial(_s5_kernel, gb=gb, nch=nch, bp=bp, cdt=cdt)

    def spec(*shape):
        return pl.BlockSpec((gb,) + shape, lambda i: (i,) + (0,) * len(shape))

    y, fr, fi = pl.pallas_call(
        kern,
        grid=(ng // gb,),
        in_specs=[spec(rows, w), spec(w, w), spec(w, p), spec(w, p), spec(p, w), spec(p, w),
                  spec(1, p), spec(1, p), spec(bp, p), spec(bp, p)],
        out_specs=[spec(rows, w), spec(bp, p), spec(bp, p)],
        out_shape=[jax.ShapeDtypeStruct((ng, rows, w), F32), jax.ShapeDtypeStruct((ng, bp, p), F32),
                   jax.ShapeDtypeStruct((ng, bp, p), F32)],
        scratch_shapes=[pltpu.VMEM((gb, rows, p), F32)] * 4,
        compiler_params=_cparams(("parallel",), gb * rows * (4 * w * 4 + 4 * LANES * 4)),
        name="s5_scan",
    )(ug, tm, rr, ri, orr, oi, ar, ai, x0r, x0i)
    y = jnp.transpose(y.reshape(ng, nch, bp, L, c), (2, 1, 3, 0, 4)).reshape(bp, t, ng, c)[:bq]
    xf = jnp.stack([jnp.transpose(fr, (1, 0, 2)), jnp.transpose(fi, (1, 0, 2))], axis=-1)[:bq]
    return y, xf


def _s5_act_kernel(y_ref, u_ref, d_ref, o_ref):
    o_ref[...] = jax.nn.gelu(y_ref[...] + d_ref[...] * u_ref[...]).astype(o_ref.dtype)


def s5_act(y, u, dvec):
    n, d = y.shape
    tm = min(n, 512)
    tile = pl.BlockSpec((tm, d), lambda i: (i, 0))
    return pl.pallas_call(
        _s5_act_kernel, grid=(n // tm,),
        in_specs=[tile, tile, pl.BlockSpec((1, d), lambda i: (0, 0))], out_specs=tile,
        out_shape=jax.ShapeDtypeStruct((n, d), BF16),
        compiler_params=_cparams(("parallel",), 2 * tm * d * 10), name="s5_act",
    )(y, u, dvec.reshape(1, d).astype(F32))


def _glu_epi(accs, vecs, tiles, j):
    return [accs[0] * jax.nn.sigmoid(accs[1]) + tiles[0]]


def s5_layer(xp, xs, up, us, b, t, n_s, st, a_re, a_im, log_dt, b_re, b_im, c_re, c_im, dvec, w_glu):
    n_p, d = up.shape
    ng = d // S5_CH
    Ab, Bb, Cc = _s5_params(a_re, a_im, log_dt, b_re, b_im, c_re, c_im)
    x0 = jnp.zeros((b, ng, Ab.shape[-1], 2), F32)
    y, sp = s5_scan(up.reshape(b, t, ng, S5_CH), x0, Ab, Bb, Cc, 16, BF16)
    z = s5_act(y.reshape(n_p, d), up, dvec)
    tmp = _row_tile(n_p, 1024)
    nb = d // 256
    xp = matmul(z, [(w_glu, 0), (w_glu, nb)], [(d, F32, 0)], _glu_epi, tm=tmp, tn=256, nj=nb, tiles=[(xp, 0)], name="s5_glu")[0]
    sr = us.shape[0]
    y, ss = s5_scan(us[:n_s].reshape(n_s, 1, ng, S5_CH), st, Ab, Bb, Cc, 1, F32)
    z = s5_act(jnp.pad(y.reshape(n_s, d), ((0, sr - n_s), (0, 0))), us, dvec)
    xs = matmul(z, [(w_glu, 0), (w_glu, nb)], [(d, F32, 0)], _glu_epi, tm=sr, tn=256, nj=nb, tiles=[(xs, 0)], name="s5_glu")[0]
    return xp, xs, (sp[None], ss[None])


def _nsa_proj(h, w_in, qn, kn, tm, nkv):
    d = h.shape[1]
    kvw = nkv * HEAD_DIM
    tn = 512
    nb = d // tn
    assert kvw == tn

    def q_epi(accs, vecs, tiles, j):
        return [_head_rms(accs[0], vecs[0])]

    def rows_epi(accs, vecs, tiles, j):
        y = jnp.where(j == 2, _head_rms(accs[0], vecs[0]), accs[0])
        return [y, y]

    def win_epi(accs, vecs, tiles, j):
        y = jnp.where(j == 0, _head_rms(accs[0], vecs[0]), accs[0])
        return [y, y]

    def gate_epi(accs, vecs, tiles, j):
        return [_valid_lanes(jax.nn.sigmoid(accs[0]), 3 * (d // HEAD_DIM))]

    q16 = matmul(h, [(w_in, 0)], [(d, BF16, 0)], q_epi, tm=tm, tn=tn, nj=nb, vecs=[(jnp.tile(_vec(qn), (1, d // HEAD_DIM)), 0)], name="nsa_q")[0]
    rows, rows16 = matmul(h, [(w_in, nb)], [(4 * kvw, F32, 0), (4 * kvw, BF16, 0)], rows_epi, tm=tm, tn=tn, nj=4,
                          vecs=[(jnp.tile(_vec(kn[1]), (1, 4 * nkv)), 0)], name="nsa_rows")
    wrows, wrows16 = matmul(h, [(w_in, nb + 4)], [(2 * kvw, F32, 0), (2 * kvw, BF16, 0)], win_epi, tm=tm, tn=tn, nj=2,
                            vecs=[(jnp.tile(_vec(kn[2]), (1, 2 * nkv)), 0)], name="nsa_wrows")
    nhq = d // HEAD_DIM
    gates = matmul(h, [(w_in, (d + 6 * kvw) // LANES)], [(LANES, F32, 0)], gate_epi, tm=tm, tn=LANES, nj=1, name="nsa_gates")[0][:, :3 * nhq]
    return q16, gates, rows, wrows, rows16, wrows16


def _bucket_table():
    n = np.arange(MAX_DIST + 1)
    exact = N_BUCKETS // 2
    lg = np.log(np.maximum(n, exact).astype(np.float32) / np.float32(exact)) / np.float32(math.log(MAX_DIST / exact))
    large = np.minimum(exact + (lg * (N_BUCKETS - exact)).astype(np.int32), N_BUCKETS - 1)
    return np.where(n < exact, n, large).astype(np.int32)


def _dist_bias(table, dist):
    bt = table.astype(F32)[_bucket_table()]
    return jnp.moveaxis(bt[np.clip(dist, 0, MAX_DIST)], -1, 0)


def _nsa_compress_kernel(rows_ref, pk_ref, pv_ref, posk_ref, posv_ref, w1_ref, w2_ref, kn_ref, kc_ref, vc_ref, *, nkv):
    hp = lax.Precision.HIGHEST
    kvw = nkv * HEAD_DIM
    xk = rows_ref[:, :kvw] + posk_ref[...]
    xv = rows_ref[:, kvw:2 * kvw] + posv_ref[...]
    pooled = (jnp.dot(pk_ref[...], xk, preferred_element_type=F32, precision=hp),
              jnp.dot(pv_ref[...], xv, preferred_element_type=F32, precision=hp))
    _cmp_mlp(pooled, w1_ref, w2_ref, kn_ref, kc_ref, vc_ref, nkv)


def _cmp_mlp(pooled, w1_ref, w2_ref, kn_ref, kc_ref, vc_ref, nkv):
    for which, out in ((0, kc_ref), (1, vc_ref)):
        w1 = w1_ref[which].astype(BF16)
        w2 = w2_ref[which].astype(BF16)
        for h in range(nkv):
            x = pooled[which][:, h * HEAD_DIM:(h + 1) * HEAD_DIM].astype(BF16)
            a = jnp.dot(x, w1, preferred_element_type=F32)
            y = jnp.dot((a * jax.nn.sigmoid(a)).astype(BF16), w2, preferred_element_type=F32)
            if which == 0:
                y = y * lax.rsqrt(jnp.mean(y * y, axis=-1, keepdims=True) + EPS) * kn_ref[...]
            out[:, h * HEAD_DIM:(h + 1) * HEAD_DIM] = y


def nsa_compress(rows, b, t, nkv, cmp_pos, cmp_w, cmp_w1, cmp_w2, kn_cmp):
    nb = t // CMP_BLK
    kvw = nkv * HEAD_DIM
    eye = jnp.eye(nb, dtype=F32)
    pk = jnp.kron(eye, cmp_w[0].astype(F32)[None, :])
    pv = jnp.kron(eye, cmp_w[1].astype(F32)[None, :])
    posk = jnp.tile(cmp_pos[0].astype(F32), (nb, nkv))
    posv = jnp.tile(cmp_pos[1].astype(F32), (nb, nkv))

    def full(shape):
        return pl.BlockSpec(shape, lambda bi: (0,) * len(shape))

    return pl.pallas_call(
        functools.partial(_nsa_compress_kernel, nkv=nkv),
        grid=(b,),
        in_specs=[pl.BlockSpec((t, 2 * kvw), lambda bi: (bi, 0)), full((nb, t)), full((nb, t)), full((t, kvw)),
                  full((t, kvw)), full((2, HEAD_DIM, HEAD_DIM)), full((2, HEAD_DIM, HEAD_DIM)), full((1, HEAD_DIM))],
        out_specs=[pl.BlockSpec((None, nb, kvw), lambda bi: (bi, 0, 0))] * 2,
        out_shape=[jax.ShapeDtypeStruct((b, nb, kvw), F32)] * 2,
        compiler_params=_cparams(("parallel",), 2 * t * 2 * kvw * 4 + 4 * t * kvw * 4),
        name="nsa_compress",
    )(rows, pk, pv, posk, posv, cmp_w1.astype(F32), cmp_w2.astype(F32), kn_cmp.reshape(1, HEAD_DIM).astype(F32))


def _nsa_cmp_attn_kernel(q_ref, kc_ref, vc_ref, bias_ref, g_ref, o_ref, sel_ref, *, tq, nb, grp, n_sel, scale):
    i = pl.program_id(2)
    kc = kc_ref[...].astype(BF16)
    vc = vc_ref[...].astype(BF16)
    qpos = i * tq + lax.broadcasted_iota(jnp.int32, (tq, nb), 0)
    blk = lax.broadcasted_iota(jnp.int32, (tq, nb), 1)
    vis = qpos >= blk * CMP_BLK + (CMP_BLK - 1)
    visf = jnp.where(vis, 1.0, 0.0)
    gates = g_ref[...]
    imp = jnp.zeros((tq, nb), F32)
    for g in range(grp):
        q = q_ref[:, g * HEAD_DIM:(g + 1) * HEAD_DIM]
        s = lax.dot_general(q, kc, (((1,), (1,)), ((), ())), preferred_element_type=F32) * scale + bias_ref[g]
        s = jnp.where(vis, s, NEG)
        p = jnp.exp(s - jnp.max(s, axis=-1, keepdims=True))
        p = p / jnp.sum(p, axis=-1, keepdims=True) * visf
        imp = imp + p
        o = jnp.dot(p.astype(BF16), vc, preferred_element_type=F32)
        o_ref[:, g * HEAD_DIM:(g + 1) * HEAD_DIM] = gates[:, g:g + 1] * o
    cur = qpos // SEL_BLK
    forced = jnp.logical_or(blk == 0, jnp.logical_or(blk == cur, blk == cur - 1))
    score = jnp.where(blk <= cur, jnp.where(forced, FORCE, imp), -1.0)
    rank = jnp.zeros((tq, nb), F32)
    for n in range(nb):
        col = score[:, n:n + 1]
        beats = jnp.logical_or(col > score, jnp.logical_and(col == score, blk > n))
        rank = rank + jnp.where(beats, 1.0, 0.0)
    sel_ref[...] = jnp.where(jnp.logical_and(rank < n_sel, score >= 0.0), 1.0, 0.0)


def _nsa_slc_kernel(q_ref, k_ref, v_ref, sel_ref, tz_ref, g_ref, oin_ref, o_ref, *, tq, nb, grp, scale):
    i = pl.program_id(2)
    tk = tq
    sel = sel_ref[...].astype(BF16)
    qpos = i * tq + lax.broadcasted_iota(jnp.int32, (tq, tk), 0)
    koff = lax.broadcasted_iota(jnp.int32, (tq, tk), 1)
    n_iota = lax.broadcasted_iota(jnp.int32, (nb, tk), 0)
    c_blk = lax.broadcasted_iota(jnp.int32, (nb, tk), 1) // SEL_BLK
    gates = g_ref[...]
    for g in range(grp):
        q = q_ref[:, g * HEAD_DIM:(g + 1) * HEAD_DIM]

        def body(j, carry, q=q, g=g):
            m, l, acc = carry
            start = pl.multiple_of(j * tk, tk)
            k = k_ref[pl.ds(start, tk), :]
            v = v_ref[pl.ds(start, tk), :]
            e = jnp.where(n_iota == (tk // SEL_BLK) * j + c_blk, 1.0, 0.0).astype(BF16)
            chosen = jnp.dot(sel, e, preferred_element_type=F32) > 0.5
            mask = jnp.logical_and(chosen, qpos >= koff + j * tk)
            s = lax.dot_general(q, k, (((1,), (1,)), ((), ())), preferred_element_type=F32) * scale
            s = s + tz_ref[g, jnp.minimum(i - j, 2)]
            s = jnp.where(mask, s, NEG)
            m_new = jnp.maximum(m, jnp.max(s, axis=-1, keepdims=True))
            alpha = jnp.exp(m - m_new)
            p = jnp.exp(s - m_new)
            l = alpha * l + jnp.sum(p, axis=-1, keepdims=True)
            acc = alpha * acc + jnp.dot(p.astype(BF16), v, preferred_element_type=F32)
            return m_new, l, acc

        init = (jnp.full((tq, 1), NEG, F32), jnp.zeros((tq, 1), F32), jnp.zeros((tq, HEAD_DIM), F32))
        m, l, acc = lax.fori_loop(0, i + 1, body, init)
        cs = slice(g * HEAD_DIM, (g + 1) * HEAD_DIM)
        o_ref[:, cs] = oin_ref[:, cs] + gates[:, grp + g:grp + g + 1] * (acc / l)


def _nsa_win_kernel(q_ref, k_ref, v_ref, wz_ref, g_ref, oin_ref, o_ref, *, tq, grp, scale, window):
    i = pl.program_id(2)
    span = window + tq
    start = pl.multiple_of(i * tq, tq)
    kw = k_ref[pl.ds(start, span), :]
    vw = v_ref[pl.ds(start, span), :]
    r = lax.broadcasted_iota(jnp.int32, (tq, span), 0)
    c = lax.broadcasted_iota(jnp.int32, (tq, span), 1)
    dist = r + window - c
    mask = jnp.logical_and(jnp.logical_and(dist >= 0, dist <= window), i * tq - window + c >= 0)
    gates = g_ref[...]
    for g in range(grp):
        cs = slice(g * HEAD_DIM, (g + 1) * HEAD_DIM)
        s = lax.dot_general(q_ref[:, cs], kw, (((1,), (1,)), ((), ())), preferred_element_type=F32) * scale + wz_ref[g]
        s = jnp.where(mask, s, NEG)
        p = jnp.exp(s - jnp.max(s, axis=-1, keepdims=True))
        o = jnp.dot(p.astype(BF16), vw, preferred_element_type=F32) / jnp.sum(p, axis=-1, keepdims=True)
        o_ref[:, cs] = (oin_ref[:, cs] + gates[:, 2 * grp + g:2 * grp + g + 1] * o).astype(o_ref.dtype)


def nsa_attention(q16, rows16, wrows16, gates, kc, vc, table, b, t, nkv):
    n, d = q16.shape
    nh = d // HEAD_DIM
    grp = nh // nkv
    gw = grp * HEAD_DIM
    nb = t // CMP_BLK
    scale = HEAD_DIM ** -0.5
    sem = ("parallel", "parallel", "arbitrary")
    gk = jnp.transpose(gates.reshape(n, 3, nkv, grp), (2, 0, 1, 3)).reshape(nkv, n, 3 * grp)
    tq = min(256, t)
    nq = t // tq
    dist_c = np.arange(t)[:, None] - (np.arange(nb) * CMP_BLK + CMP_BLK - 1)[None, :]
    bias_c = _dist_bias(table, dist_c)
    qspec = pl.BlockSpec((tq, gw), lambda bi, h, i: (bi * nq + i, h))
    gspec = pl.BlockSpec((None, tq, 3 * grp), lambda bi, h, i: (h, bi * nq + i, 0))
    o1, sel = pl.pallas_call(
        functools.partial(_nsa_cmp_attn_kernel, tq=tq, nb=nb, grp=grp, n_sel=min(N_SEL, nb), scale=scale),
        grid=(b, nkv, nq),
        in_specs=[qspec,
                  pl.BlockSpec((None, nb, HEAD_DIM), lambda bi, h, i: (bi, 0, h)),
                  pl.BlockSpec((None, nb, HEAD_DIM), lambda bi, h, i: (bi, 0, h)),
                  pl.BlockSpec((grp, tq, nb), lambda bi, h, i: (h, i, 0)),
                  gspec],
        out_specs=[qspec, pl.BlockSpec((None, None, tq, nb), lambda bi, h, i: (bi, h, i, 0))],
        out_shape=[jax.ShapeDtypeStruct((n, d), F32), jax.ShapeDtypeStruct((b, nkv, t, nb), F32)],
        compiler_params=_cparams(sem, 16 << 20), name="nsa_cmp_attn",
    )(q16, kc, vc, bias_c, gk)
    assert tq >= MAX_DIST
    r = np.arange(tq)
    tz = _dist_bias(table, np.stack([dd * tq + r[:, None] - r[None, :] for dd in range(3)]))
    o2 = pl.pallas_call(
        functools.partial(_nsa_slc_kernel, tq=tq, nb=nb, grp=grp, scale=scale),
        grid=(b, nkv, nq),
        in_specs=[qspec,
                  pl.BlockSpec((t, HEAD_DIM), lambda bi, h, i: (bi, 2 * nkv + h)),
                  pl.BlockSpec((t, HEAD_DIM), lambda bi, h, i: (bi, 3 * nkv + h)),
                  pl.BlockSpec((None, None, tq, nb), lambda bi, h, i: (bi, h, i, 0)),
                  pl.BlockSpec((grp, 3, tq, tq), lambda bi, h, i: (h, 0, 0, 0)),
                  gspec, qspec],
        out_specs=qspec,
        out_shape=jax.ShapeDtypeStruct((n, d), F32),
        compiler_params=_cparams(sem, 32 << 20), name="nsa_slc_attn",
    )(q16, rows16, rows16, sel, tz, gk, o1)
    tq = min(128, t)
    nq = t // tq
    r = np.arange(tq)
    qspec = pl.BlockSpec((tq, gw), lambda bi, h, i: (bi * nq + i, h))
    gspec = pl.BlockSpec((None, tq, 3 * grp), lambda bi, h, i: (h, bi * nq + i, 0))
    span = WINDOW + tq
    wz = _dist_bias(table, r[:, None] + WINDOW - np.arange(span)[None, :])
    kvw = nkv * HEAD_DIM
    kwp = jnp.pad(wrows16.reshape(b, t, 2 * kvw), ((0, 0), (WINDOW, 0), (0, 0))).reshape(b * (t + WINDOW), 2 * kvw)
    og = pl.pallas_call(
        functools.partial(_nsa_win_kernel, tq=tq, grp=grp, scale=scale, window=WINDOW),
        grid=(b, nkv, nq),
        in_specs=[qspec,
                  pl.BlockSpec((t + WINDOW, HEAD_DIM), lambda bi, h, i: (bi, h)),
                  pl.BlockSpec((t + WINDOW, HEAD_DIM), lambda bi, h, i: (bi, nkv + h)),
                  pl.BlockSpec((grp, tq, span), lambda bi, h, i: (h, 0, 0)),
                  gspec, qspec],
        out_specs=qspec,
        out_shape=jax.ShapeDtypeStruct((n, d), BF16),
        compiler_params=_cparams(sem, 24 << 20), name="nsa_win_attn",
    )(q16, kwp, kwp, wz, gk, o2)
    return og


DEC_ROWS = 16
DEC_BLOCKS = 256


def _nsa_dec_compress_kernel(pt_ref, page_ref, new_ref, wk_ref, wv_ref, posk_ref, posv_ref, w1_ref, w2_ref, kn_ref,
                             kc_ref, vc_ref, ak_s, av_s, *, nkv, n_pages, page):
    p = pl.program_id(1)
    hp = lax.Precision.HIGHEST
    kvw = nkv * HEAD_DIM

    @pl.when(p == 0)
    def _():
        ak_s[...] = jnp.zeros(ak_s.shape, F32)
        av_s[...] = jnp.zeros(av_s.shape, F32)

    last = p == n_pages
    n_iota = lax.broadcasted_iota(jnp.int32, (DEC_BLOCKS, page), 0)
    tok_blk = lax.broadcasted_iota(jnp.int32, (DEC_BLOCKS, page), 1) // CMP_BLK
    hit = n_iota == (page // CMP_BLK) * p + tok_blk
    pk = jnp.where(hit, wk_ref[...], 0.0)
    pv = jnp.where(hit, wv_ref[...], 0.0)
    per_tok = 4 * nkv
    for h in range(nkv):
        kr = pl.ds(h, page, stride=per_tok)
        vr = pl.ds(nkv + h, page, stride=per_tok)
        xk = jnp.where(last, new_ref[kr, :], page_ref[kr, :]) + posk_ref[...]
        xv = jnp.where(last, new_ref[vr, :], page_ref[vr, :]) + posv_ref[...]
        cs = slice(h * HEAD_DIM, (h + 1) * HEAD_DIM)
        ak_s[:, cs] += jnp.dot(pk, xk, preferred_element_type=F32, precision=hp)
        av_s[:, cs] += jnp.dot(pv, xv, preferred_element_type=F32, precision=hp)

    @pl.when(p == n_pages)
    def _():
        _cmp_mlp((ak_s[...], av_s[...]), w1_ref, w2_ref, kn_ref, kc_ref, vc_ref, nkv)


def _nsa_dec_cmp_kernel(q_ref, kc_ref, vc_ref, bias_ref, g_ref, o_ref, sel_ref, *, grp, qpos, n_sel, scale):
    q = q_ref[...]
    rows = q.shape[0]
    s = lax.dot_general(q, kc_ref[...].astype(BF16), (((1,), (1,)), ((), ())), preferred_element_type=F32) * scale
    s = s + bias_ref[...]
    blk = lax.broadcasted_iota(jnp.int32, s.shape, 1)
    vis = qpos >= blk * CMP_BLK + (CMP_BLK - 1)
    s = jnp.where(vis, s, NEG)
    p = jnp.exp(s - jnp.max(s, axis=-1, keepdims=True))
    p = p / jnp.sum(p, axis=-1, keepdims=True) * jnp.where(vis, 1.0, 0.0)
    o = jnp.dot(p.astype(BF16), vc_ref[...].astype(BF16), preferred_element_type=F32)
    o_ref[...] = g_ref[...][:, 0:1] * o
    head_row = lax.broadcasted_iota(jnp.int32, s.shape, 0) < grp
    imp = jnp.broadcast_to(jnp.sum(jnp.where(head_row, p, 0.0), axis=0, keepdims=True), s.shape)
    cur = qpos // SEL_BLK
    forced = jnp.logical_or(blk == 0, jnp.logical_or(blk == cur, blk == cur - 1))
    score = jnp.where(blk <= cur, jnp.where(forced, FORCE, imp), -1.0)
    rank = jnp.zeros(s.shape, F32)
    for n in range(cur + 1):
        col = _lane_column(score, blk, n)
        beats = jnp.logical_or(col > score, jnp.logical_and(col == score, blk > n))
        rank = rank + jnp.where(beats, 1.0, 0.0)
    sel_ref[...] = jnp.where(jnp.logical_and(rank < n_sel, score >= 0.0), 1.0, 0.0)


def _nsa_dec_slc_kernel(pt_ref, q_ref, page_ref, sel_ref, blast_ref, bfar_ref, b0_ref, kn_ref, vn_ref, g_ref,
                        oin_ref, o_ref, m_s, l_s, acc_s, *, nkv, n_pages, page, scale):
    p = pl.program_id(1)

    @pl.when(p == 0)
    def _():
        m_s[...] = jnp.full(m_s.shape, NEG, F32)
        l_s[...] = jnp.zeros(l_s.shape, F32)
        acc_s[...] = jnp.zeros(acc_s.shape, F32)

    n_iota = lax.broadcasted_iota(jnp.int32, (DEC_BLOCKS, page), 0)
    tok_blk = lax.broadcasted_iota(jnp.int32, (DEC_BLOCKS, page), 1) // SEL_BLK
    e = jnp.where(n_iota == (page // SEL_BLK) * p + tok_blk, 1.0, 0.0).astype(BF16)
    per_tok = 4 * nkv
    for h in range(nkv):
        q = q_ref[h]
        k = page_ref[pl.ds(2 * nkv + h, page, stride=per_tok), :].astype(BF16)
        v = page_ref[pl.ds(3 * nkv + h, page, stride=per_tok), :].astype(BF16)
        chosen = jnp.dot(sel_ref[h].astype(BF16), e, preferred_element_type=F32) > 0.5
        s = lax.dot_general(q, k, (((1,), (1,)), ((), ())), preferred_element_type=F32) * scale
        s = s + jnp.where(p == n_pages - 1, blast_ref[h], bfar_ref[h])
        s = jnp.where(chosen, s, NEG)
        m_old = m_s[h]
        m_new = jnp.maximum(m_old, jnp.max(s, axis=-1, keepdims=True))
        alpha = jnp.exp(m_old - m_new)
        pm = jnp.exp(s - m_new)
        l_s[h] = alpha * l_s[h] + jnp.sum(pm, axis=-1, keepdims=True)
        acc_s[h] = alpha * acc_s[h] + jnp.dot(pm.astype(BF16), v, preferred_element_type=F32)
        m_s[h] = m_new

    @pl.when(p == n_pages - 1)
    def _():
        nblk = n_pages * (page // SEL_BLK)
        for h in range(nkv):
            sel = sel_ref[h]
            s_n = jnp.sum(q_ref[h].astype(F32) * kn_ref[h], axis=-1, keepdims=True) * scale + b0_ref[h]
            sel_new = _lane_column(sel, lax.broadcasted_iota(jnp.int32, sel.shape, 1), nblk)
            s_n = jnp.where(sel_new > 0.5, s_n, NEG)
            m_f = jnp.maximum(m_s[h], s_n)
            a_f = jnp.exp(m_s[h] - m_f)
            p_n = jnp.exp(s_n - m_f)
            l_f = a_f * l_s[h] + p_n
            acc = a_f * acc_s[h] + p_n.astype(BF16).astype(F32) * vn_ref[h]
            o_ref[h] = oin_ref[h] + g_ref[h][:, 1:2] * (acc / l_f)


def _nsa_dec_win_kernel(q_ref, k_ref, v_ref, bias_ref, b0_ref, kn_ref, vn_ref, g_ref, oin_ref, o_ref, *, scale):
    q = q_ref[...]
    s = lax.dot_general(q, k_ref[...].astype(BF16), (((1,), (1,)), ((), ())), preferred_element_type=F32) * scale
    s = s + bias_ref[...]
    s_n = jnp.sum(q.astype(F32) * kn_ref[...], axis=-1, keepdims=True) * scale + b0_ref[...]
    m = jnp.maximum(jnp.max(s, axis=-1, keepdims=True), s_n)
    pm = jnp.exp(s - m)
    p_n = jnp.exp(s_n - m)
    l = jnp.sum(pm, axis=-1, keepdims=True) + p_n
    o = jnp.dot(pm.astype(BF16), v_ref[...].astype(BF16), preferred_element_type=F32)
    o = (o + p_n.astype(BF16).astype(F32) * vn_ref[...]) / l
    o_ref[...] = oin_ref[...] + g_ref[...][:, 2:3] * o


def nsa_decode(q16, gates, rows_new, wrows_new, cache, layer, win_buf, page_table, cmp_pos, cmp_w, cmp_w1, cmp_w2, kn_cmp, table):
    bq, d = q16.shape
    nkv = cache.shape[-2]
    nh = d // HEAD_DIM
    grp = nh // nkv
    kvw = nkv * HEAD_DIM
    n_phys, page = cache.shape[1], cache.shape[2]
    n_pages = page_table.shape[1]
    p_len = n_pages * page
    lb = win_buf.shape[1]
    assert grp <= DEC_ROWS and p_len // CMP_BLK + 1 <= DEC_BLOCKS and page % CMP_BLK == 0 and lb <= min(WINDOW, p_len)
    assert page >= MAX_DIST
    scale = HEAD_DIM ** -0.5
    pt = page_table.reshape(-1).astype(jnp.int32)
    per_tok = 4 * nkv
    prow = page * per_tok
    pool2 = cache.reshape(-1, HEAD_DIM)
    page_spec2 = lambda clamp: pl.BlockSpec(
        (prow, HEAD_DIM), lambda b, p, t_: (layer * n_phys + t_[b * n_pages + (jnp.minimum(p, n_pages - 1) if clamp else p)], 0))
    rpad = ((0, 0), (0, 0), (0, DEC_ROWS - grp), (0, 0))
    qg = jnp.pad(q16.reshape(bq, nkv, grp, HEAD_DIM), rpad)
    gg = jnp.pad(jnp.transpose(gates.reshape(bq, 3, nkv, grp), (0, 2, 3, 1)), rpad)

    def head_rows(bias):
        return jnp.pad(bias.reshape(nkv, grp, -1), ((0, 0), (0, DEC_ROWS - grp), (0, 0)))

    reps = page // CMP_BLK
    newpage = jnp.zeros((bq, prow, HEAD_DIM), F32).at[:, :per_tok].set(rows_new.reshape(bq, per_tok, HEAD_DIM))
    cspec = lambda shape: pl.BlockSpec(shape, lambda b, p, t_: (0,) * len(shape))
    kc, vc = pl.pallas_call(
        functools.partial(_nsa_dec_compress_kernel, nkv=nkv, n_pages=n_pages, page=page),
        grid_spec=pltpu.PrefetchScalarGridSpec(
            num_scalar_prefetch=1, grid=(bq, n_pages + 1),
            in_specs=[page_spec2(True),
                      pl.BlockSpec((None, prow, HEAD_DIM), lambda b, p, t_: (b, 0, 0)),
                      cspec((1, page)), cspec((1, page)), cspec((page, HEAD_DIM)), cspec((page, HEAD_DIM)),
                      cspec((2, HEAD_DIM, HEAD_DIM)), cspec((2, HEAD_DIM, HEAD_DIM)), cspec((1, HEAD_DIM))],
            out_specs=[pl.BlockSpec((None, DEC_BLOCKS, kvw), lambda b, p, t_: (b, 0, 0))] * 2,
            scratch_shapes=[pltpu.VMEM((DEC_BLOCKS, kvw), F32)] * 2),
        out_shape=[jax.ShapeDtypeStruct((bq, DEC_BLOCKS, kvw), F32)] * 2,
        compiler_params=_cparams(("parallel", "arbitrary"), 16 << 20), name="nsa_dec_compress",
    )(pt, pool2, newpage, jnp.tile(cmp_w[0].astype(F32), reps)[None], jnp.tile(cmp_w[1].astype(F32), reps)[None],
      jnp.tile(cmp_pos[0].astype(F32), (reps, 1)), jnp.tile(cmp_pos[1].astype(F32), (reps, 1)),
      cmp_w1.astype(F32), cmp_w2.astype(F32), kn_cmp.reshape(1, HEAD_DIM).astype(F32))
    blk_end = np.arange(DEC_BLOCKS) * CMP_BLK + CMP_BLK - 1
    bias_c = head_rows(_dist_bias(table, p_len - blk_end))
    hspec = lambda n: pl.BlockSpec((None, None, DEC_ROWS, n), lambda b, h: (b, h, 0, 0))
    bspec = lambda n: pl.BlockSpec((None, DEC_ROWS, n), lambda b, h: (h, 0, 0))
    o1, sel = pl.pallas_call(
        functools.partial(_nsa_dec_cmp_kernel, grp=grp, qpos=p_len, n_sel=min(N_SEL, p_len // CMP_BLK + 1), scale=scale),
        grid=(bq, nkv),
        in_specs=[hspec(HEAD_DIM), pl.BlockSpec((None, DEC_BLOCKS, HEAD_DIM), lambda b, h: (b, 0, h)),
                  pl.BlockSpec((None, DEC_BLOCKS, HEAD_DIM), lambda b, h: (b, 0, h)), bspec(DEC_BLOCKS), hspec(3)],
        out_specs=[hspec(HEAD_DIM), hspec(DEC_BLOCKS)],
        out_shape=[jax.ShapeDtypeStruct((bq, nkv, DEC_ROWS, HEAD_DIM), F32),
                   jax.ShapeDtypeStruct((bq, nkv, DEC_ROWS, DEC_BLOCKS), F32)],
        compiler_params=_cparams(("parallel", "parallel"), 8 << 20), name="nsa_dec_cmp",
    )(qg, kc, vc, bias_c, gg)
    blast = head_rows(_dist_bias(table, page - np.arange(page)))
    bfar = head_rows(_dist_bias(table, np.array([MAX_DIST])))
    b0 = head_rows(_dist_bias(table, np.array([0])))
    hspec3 = lambda r, n: pl.BlockSpec((None, nkv, r, n), lambda b, p, t_: (b, 0, 0, 0))
    bspec3 = lambda n: pl.BlockSpec((nkv, DEC_ROWS, n), lambda b, p, t_: (0, 0, 0))
    ksn = rows_new[:, 2 * kvw:3 * kvw].reshape(bq, nkv, 1, HEAD_DIM)
    vsn = rows_new[:, 3 * kvw:].reshape(bq, nkv, 1, HEAD_DIM)
    o2 = pl.pallas_call(
        functools.partial(_nsa_dec_slc_kernel, nkv=nkv, n_pages=n_pages, page=page, scale=scale),
        grid_spec=pltpu.PrefetchScalarGridSpec(
            num_scalar_prefetch=1, grid=(bq, n_pages),
            in_specs=[hspec3(DEC_ROWS, HEAD_DIM), page_spec2(False), hspec3(DEC_ROWS, DEC_BLOCKS),
                      bspec3(page), bspec3(1), bspec3(1), hspec3(1, HEAD_DIM), hspec3(1, HEAD_DIM),
                      hspec3(DEC_ROWS, 3), hspec3(DEC_ROWS, HEAD_DIM)],
            out_specs=hspec3(DEC_ROWS, HEAD_DIM),
            scratch_shapes=[pltpu.VMEM((nkv, DEC_ROWS, 1), F32), pltpu.VMEM((nkv, DEC_ROWS, 1), F32),
                            pltpu.VMEM((nkv, DEC_ROWS, HEAD_DIM), F32)]),
        out_shape=jax.ShapeDtypeStruct((bq, nkv, DEC_ROWS, HEAD_DIM), F32),
        compiler_params=_cparams(("parallel", "arbitrary"), 8 << 20), name="nsa_dec_slc",
    )(pt, qg, pool2, sel, blast, bfar, b0, ksn, vsn, gg, o1)
    kpos_ok = np.arange(lb) >= lb - p_len
    assert kpos_ok.all()
    bias_w = head_rows(_dist_bias(table, lb - np.arange(lb)))
    win3 = win_buf.reshape(bq, lb, 2 * kvw)
    nspec2 = pl.BlockSpec((None, None, 1, HEAD_DIM), lambda b, h: (b, h, 0, 0))
    kwn = wrows_new[:, :kvw].reshape(bq, nkv, 1, HEAD_DIM)
    vwn = wrows_new[:, kvw:].reshape(bq, nkv, 1, HEAD_DIM)
    o3 = pl.pallas_call(
        functools.partial(_nsa_dec_win_kernel, scale=scale),
        grid=(bq, nkv),
        in_specs=[hspec(HEAD_DIM), pl.BlockSpec((None, lb, HEAD_DIM), lambda b, h: (b, 0, h)),
                  pl.BlockSpec((None, lb, HEAD_DIM), lambda b, h: (b, 0, nkv + h)), bspec(lb), bspec(1),
                  nspec2, nspec2, hspec(3), hspec(HEAD_DIM)],
        out_specs=hspec(HEAD_DIM),
        out_shape=jax.ShapeDtypeStruct((bq, nkv, DEC_ROWS, HEAD_DIM), F32),
        compiler_params=_cparams(("parallel", "parallel"), 8 << 20), name="nsa_dec_win",
    )(qg, win3, win3, bias_w, b0, kwn, vwn, gg, o2)
    return o3[:, :, :grp].reshape(bq, d).astype(BF16)


def nsa_layer(xp, xs, hp, hs, b, t, n_s, cache, layer, win_buf, page_table, w_in, qn, kn, cmp_pos, cmp_w, cmp_w1, cmp_w2, w_out, table):
    n_p, d = hp.shape
    nh = d // HEAD_DIM
    nkv = cache.shape[-2]
    tmp = _row_tile(n_p, 1024)
    q16, gates, rows, wrows, rows16, wrows16 = _nsa_proj(hp, w_in, qn, kn, tmp, nkv)
    rows5 = rows.reshape(b, t, 4, nkv, HEAD_DIM)
    wrows5 = wrows.reshape(b, t, 2, nkv, HEAD_DIM)
    kc, vc = nsa_compress(rows, b, t, nkv, cmp_pos, cmp_w, cmp_w1, cmp_w2, kn[0])
    og = nsa_attention(q16, rows16, wrows16, gates, kc, vc, table, b, t, nkv)
    xp = matmul(og, [(w_out, 0)], [(d, F32, 0)], _resid_epi, tm=tmp, tn=512, nj=d // 512, tiles=[(xp, 0)], name="nsa_out")[0]
    lbw = min(WINDOW, t)
    out_p = (rows5[None], wrows5[:, t - lbw:][None])
    sr = hs.shape[0]
    q16, gates, rows, wrows, _, _ = _nsa_proj(hs, w_in, qn, kn, sr, nkv)
    rows_new = rows[:n_s].reshape(n_s, 1, 4, nkv, HEAD_DIM)
    wrows_new = wrows[:n_s].reshape(n_s, 1, 2, nkv, HEAD_DIM)
    lb_ = win_buf.shape[1]
    kw = jnp.concatenate([win_buf, wrows_new], axis=1)
    ogs = nsa_decode(q16[:n_s], gates[:n_s], rows[:n_s], wrows[:n_s], cache, layer, win_buf, page_table,
                     cmp_pos, cmp_w, cmp_w1, cmp_w2, kn[0], table)
    ogs = jnp.pad(ogs, ((0, sr - n_s), (0, 0)))
    xs = matmul(ogs, [(w_out, 0)], [(d, F32, 0)], _resid_epi, tm=sr, tn=512, nj=d // 512, tiles=[(xs, 0)], name="nsa_out")[0]
    out_s = (rows_new[None], kw[:, -lb_:][None])
    return xp, xs, out_p + out_s


def kernel(x_prompt, x_sample, cache_fox_kv, cache_fox_logf, state_hgrn, state_s5, cache_nsa_kv, state_nsa_win, page_table, norm_mix, norm_ffn, fox_w_in, fox_b_f, fox_q_norm, fox_k_norm, fox_w_out, hg_w_in, hg_lb, hg_o_norm, hg_w_out, s5_a_re, s5_a_im, s5_log_dt, s5_b_re, s5_b_im, s5_c_re, s5_c_im, s5_d, s5_w_glu, nsa_w_in, nsa_q_norm, nsa_k_norm, nsa_cmp_pos, nsa_cmp_w, nsa_cmp_w1, nsa_cmp_w2, nsa_w_out, rel_bias, ffn_w1, ffn_w3, ffn_w2, moe_router, moe_router_b, moe_w1, moe_w3, moe_w2):
    b, t, d = x_prompt.shape
    n_s = x_sample.shape[0]
    assert x_sample.shape[1] == 1
    depth = norm_mix.shape[0]
    xp = x_prompt.reshape(b * t, d)
    xs = jnp.pad(x_sample.reshape(n_s, d), ((0, SAMPLE_ROWS - n_s), (0, 0)))
    sm = jax.nn.softmax(hg_lb.astype(F32), axis=0)
    lower_bounds = jnp.cumsum(sm, axis=0) - sm[0]
    res = {}
    for i in range(depth):
        j = i // 4
        kind = i % 4
        if kind == 2:
            hp = rms_norm(xp, norm_mix[i], F32)
            hs = rms_norm(xs, norm_mix[i], F32)
        else:
            hp = rms_norm(xp, norm_mix[i])
            hs = rms_norm(xs, norm_mix[i])
        if kind == 0:
            xp, xs, o = fox_layer(xp, xs, hp, hs, b, t, n_s, cache_fox_kv, j, cache_fox_logf[j], page_table,
                                  fox_w_in[j], fox_b_f[j], fox_q_norm[j], fox_k_norm[j], fox_w_out[j])
        elif kind == 1:
            xp, xs, o = hgrn_layer(xp, xs, hp, hs, b, t, n_s, state_hgrn[j], hg_w_in[j], lower_bounds[i],
                                   hg_o_norm[j], hg_w_out[j])
        elif kind == 2:
            xp, xs, o = s5_layer(xp, xs, hp, hs, b, t, n_s, state_s5[j], s5_a_re[j], s5_a_im[j], s5_log_dt[j],
                                 s5_b_re[j], s5_b_im[j], s5_c_re[j], s5_c_im[j], s5_d[j], s5_w_glu[j])
        else:
            xp, xs, o = nsa_layer(xp, xs, hp, hs, b, t, n_s, cache_nsa_kv, j, state_nsa_win[j], page_table,
                                  nsa_w_in[j], nsa_q_norm[j], nsa_k_norm[j], nsa_cmp_pos[j], nsa_cmp_w[j],
                                  nsa_cmp_w1[j], nsa_cmp_w2[j], nsa_w_out[j], rel_bias)
        res[kind] = o
        hp = rms_norm(xp, norm_ffn[i])
        hs = rms_norm(xs, norm_ffn[i])
        f = i // 2
        if i % 2 == 0:
            xp, xs = dense_ffn(xp, xs, hp, hs, ffn_w1[f], ffn_w3[f], ffn_w2[f])
        else:
            xp, xs = moe_ffn(xp, xs, hp, hs, n_s, moe_router[f], moe_router_b[f], moe_w1[f], moe_w3[f], moe_w2[f])
    return (xp.reshape(b, t, d), xs[:n_s].reshape(n_s, 1, d)) + res[0] + res[1] + res[2] + res[3]
```

```python
import functools
import math

import numpy as np
import jax
import jax.numpy as jnp
from jax import lax
from jax.experimental import pallas as pl
from jax.experimental.pallas import tpu as pltpu

F32 = jnp.float32
BF16 = jnp.bfloat16
EPS = 1e-6
NEG = -1e30
FORCE = 1e9

HEAD_DIM = 128
LANES = 128
V7X_VMEM_CAP_MB = 60
SAMPLE_ROWS = 16

HG_CHUNK = 16
S5_CH = 16
S5_CHUNK = 128
CMP_BLK = 64
SEL_BLK = 64
N_SEL = 16
WINDOW = 512
N_BUCKETS = 32
MAX_DIST = 128
TOP_K = 2
MOE_ROWS = 512


def _cparams(sem, vmem_bytes):
    mb = min(V7X_VMEM_CAP_MB, max(16, int(vmem_bytes / (1 << 20)) + 8))
    return pltpu.CompilerParams(dimension_semantics=sem, vmem_limit_bytes=mb << 20)


def _rms_kernel(x_ref, g_ref, o_ref):
    x = x_ref[...]
    y = x * lax.rsqrt(jnp.mean(x * x, axis=-1, keepdims=True) + EPS)
    o_ref[...] = (y * g_ref[...]).astype(o_ref.dtype)


def rms_norm(x, g, out_dtype=BF16):
    n, d = x.shape
    tm = min(n, 512)
    return pl.pallas_call(
        _rms_kernel,
        grid=(n // tm,),
        in_specs=[pl.BlockSpec((tm, d), lambda i: (i, 0)),
                  pl.BlockSpec((1, d), lambda i: (0, 0))],
        out_specs=pl.BlockSpec((tm, d), lambda i: (i, 0)),
        out_shape=jax.ShapeDtypeStruct((n, d), out_dtype),
        compiler_params=_cparams(("parallel",), 2 * tm * d * (4 + 4)),
        name="rms_norm",
    )(x, g.reshape(1, d).astype(F32))


def _mm_kernel(*refs, nw, nv, nt, no, ni, has_e, epilogue):
    pos = 0
    eid_ref = None
    if has_e:
        eid_ref = refs[0]
        pos = 1
    lhs_ref = refs[pos]
    pos += 1
    w_refs = refs[pos:pos + nw]
    pos += nw
    vec_refs = refs[pos:pos + nv]
    pos += nv
    tile_refs = refs[pos:pos + nt]
    pos += nt
    out_refs = refs[pos:pos + no]
    wc_ref = refs[-1]
    j = pl.program_id(0)
    i = pl.program_id(1)
    if has_e:
        prev = eid_ref[jnp.maximum(i - 1, 0)]
        recast = jnp.logical_or(i == 0, eid_ref[i] != prev)
        active = i < eid_ref[ni]
    else:
        recast = i == 0
        active = None

    @pl.when(recast)
    def _():
        for t in range(nw):
            wc_ref[t] = w_refs[t][...].astype(BF16)

    def compute():
        x = lhs_ref[...]
        accs = [jnp.dot(x, wc_ref[t], preferred_element_type=F32) for t in range(nw)]
        res = epilogue(accs, [v[...] for v in vec_refs], [t[...] for t in tile_refs], j)
        for o, r in zip(out_refs, res):
            o[...] = r.astype(o.dtype)

    if has_e:
        pl.when(active)(compute)

        @pl.when(jnp.logical_not(active))
        def _():
            for o in out_refs:
                o[...] = jnp.zeros(o.shape, o.dtype)
    else:
        compute()


def matmul(lhs, ws, outs, epilogue, *, tm, tn, nj, tk=None, kb=0, vecs=(), tiles=(), eids=None, layer=None, name="mm"):
    n = lhs.shape[0]
    assert n % tm == 0, (n, tm)
    ni = n // tm
    kdim = ws[0][0].shape[-2]
    tk = tk or kdim
    assert kdim % tk == 0
    has_e = eids is not None
    nw, nv, nt, no = len(ws), len(vecs), len(tiles), len(outs)

    def wspec(arr, off):
        lead = () if layer is None else (layer,)
        none = (None,) * len(lead)
        if arr.ndim - len(lead) == 3:
            return pl.BlockSpec(none + (None, tk, tn), lambda j, i, e: lead + (e[i], kb, j + off))
        if has_e:
            return pl.BlockSpec(none + (tk, tn), lambda j, i, e: lead + (kb, j + off))
        return pl.BlockSpec(none + (tk, tn), lambda j, i: lead + (kb, j + off))

    def fix(f):
        return (lambda j, i, e: f(j, i)) if has_e else f

    def fix_in(f):
        return (lambda j, i, e: f(j, jnp.minimum(i, e[ni] - 1))) if has_e else f

    in_specs = [pl.BlockSpec((tm, tk), fix_in(lambda j, i: (i, kb)))]
    in_specs += [wspec(a, off) for a, off in ws]
    in_specs += [pl.BlockSpec((1, tn), fix(functools.partial(lambda j, i, off: (0, j + off), off=off))) for _, off in vecs]
    in_specs += [pl.BlockSpec((tm, tn), fix_in(functools.partial(lambda j, i, off: (i, j + off), off=off))) for _, off in tiles]
    out_specs = [pl.BlockSpec((tm, tn), fix(functools.partial(lambda j, i, off: (i, j + off), off=off))) for _, _, off in outs]
    out_shape = [jax.ShapeDtypeStruct((n, c), dt) for c, dt, _ in outs]
    vmem = (2 * tm * tk * 2 + nw * (2 * tk * tn * 4 + tk * tn * 2)
            + sum(2 * tm * tn * jnp.dtype(dt).itemsize for _, dt, _ in outs)
            + nt * 2 * tm * tn * 4 + (nw + 1) * tm * tn * 4)
    kern = functools.partial(_mm_kernel, nw=nw, nv=nv, nt=nt, no=no, ni=ni, has_e=has_e, epilogue=epilogue)
    scratch = [pltpu.VMEM((nw, tk, tn), BF16)]
    args = [lhs] + [a for a, _ in ws] + [a for a, _ in vecs] + [a for a, _ in tiles]
    if has_e:
        grid_spec = pltpu.PrefetchScalarGridSpec(num_scalar_prefetch=1, grid=(nj, ni), in_specs=in_specs,
                                                 out_specs=out_specs, scratch_shapes=scratch)
        args = [eids] + args
    else:
        grid_spec = pl.GridSpec(grid=(nj, ni), in_specs=in_specs, out_specs=out_specs, scratch_shapes=scratch)
    return pl.pallas_call(kern, grid_spec=grid_spec, out_shape=out_shape,
                          compiler_params=_cparams(("arbitrary", "arbitrary"), vmem), name=name)(*args)


def _row_tile(n, pref):
    return pref if n % pref == 0 else n


def _head_rms(x, g):
    parts = []
    for h in range(x.shape[1] // HEAD_DIM):
        xh = x[:, h * HEAD_DIM:(h + 1) * HEAD_DIM]
        gh = g[:, h * HEAD_DIM:(h + 1) * HEAD_DIM]
        parts.append(xh * lax.rsqrt(jnp.mean(xh * xh, axis=-1, keepdims=True) + EPS) * gh)
    return parts[0] if len(parts) == 1 else jnp.concatenate(parts, axis=-1)


def _vec(v, width=None):
    v = v.reshape(1, -1).astype(F32)
    if width is not None and v.shape[1] < width:
        v = jnp.pad(v, ((0, 0), (0, width - v.shape[1])))
    return v


def _valid_lanes(x, n):
    return jnp.where(lax.broadcasted_iota(jnp.int32, x.shape, 1) < n, x, 0.0)


def _log_sigmoid(x):
    return jnp.minimum(x, 0.0) - jnp.log(1.0 + jnp.exp(-jnp.abs(x)))


def _swiglu_epi(accs, vecs, tiles, j):
    a, b = accs
    return [a * jax.nn.sigmoid(a) * b]


def _resid_epi(accs, vecs, tiles, j):
    return [accs[0] + tiles[0]]


def _plain_epi(accs, vecs, tiles, j):
    return [accs[0]]


def ffn_up(h, w1, w3, layer, eids=None, tm=1024):
    dff = w1.shape[-1]
    tn = 256
    assert dff % tn == 0
    return matmul(h, [(w1, 0), (w3, 0)], [(dff, BF16, 0)], _swiglu_epi, tm=tm, tn=tn, nj=dff // tn,
                  eids=eids, layer=layer, name="ffn_up")[0]


def ffn_down(hmid, w2, resid, layer, eids=None, tm=512):
    dff, d = w2.shape[-2], w2.shape[-1]
    halves = 2 if dff % (2 * LANES) == 0 and dff > 4096 else 1
    tk = dff // halves
    tn = min(512, d)
    out = resid
    for kb in range(halves):
        out = matmul(hmid, [(w2, 0)], [(d, F32, 0)], _resid_epi if out is not None else _plain_epi, tm=tm, tn=tn,
                     nj=d // tn, tk=tk, kb=kb, tiles=[(out, 0)] if out is not None else [], eids=eids, layer=layer,
                     name="ffn_down")[0]
    return out


def dense_ffn(xp, xs, hp, hs, w1, w3, w2, layer):
    mp = ffn_up(hp, w1, w3, layer, tm=_row_tile(hp.shape[0], 1024))
    xp = ffn_down(mp, w2, xp, layer, tm=_row_tile(hp.shape[0], 512))
    ms = ffn_up(hs, w1, w3, layer, tm=hs.shape[0])
    xs = ffn_down(ms, w2, xs, layer, tm=hs.shape[0])
    return xp, xs


def _router_epi(accs, vecs, tiles, j):
    return [accs[0] + vecs[0]]


def moe_ffn(xp, xs, hp, hs, n_s, router, router_b, w1, w3, w2, layer):
    n_p, d = hp.shape
    n_e = router.shape[-1]
    h_all = jnp.concatenate([hp, hs[:n_s]], axis=0)
    n = n_p + n_s
    rw = jnp.pad(router, ((0, 0), (0, LANES - n_e)))
    lp = matmul(hp, [(rw, 0)], [(LANES, F32, 0)], _router_epi, tm=_row_tile(n_p, 1024), tn=LANES, nj=1,
                vecs=[(_vec(router_b, LANES), 0)], name="router")[0]
    ls = matmul(hs, [(rw, 0)], [(LANES, F32, 0)], _router_epi, tm=hs.shape[0], tn=LANES, nj=1,
                vecs=[(_vec(router_b, LANES), 0)], name="router")[0]
    logits = jnp.concatenate([lp[:, :n_e], ls[:n_s, :n_e]], axis=0)
    top_l, top_e = lax.top_k(logits, TOP_K)
    gate = jax.nn.softmax(top_l, axis=-1)
    a = n * TOP_K
    bm = MOE_ROWS
    e_flat = top_e.reshape(-1)
    tok = jnp.arange(a) // TOP_K
    order = jnp.argsort(e_flat)
    e_sorted = e_flat[order]
    tok_sorted = tok[order]
    counts = jnp.bincount(e_flat, length=n_e)
    padded = (counts + bm - 1) // bm * bm
    pad_end = jnp.cumsum(padded)
    grp_start = jnp.cumsum(counts) - counts
    dest = (pad_end - padded)[e_sorted] + jnp.arange(a) - grp_start[e_sorted]
    n_blocks = -(-a // bm) + n_e
    blk_e = jnp.minimum(jnp.searchsorted(pad_end, jnp.arange(n_blocks) * bm, side='right'), n_e - 1)
    n_act = pad_end[-1] // bm
    eids = jnp.concatenate([blk_e, n_act[None]]).astype(jnp.int32)
    src = jnp.full((n_blocks * bm,), -1, jnp.int32).at[dest].set(tok_sorted.astype(jnp.int32))
    pos = jnp.zeros((a,), jnp.int32).at[order].set(dest.astype(jnp.int32)).reshape(n, TOP_K)
    buf = moe_dispatch(h_all, src, bm)
    mid = ffn_up(buf, w1, w3, layer, eids=eids, tm=bm)
    out = ffn_down(mid, w2, None, layer, eids=eids, tm=bm)
    xp = moe_combine(xp, out, pos[:n_p], gate[:n_p])
    sr = xs.shape[0]
    xs = moe_combine(xs, out, jnp.pad(pos[n_p:], ((0, sr - n_s), (0, 0))), jnp.pad(gate[n_p:], ((0, sr - n_s), (0, 0))))
    return xp, xs


def _moe_dispatch_kernel(src_ref, h_ref, o_ref, sem, *, bm):
    base = pl.program_id(0) * bm

    def row_copy(r, s):
        return pltpu.make_async_copy(h_ref.at[pl.ds(s, 1), :], o_ref.at[pl.ds(r, 1), :], sem)

    def start(r, c):
        s = src_ref[base + r]

        @pl.when(s >= 0)
        def _():
            row_copy(r, s).start()

        @pl.when(s < 0)
        def _():
            o_ref[pl.ds(r, 1), :] = jnp.zeros((1, o_ref.shape[1]), o_ref.dtype)
        return c

    def wait(r, c):
        s = src_ref[base + r]

        @pl.when(s >= 0)
        def _():
            row_copy(r, s).wait()
        return c

    lax.fori_loop(0, bm, start, 0)
    lax.fori_loop(0, bm, wait, 0)


def moe_dispatch(h, src, bm):
    n, d = h.shape
    r = src.shape[0]
    h32 = lax.bitcast_convert_type(h.reshape(n, d // 2, 2), jnp.uint32)
    out = pl.pallas_call(
        functools.partial(_moe_dispatch_kernel, bm=bm),
        grid_spec=pltpu.PrefetchScalarGridSpec(
            num_scalar_prefetch=1, grid=(r // bm,),
            in_specs=[pl.BlockSpec(memory_space=pl.ANY)],
            out_specs=pl.BlockSpec((bm, d // 2), lambda i, s: (i, 0)),
            scratch_shapes=[pltpu.SemaphoreType.DMA(())]),
        out_shape=jax.ShapeDtypeStruct((r, d // 2), jnp.uint32),
        compiler_params=_cparams(("arbitrary",), 2 * bm * d * 2), name="moe_dispatch",
    )(src, h32)
    return lax.bitcast_convert_type(out, BF16).reshape(r, d)


def _moe_combine_kernel(pos_ref, x_ref, g_ref, y_ref, o_ref, buf, sem, *, tm, top_k):
    base = pl.program_id(0) * tm

    def row_copy(r, k):
        return pltpu.make_async_copy(y_ref.at[pl.ds(pos_ref[(base + r) * top_k + k], 1), :],
                                     buf.at[k, pl.ds(r, 1), :], sem)

    def start(r, c):
        for k in range(top_k):
            row_copy(r, k).start()
        return c

    def wait(r, c):
        for k in range(top_k):
            row_copy(r, k).wait()
        return c

    lax.fori_loop(0, tm, start, 0)
    lax.fori_loop(0, tm, wait, 0)
    g = g_ref[...]
    acc = x_ref[...]
    for k in range(top_k):
        acc = acc + g[:, k:k + 1] * buf[k]
    o_ref[...] = acc


def moe_combine(x, y, pos, gate):
    n, d = x.shape
    top_k = pos.shape[1]
    tm = min(n, 256)
    tile = pl.BlockSpec((tm, d), lambda i, p: (i, 0))
    return pl.pallas_call(
        functools.partial(_moe_combine_kernel, tm=tm, top_k=top_k),
        grid_spec=pltpu.PrefetchScalarGridSpec(
            num_scalar_prefetch=1, grid=(n // tm,),
            in_specs=[tile, pl.BlockSpec((tm, top_k), lambda i, p: (i, 0)), pl.BlockSpec(memory_space=pl.ANY)],
            out_specs=tile,
            scratch_shapes=[pltpu.VMEM((top_k, tm, d), F32), pltpu.SemaphoreType.DMA(())]),
        out_shape=jax.ShapeDtypeStruct((n, d), F32),
        compiler_params=_cparams(("arbitrary",), (4 + top_k) * tm * d * 4), name="moe_combine",
    )(pos.reshape(-1).astype(jnp.int32), x, gate.astype(F32), y)


def _cumsum_kernel(x_ref, o_ref):
    c = x_ref[...]
    row = lax.broadcasted_iota(jnp.int32, c.shape, 0)
    sh = 1
    while sh < c.shape[0]:
        c = c + jnp.where(row >= sh, pltpu.roll(c, sh, 0), 0.0)
        sh *= 2
    o_ref[...] = c


def seq_cumsum(x):
    b, t, h = x.shape
    spec = pl.BlockSpec((None, t, h), lambda i: (i, 0, 0))
    return pl.pallas_call(_cumsum_kernel, grid=(b,), in_specs=[spec], out_specs=spec,
                          out_shape=jax.ShapeDtypeStruct(x.shape, F32),
                          compiler_params=_cparams(("parallel",), 8 * t * LANES * 4), name="seq_cumsum")(x)


def _fox_attn_kernel(q_ref, k_ref, v_ref, ccol_ref, crow_ref, g_ref, o_ref, *, tq, tk, scale):
    qi = pl.program_id(2)
    q = q_ref[...]
    cq = ccol_ref[...]
    qpos = qi * tq + lax.broadcasted_iota(jnp.int32, (tq, tk), 0)
    koff = lax.broadcasted_iota(jnp.int32, (tq, tk), 1)

    def body(j, carry):
        m, l, acc = carry
        start = pl.multiple_of(j * tk, tk)
        k = k_ref[pl.ds(start, tk), :]
        v = v_ref[pl.ds(start, tk), :]
        s = lax.dot_general(q, k, (((1,), (1,)), ((), ())), preferred_element_type=F32) * scale
        s = s + cq - crow_ref[pl.ds(j, 1), :]
        s = jnp.where(qpos >= koff + j * tk, s, NEG)
        m_new = jnp.maximum(m, jnp.max(s, axis=-1, keepdims=True))
        alpha = jnp.exp(m - m_new)
        p = jnp.exp(s - m_new)
        l = alpha * l + jnp.sum(p, axis=-1, keepdims=True)
        acc = alpha * acc + jnp.dot(p.astype(BF16), v, preferred_element_type=F32)
        return m_new, l, acc

    n_kv = (qi * tq + tq + tk - 1) // tk
    init = (jnp.full((tq, 1), NEG, F32), jnp.zeros((tq, 1), F32), jnp.zeros((tq, HEAD_DIM), F32))
    m, l, acc = lax.fori_loop(0, n_kv, body, init)
    o_ref[...] = (acc / l * g_ref[...]).astype(o_ref.dtype)


def fox_attention(q16, kv16, c, gate, b, t, nh):
    tq = min(256, t)
    tk = tq
    nq = t // tq
    ccol = jnp.transpose(c, (0, 2, 1)).reshape(b, nh, t, 1)
    crow = jnp.transpose(c, (0, 2, 1)).reshape(b, nh, t // tk, tk)
    kern = functools.partial(_fox_attn_kernel, tq=tq, tk=tk, scale=HEAD_DIM ** -0.5)
    return pl.pallas_call(
        kern,
        grid=(b, nh, nq),
        in_specs=[pl.BlockSpec((tq, HEAD_DIM), lambda bi, h, i: (bi * nq + i, h)),
                  pl.BlockSpec((t, HEAD_DIM), lambda bi, h, i: (bi, h)),
                  pl.BlockSpec((t, HEAD_DIM), lambda bi, h, i: (bi, nh + h)),
                  pl.BlockSpec((None, None, tq, 1), lambda bi, h, i: (bi, h, i, 0)),
                  pl.BlockSpec((None, None, t // tk, tk), lambda bi, h, i: (bi, h, 0, 0)),
                  pl.BlockSpec((tq, HEAD_DIM), lambda bi, h, i: (bi * nq + i, h))],
        out_specs=pl.BlockSpec((tq, HEAD_DIM), lambda bi, h, i: (bi * nq + i, h)),
        out_shape=jax.ShapeDtypeStruct(q16.shape, BF16),
        compiler_params=_cparams(("parallel", "parallel", "arbitrary"), 8 << 20),
        name="fox_attention",
    )(q16, kv16, kv16, ccol, crow, gate)


def _fox_proj(h, w_in, b_f, qn, kn, tm):
    d = h.shape[1]
    nh = d // HEAD_DIM
    tn = 512
    nb = d // tn
    qn_v, kn_v = _vec(qn), _vec(kn)

    def q_epi(accs, vecs, tiles, j):
        return [_head_rms(accs[0], vecs[0])]

    def kv_epi(accs, vecs, tiles, j):
        y = jnp.where(j < nb, _head_rms(accs[0], vecs[0]), accs[0])
        return [y, y]

    def gate_epi(accs, vecs, tiles, j):
        return [jax.nn.sigmoid(accs[0])]

    def lf_epi(accs, vecs, tiles, j):
        return [_valid_lanes(_log_sigmoid(accs[0] + vecs[0]), nh)]

    def tiled(v):
        return jnp.tile(v, (1, 2 * d // HEAD_DIM))

    q16 = matmul(h, [(w_in, 0)], [(d, BF16, 0)], q_epi, tm=tm, tn=tn, nj=nb, vecs=[(tiled(qn_v), 0)], name="fox_q")[0]
    kv, kv16 = matmul(h, [(w_in, nb)], [(2 * d, F32, 0), (2 * d, BF16, 0)], kv_epi, tm=tm, tn=tn, nj=2 * nb,
                      vecs=[(tiled(kn_v), 0)], name="fox_kv")
    gate = matmul(h, [(w_in, 3 * nb)], [(d, F32, 0)], gate_epi, tm=tm, tn=tn, nj=nb, name="fox_gate")[0]
    lf = matmul(h, [(w_in, 4 * d // LANES)], [(LANES, F32, 0)], lf_epi, tm=tm, tn=LANES, nj=1,
                vecs=[(_vec(b_f, LANES), 0)], name="fox_logf")[0][:, :nh]
    return q16, kv, kv16, gate, lf


def _lane_column(x, lane, n):
    if x.shape[1] <= LANES:
        return x[:, n:n + 1]
    return jnp.sum(jnp.where(lane == n, x, 0.0), axis=-1, keepdims=True)


def _fox_decode3_kernel(pt_ref, q_ref, kv_ref, lft_ref, kn_ref, vn_ref, lfn_ref, g_ref, o_ref,
                        m_s, l_s, acc_s, c_s, *, nh, page, scale):
    p = pl.program_id(1)

    @pl.when(p == 0)
    def _():
        m_s[...] = jnp.full(m_s.shape, NEG, F32)
        l_s[...] = jnp.zeros(l_s.shape, F32)
        acc_s[...] = jnp.zeros(acc_s.shape, F32)
        c_s[...] = jnp.zeros(c_s.shape, F32)

    q = q_ref[...]
    x = kv_ref[...].reshape(page, 2 * nh, HEAD_DIM)
    qk = jnp.sum(x[:, :nh, :] * q[None], axis=-1, keepdims=True)
    ct = lft_ref[...]
    lane = lax.broadcasted_iota(jnp.int32, ct.shape, 1)
    sh = 1
    while sh < page:
        ct = ct + jnp.where(lane >= sh, pltpu.roll(ct, sh, 1), 0.0)
        sh *= 2
    ct = ct + c_s[...]
    c3 = jnp.stack([ct[:, t:t + 1] for t in range(page)], axis=0)
    s = qk * scale - c3
    m_old = m_s[...]
    m_new = jnp.maximum(m_old, jnp.max(s, axis=0))
    alpha = jnp.exp(m_old - m_new)
    pm = jnp.exp(s - m_new[None])
    l_s[...] = alpha * l_s[...] + jnp.sum(pm, axis=0)
    acc_s[...] = alpha * acc_s[...] + jnp.sum(pm * x[:, nh:, :], axis=0)
    m_s[...] = m_new
    c_s[...] = ct[:, page - 1:page]

    @pl.when(p == pl.num_programs(1) - 1)
    def _():
        c_new = c_s[...] + lfn_ref[...]
        s_n = jnp.sum(kn_ref[...] * q, axis=-1, keepdims=True) * scale - c_new
        m_f = jnp.maximum(m_s[...], s_n)
        a_f = jnp.exp(m_s[...] - m_f)
        p_n = jnp.exp(s_n - m_f)
        l_f = a_f * l_s[...] + p_n
        acc = a_f * acc_s[...] + p_n * vn_ref[...]
        o_ref[...] = acc / l_f * g_ref[...]


def fox_decode3(q, k_new, v_new, lf_new, gate, kv_cache, lf_pool, page_table, layer):
    bq, d = q.shape
    nh = d // HEAD_DIM
    n_phys, page = kv_cache.shape[1], kv_cache.shape[2]
    n_pages = page_table.shape[1]
    rows = page * 2 * nh
    kv2 = kv_cache.reshape(-1, HEAD_DIM)
    lft = jnp.transpose(lf_pool.astype(F32), (0, 2, 1))

    def per_seq(n):
        return pl.BlockSpec((None, nh, n), lambda b, p, pt: (b, 0, 0))

    grid_spec = pltpu.PrefetchScalarGridSpec(
        num_scalar_prefetch=1, grid=(bq, n_pages),
        in_specs=[per_seq(HEAD_DIM),
                  pl.BlockSpec((rows, HEAD_DIM), lambda b, p, pt: (layer * n_phys + pt[b * n_pages + p], 0)),
                  pl.BlockSpec((None, nh, page), lambda b, p, pt: (pt[b * n_pages + p], 0, 0)),
                  per_seq(HEAD_DIM), per_seq(HEAD_DIM), per_seq(1), per_seq(HEAD_DIM)],
        out_specs=per_seq(HEAD_DIM),
        scratch_shapes=[pltpu.VMEM((nh, 1), F32), pltpu.VMEM((nh, 1), F32), pltpu.VMEM((nh, HEAD_DIM), F32),
                        pltpu.VMEM((nh, 1), F32)])
    out = pl.pallas_call(
        functools.partial(_fox_decode3_kernel, nh=nh, page=page, scale=HEAD_DIM ** -0.5),
        grid_spec=grid_spec, out_shape=jax.ShapeDtypeStruct((bq, nh, HEAD_DIM), F32),
        compiler_params=_cparams(("parallel", "arbitrary"), 8 * rows * HEAD_DIM * 4),
        name="fox_decode",
    )(page_table.reshape(-1).astype(jnp.int32), q.reshape(bq, nh, HEAD_DIM), kv2, lft,
      k_new.reshape(bq, nh, HEAD_DIM), v_new.reshape(bq, nh, HEAD_DIM), lf_new.reshape(bq, nh, 1),
      gate.reshape(bq, nh, HEAD_DIM))
    return out.reshape(bq, d)


def fox_layer(xp, xs, hp, hs, b, t, n_s, kv_cache, layer, lf_pool, page_table, w_in, b_f, qn, kn, w_out):
    n_p, d = hp.shape
    nh = d // HEAD_DIM
    tmp = _row_tile(n_p, 1024)
    q16, kvp, kv16, gate, lf = _fox_proj(hp, w_in, b_f, qn, kn, tmp)
    lfp = lf.reshape(b, t, nh)
    c = seq_cumsum(lfp)
    og = fox_attention(q16, kv16, c, gate, b, t, nh)
    xp = matmul(og, [(w_out, 0)], [(d, F32, 0)], _resid_epi, tm=tmp, tn=512, nj=d // 512, tiles=[(xp, 0)], name="fox_out")[0]
    sr = hs.shape[0]
    q16s, kvs, _, gates, lfs = _fox_proj(hs, w_in, b_f, qn, kn, sr)
    logf = lfs[:n_s].reshape(n_s, 1, nh)
    ogs = fox_decode3(q16s[:n_s].astype(F32), kvs[:n_s, :d], kvs[:n_s, d:], lfs[:n_s], gates[:n_s],
                      kv_cache, lf_pool, page_table, layer).astype(BF16)
    ogs = jnp.pad(ogs, ((0, sr - n_s), (0, 0)))
    xs = matmul(ogs, [(w_out, 0)], [(d, F32, 0)], _resid_epi, tm=sr, tn=512, nj=d // 512, tiles=[(xs, 0)], name="fox_out")[0]
    outs = (kvp.reshape(1, b, t, 2, nh, HEAD_DIM), lfp.reshape(1, b, t, nh),
            kvs[:n_s].reshape(1, n_s, 1, 2, nh, HEAD_DIM), logf.reshape(1, n_s, 1, nh))
    return xp, xs, outs


def _hg_proj(h, w_in, lb, tm):
    d = h.shape[1]
    tn = 512
    nb = d // tn

    def q_epi(accs, vecs, tiles, j):
        a = accs[0]
        return [a * jax.nn.sigmoid(a)]

    def f_epi(accs, vecs, tiles, j):
        f = vecs[0] + (1.0 - vecs[0]) * jax.nn.sigmoid(accs[0])
        return [1.0 - f, jnp.log(f)]

    def id_epi(accs, vecs, tiles, j):
        return [accs[0]]

    def gate_epi(accs, vecs, tiles, j):
        return [jax.nn.sigmoid(accs[0])]

    q = matmul(h, [(w_in, 0)], [(d, F32, 0)], q_epi, tm=tm, tn=tn, nj=nb, name="hg_q")[0]
    k, logf = matmul(h, [(w_in, nb)], [(d, F32, 0), (d, F32, 0)], f_epi, tm=tm, tn=tn, nj=nb,
                     vecs=[(_vec(lb), 0)], name="hg_f")
    v = matmul(h, [(w_in, 2 * nb)], [(d, F32, 0)], id_epi, tm=tm, tn=tn, nj=nb, name="hg_v")[0]
    gate = matmul(h, [(w_in, 3 * nb)], [(d, F32, 0)], gate_epi, tm=tm, tn=tn, nj=nb, name="hg_gate")[0]
    return q, k, v, logf, gate


def _hgrn_kernel(q_ref, k_ref, v_ref, lf_ref, gate_ref, on_ref, s0_ref, o_ref, sf_ref, st_s, *, hb, nchunk, chunk):
    tb = pl.program_id(2)

    @pl.when(tb == 0)
    def _():
        st_s[...] = s0_ref[...]

    lf = lf_ref[...]
    row = lax.broadcasted_iota(jnp.int32, lf.shape, 0) % chunk
    b = lf
    sh = 1
    while sh < chunk:
        b = b + jnp.where(row >= sh, pltpu.roll(b, sh, 0), 0.0)
        sh *= 2
    k = k_ref[...]
    v = v_ref[...]
    qd = q_ref[...] * jnp.exp(b)
    kd = k * jnp.exp(-b)
    tri = lax.broadcasted_iota(jnp.int32, (chunk, chunk), 0) >= lax.broadcasted_iota(jnp.int32, (chunk, chunk), 1)
    on = on_ref[...]
    for h in range(hb):
        st = st_s[h]
        cs = slice(h * HEAD_DIM, (h + 1) * HEAD_DIM)
        for c in range(nchunk):
            rs = slice(c * chunk, (c + 1) * chunk)
            bc = b[rs, cs]
            bl = bc[chunk - 1:chunk, :]
            qdc = qd[rs, cs].astype(BF16)
            kdc = kd[rs, cs].astype(BF16)
            vc = v[rs, cs].astype(BF16)
            k2 = (k[rs, cs] * jnp.exp(bl - bc)).astype(BF16)
            att = lax.dot_general(qdc, kdc, (((1,), (1,)), ((), ())), preferred_element_type=F32)
            att = jnp.where(tri, att, 0.0)
            o = lax.dot_general(qdc, st.astype(BF16), (((1,), (1,)), ((), ())), preferred_element_type=F32)
            o = o + jnp.dot(att.astype(BF16), vc, preferred_element_type=F32)
            st = st * jnp.exp(bl) + lax.dot_general(vc, k2, (((0,), (0,)), ((), ())), preferred_element_type=F32)
            y = o * lax.rsqrt(jnp.mean(o * o, axis=-1, keepdims=True) + EPS) * on
            o_ref[rs, cs] = (y * gate_ref[rs, cs]).astype(o_ref.dtype)
        st_s[h] = st

    @pl.when(tb == pl.num_programs(2) - 1)
    def _():
        sf_ref[...] = st_s[...]


def hgrn_scan(q, k, v, logf, gate, on, s0t, b, t):
    d = q.shape[1]
    nh = d // HEAD_DIM
    hb = 2
    tb = min(256, t)
    ntb = t // tb
    kern = functools.partial(_hgrn_kernel, hb=hb, nchunk=tb // HG_CHUNK, chunk=HG_CHUNK)
    tile = pl.BlockSpec((tb, hb * HEAD_DIM), lambda bi, h, i: (bi * ntb + i, h))
    st_spec = pl.BlockSpec((None, hb, HEAD_DIM, HEAD_DIM), lambda bi, h, i: (bi, h, 0, 0))
    return pl.pallas_call(
        kern,
        grid=(b, nh // hb, ntb),
        in_specs=[tile, tile, tile, tile, tile, pl.BlockSpec((1, HEAD_DIM), lambda bi, h, i: (0, 0)), st_spec],
        out_specs=[tile, st_spec],
        out_shape=[jax.ShapeDtypeStruct(q.shape, BF16), jax.ShapeDtypeStruct(s0t.shape, F32)],
        scratch_shapes=[pltpu.VMEM((hb, HEAD_DIM, HEAD_DIM), F32)],
        compiler_params=_cparams(("parallel", "parallel", "arbitrary"), 16 << 20),
        name="hgrn_scan",
    )(q, k, v, logf, gate, on.reshape(1, HEAD_DIM).astype(F32), s0t)


def hgrn_layer(xp, xs, hp, hs, b, t, n_s, state, w_in, lb, on, w_out):
    n_p, d = hp.shape
    nh = d // HEAD_DIM
    tmp = _row_tile(n_p, 1024)
    q, k, v, logf, gate = _hg_proj(hp, w_in, lb, tmp)
    s0 = jnp.zeros((b, nh, HEAD_DIM, HEAD_DIM), F32)
    og, sft = hgrn_scan(q, k, v, logf, gate, on, s0, b, t)
    xp = matmul(og, [(w_out, 0)], [(d, F32, 0)], _resid_epi, tm=tmp, tn=512, nj=d // 512, tiles=[(xp, 0)], name="hg_out")[0]
    sr = hs.shape[0]
    proj = _hg_proj(hs, w_in, lb, sr)

    def spread(a_):
        return jnp.zeros((n_s, HG_CHUNK, d), a_.dtype).at[:, 0].set(a_[:n_s]).reshape(n_s * HG_CHUNK, d)

    qs, ks, vs, lfs, gs = [spread(a_) for a_ in proj]
    ogs, sst = hgrn_scan(qs, ks, vs, lfs, gs, on, jnp.swapaxes(state.astype(F32), -1, -2), n_s, HG_CHUNK)
    ogs = jnp.pad(ogs.reshape(n_s, HG_CHUNK, d)[:, 0], ((0, sr - n_s), (0, 0)))
    xs = matmul(ogs, [(w_out, 0)], [(d, F32, 0)], _resid_epi, tm=sr, tn=512, nj=d // 512, tiles=[(xs, 0)], name="hg_out")[0]
    return xp, xs, (jnp.swapaxes(sft, -1, -2)[None], jnp.swapaxes(sst, -1, -2)[None])


def _s5_params(a_re, a_im, log_dt, b_re, b_im, c_re, c_im):
    A = lax.complex(a_re.astype(F32), a_im.astype(F32))
    dt = jnp.exp(log_dt.astype(F32))[:, None]
    Ab = jnp.exp(A * dt)
    Bb = ((Ab - 1.0) / A)[..., None] * lax.complex(b_re.astype(F32), b_im.astype(F32))
    Cc = lax.complex(c_re.astype(F32), c_im.astype(F32))
    return Ab, Bb, Cc


def _s5_mats(Ab, Bb, Cc, L):
    hp = lax.Precision.HIGHEST
    g, p = Ab.shape
    c = Bb.shape[-1]
    pw = jnp.cumprod(jnp.concatenate([jnp.ones((1, g, p), Ab.dtype), jnp.broadcast_to(Ab, (L, g, p))], axis=0), axis=0)
    kt = jnp.einsum('gcp,tgp,gpd->tgcd', Cc, pw[:L], Bb, precision=hp).real
    lag = jnp.arange(L)[None, :] - jnp.arange(L)[:, None]
    tm = jnp.where((lag >= 0)[:, :, None, None, None], kt[jnp.maximum(lag, 0)], 0.0)
    tm = jnp.transpose(tm, (2, 0, 4, 1, 3)).reshape(g, L * c, L * c)
    rc = pw[L - 1 - jnp.arange(L)][:, :, :, None] * Bb[None]
    rc = jnp.transpose(rc, (1, 0, 3, 2)).reshape(g, L * c, p)
    oc = jnp.transpose(Cc, (0, 2, 1))[:, :, None, :] * jnp.transpose(pw[1:L + 1], (1, 2, 0))[:, :, :, None]
    oc = oc.reshape(g, p, L * c)
    ab = pw[L]
    return tm, rc.real, rc.imag, oc.real, -oc.imag, ab.real, ab.imag


def _s5_kernel(u_ref, t_ref, rr_ref, ri_ref, or_ref, oi_ref, ar_ref, ai_ref, x0r_ref, x0i_ref,
               y_ref, fr_ref, fi_ref, vr_s, vi_s, xr_s, xi_s, *, gb, nch, bp, cdt):
    for g in range(gb):
        u = u_ref[g].astype(cdt)
        vr_s[g] = jnp.dot(u, rr_ref[g], preferred_element_type=F32)
        vi_s[g] = jnp.dot(u, ri_ref[g], preferred_element_type=F32)
    ar = ar_ref[...]
    ai = ai_ref[...]

    def step(n, carry):
        xr, xi = carry
        r0 = pl.multiple_of(n * bp, bp)
        xr_s[:, pl.ds(r0, bp), :] = xr
        xi_s[:, pl.ds(r0, bp), :] = xi
        vr = vr_s[:, pl.ds(r0, bp), :]
        vi = vi_s[:, pl.ds(r0, bp), :]
        return ar * xr - ai * xi + vr, ar * xi + ai * xr + vi

    xr, xi = lax.fori_loop(0, nch, step, (x0r_ref[...], x0i_ref[...]))
    fr_ref[...] = xr
    fi_ref[...] = xi
    for g in range(gb):
        u = u_ref[g].astype(cdt)
        y = jnp.dot(u, t_ref[g], preferred_element_type=F32)
        y = y + jnp.dot(xr_s[g].astype(cdt), or_ref[g], preferred_element_type=F32)
        y = y + jnp.dot(xi_s[g].astype(cdt), oi_ref[g], preferred_element_type=F32)
        y_ref[g] = y


def s5_scan(u, x0, Ab, Bb, Cc, L, cdt):
    bq, t, ng, c = u.shape
    p = Ab.shape[-1]
    bp = -(-bq // 8) * 8
    nch = t // L
    w = L * c
    rows = nch * bp
    mats = _s5_mats(Ab, Bb, Cc, L)
    tm, rr, ri, orr, oi = [m.astype(cdt) for m in mats[:5]]
    ar, ai = [m.reshape(ng, 1, p) for m in mats[5:]]
    up = jnp.pad(u, ((0, bp - bq), (0, 0), (0, 0), (0, 0)))
    ug = jnp.transpose(up.reshape(bp, nch, L, ng, c), (3, 1, 0, 2, 4)).reshape(ng, rows, w)
    x0p = jnp.pad(x0.astype(F32), ((0, bp - bq), (0, 0), (0, 0), (0, 0)))
    x0r = jnp.transpose(x0p[..., 0], (1, 0, 2))
    x0i = jnp.transpose(x0p[..., 1], (1, 0, 2))
    gb = 4
    kern = functools.partial(_s5_kernel, gb=gb, nch=nch, bp=bp, cdt=cdt)

    def spec(*shape):
        return pl.BlockSpec((gb,) + shape, lambda i: (i,) + (0,) * len(shape))

    y, fr, fi = pl.pallas_call(
        kern,
        grid=(ng // gb,),
        in_specs=[spec(rows, w), spec(w, w), spec(w, p), spec(w, p), spec(p, w), spec(p, w),
                  spec(1, p), spec(1, p), spec(bp, p), spec(bp, p)],
        out_specs=[spec(rows, w), spec(bp, p), spec(bp, p)],
        out_shape=[jax.ShapeDtypeStruct((ng, rows, w), F32), jax.ShapeDtypeStruct((ng, bp, p), F32),
                   jax.ShapeDtypeStruct((ng, bp, p), F32)],
        scratch_shapes=[pltpu.VMEM((gb, rows, p), F32)] * 4,
        compiler_params=_cparams(("parallel",), gb * rows * (4 * w * 4 + 4 * LANES * 4)),
        name="s5_scan",
    )(ug, tm, rr, ri, orr, oi, ar, ai, x0r, x0i)
    y = jnp.transpose(y.reshape(ng, nch, bp, L, c), (2, 1, 3, 0, 4)).reshape(bp, t, ng, c)[:bq]
    xf = jnp.stack([jnp.transpose(fr, (1, 0, 2)), jnp.transpose(fi, (1, 0, 2))], axis=-1)[:bq]
    return y, xf


def _s5_act_kernel(y_ref, u_ref, d_ref, o_ref):
    o_ref[...] = jax.nn.gelu(y_ref[...] + d_ref[...] * u_ref[...]).astype(o_ref.dtype)


def s5_act(y, u, dvec):
    n, d = y.shape
    tm = min(n, 512)
    tile = pl.BlockSpec((tm, d), lambda i: (i, 0))
    return pl.pallas_call(
        _s5_act_kernel, grid=(n // tm,),
        in_specs=[tile, tile, pl.BlockSpec((1, d), lambda i: (0, 0))], out_specs=tile,
        out_shape=jax.ShapeDtypeStruct((n, d), BF16),
        compiler_params=_cparams(("parallel",), 2 * tm * d * 10), name="s5_act",
    )(y, u, dvec.reshape(1, d).astype(F32))


def _glu_epi(accs, vecs, tiles, j):
    return [accs[0] * jax.nn.sigmoid(accs[1]) + tiles[0]]


def s5_layer(xp, xs, up, us, b, t, n_s, st, a_re, a_im, log_dt, b_re, b_im, c_re, c_im, dvec, w_glu):
    n_p, d = up.shape
    ng = d // S5_CH
    Ab, Bb, Cc = _s5_params(a_re, a_im, log_dt, b_re, b_im, c_re, c_im)
    x0 = jnp.zeros((b, ng, Ab.shape[-1], 2), F32)
    y, sp = s5_scan(up.reshape(b, t, ng, S5_CH), x0, Ab, Bb, Cc, 16, BF16)
    z = s5_act(y.reshape(n_p, d), up, dvec)
    tmp = _row_tile(n_p, 1024)
    nb = d // 256
    xp = matmul(z, [(w_glu, 0), (w_glu, nb)], [(d, F32, 0)], _glu_epi, tm=tmp, tn=256, nj=nb, tiles=[(xp, 0)], name="s5_glu")[0]
    sr = us.shape[0]
    y, ss = s5_scan(us[:n_s].reshape(n_s, 1, ng, S5_CH), st, Ab, Bb, Cc, 1, F32)
    z = s5_act(jnp.pad(y.reshape(n_s, d), ((0, sr - n_s), (0, 0))), us, dvec)
    xs = matmul(z, [(w_glu, 0), (w_glu, nb)], [(d, F32, 0)], _glu_epi, tm=sr, tn=256, nj=nb, tiles=[(xs, 0)], name="s5_glu")[0]
    return xp, xs, (sp[None], ss[None])


def _nsa_proj(h, w_in, qn, kn, tm, nkv):
    d = h.shape[1]
    kvw = nkv * HEAD_DIM
    tn = 512
    nb = d // tn
    assert kvw == tn

    def q_epi(accs, vecs, tiles, j):
        return [_head_rms(accs[0], vecs[0])]

    def rows_epi(accs, vecs, tiles, j):
        y = jnp.where(j == 2, _head_rms(accs[0], vecs[0]), accs[0])
        return [y, y]

    def win_epi(accs, vecs, tiles, j):
        y = jnp.where(j == 0, _head_rms(accs[0], vecs[0]), accs[0])
        return [y, y]

    def gate_epi(accs, vecs, tiles, j):
        return [_valid_lanes(jax.nn.sigmoid(accs[0]), 3 * (d // HEAD_DIM))]

    q16 = matmul(h, [(w_in, 0)], [(d, BF16, 0)], q_epi, tm=tm, tn=tn, nj=nb, vecs=[(jnp.tile(_vec(qn), (1, d // HEAD_DIM)), 0)], name="nsa_q")[0]
    rows, rows16 = matmul(h, [(w_in, nb)], [(4 * kvw, F32, 0), (4 * kvw, BF16, 0)], rows_epi, tm=tm, tn=tn, nj=4,
                          vecs=[(jnp.tile(_vec(kn[1]), (1, 4 * nkv)), 0)], name="nsa_rows")
    wrows, wrows16 = matmul(h, [(w_in, nb + 4)], [(2 * kvw, F32, 0), (2 * kvw, BF16, 0)], win_epi, tm=tm, tn=tn, nj=2,
                            vecs=[(jnp.tile(_vec(kn[2]), (1, 2 * nkv)), 0)], name="nsa_wrows")
    nhq = d // HEAD_DIM
    gates = matmul(h, [(w_in, (d + 6 * kvw) // LANES)], [(LANES, F32, 0)], gate_epi, tm=tm, tn=LANES, nj=1, name="nsa_gates")[0][:, :3 * nhq]
    return q16, gates, rows, wrows, rows16, wrows16


def _bucket_table():
    n = np.arange(MAX_DIST + 1)
    exact = N_BUCKETS // 2
    lg = np.log(np.maximum(n, exact).astype(np.float32) / np.float32(exact)) / np.float32(math.log(MAX_DIST / exact))
    large = np.minimum(exact + (lg * (N_BUCKETS - exact)).astype(np.int32), N_BUCKETS - 1)
    return np.where(n < exact, n, large).astype(np.int32)


def _dist_bias(table, dist):
    bt = table.astype(F32)[_bucket_table()]
    return jnp.moveaxis(bt[np.clip(dist, 0, MAX_DIST)], -1, 0)


def _nsa_compress_kernel(rows_ref, pk_ref, pv_ref, posk_ref, posv_ref, w1_ref, w2_ref, kn_ref, kc_ref, vc_ref, *, nkv):
    hp = lax.Precision.HIGHEST
    kvw = nkv * HEAD_DIM
    xk = rows_ref[:, :kvw] + posk_ref[...]
    xv = rows_ref[:, kvw:2 * kvw] + posv_ref[...]
    pooled = (jnp.dot(pk_ref[...], xk, preferred_element_type=F32, precision=hp),
              jnp.dot(pv_ref[...], xv, preferred_element_type=F32, precision=hp))
    _cmp_mlp(pooled, w1_ref, w2_ref, kn_ref, kc_ref, vc_ref, nkv)


def _cmp_mlp(pooled, w1_ref, w2_ref, kn_ref, kc_ref, vc_ref, nkv):
    for which, out in ((0, kc_ref), (1, vc_ref)):
        w1 = w1_ref[which].astype(BF16)
        w2 = w2_ref[which].astype(BF16)
        for h in range(nkv):
            x = pooled[which][:, h * HEAD_DIM:(h + 1) * HEAD_DIM].astype(BF16)
            a = jnp.dot(x, w1, preferred_element_type=F32)
            y = jnp.dot((a * jax.nn.sigmoid(a)).astype(BF16), w2, preferred_element_type=F32)
            if which == 0:
                y = y * lax.rsqrt(jnp.mean(y * y, axis=-1, keepdims=True) + EPS) * kn_ref[...]
            out[:, h * HEAD_DIM:(h + 1) * HEAD_DIM] = y


def nsa_compress(rows, b, t, nkv, cmp_pos, cmp_w, cmp_w1, cmp_w2, kn_cmp):
    nb = t // CMP_BLK
    kvw = nkv * HEAD_DIM
    eye = jnp.eye(nb, dtype=F32)
    pk = jnp.kron(eye, cmp_w[0].astype(F32)[None, :])
    pv = jnp.kron(eye, cmp_w[1].astype(F32)[None, :])
    posk = jnp.tile(cmp_pos[0].astype(F32), (nb, nkv))
    posv = jnp.tile(cmp_pos[1].astype(F32), (nb, nkv))

    def full(shape):
        return pl.BlockSpec(shape, lambda bi: (0,) * len(shape))

    return pl.pallas_call(
        functools.partial(_nsa_compress_kernel, nkv=nkv),
        grid=(b,),
        in_specs=[pl.BlockSpec((t, 2 * kvw), lambda bi: (bi, 0)), full((nb, t)), full((nb, t)), full((t, kvw)),
                  full((t, kvw)), full((2, HEAD_DIM, HEAD_DIM)), full((2, HEAD_DIM, HEAD_DIM)), full((1, HEAD_DIM))],
        out_specs=[pl.BlockSpec((None, nb, kvw), lambda bi: (bi, 0, 0))] * 2,
        out_shape=[jax.ShapeDtypeStruct((b, nb, kvw), F32)] * 2,
        compiler_params=_cparams(("parallel",), 2 * t * 2 * kvw * 4 + 4 * t * kvw * 4),
        name="nsa_compress",
    )(rows, pk, pv, posk, posv, cmp_w1.astype(F32), cmp_w2.astype(F32), kn_cmp.reshape(1, HEAD_DIM).astype(F32))


def _nsa_cmp_attn_kernel(q_ref, kc_ref, vc_ref, bias_ref, g_ref, o_ref, sel_ref, *, tq, nb, grp, n_sel, scale):
    i = pl.program_id(2)
    kc = kc_ref[...].astype(BF16)
    vc = vc_ref[...].astype(BF16)
    qpos = i * tq + lax.broadcasted_iota(jnp.int32, (tq, nb), 0)
    blk = lax.broadcasted_iota(jnp.int32, (tq, nb), 1)
    vis = qpos >= blk * CMP_BLK + (CMP_BLK - 1)
    visf = jnp.where(vis, 1.0, 0.0)
    gates = g_ref[...]
    imp = jnp.zeros((tq, nb), F32)
    for g in range(grp):
        q = q_ref[:, g * HEAD_DIM:(g + 1) * HEAD_DIM]
        s = lax.dot_general(q, kc, (((1,), (1,)), ((), ())), preferred_element_type=F32) * scale + bias_ref[g]
        s = jnp.where(vis, s, NEG)
        p = jnp.exp(s - jnp.max(s, axis=-1, keepdims=True))
        p = p / jnp.sum(p, axis=-1, keepdims=True) * visf
        imp = imp + p
        o = jnp.dot(p.astype(BF16), vc, preferred_element_type=F32)
        o_ref[:, g * HEAD_DIM:(g + 1) * HEAD_DIM] = gates[:, g:g + 1] * o
    cur = qpos // SEL_BLK
    forced = jnp.logical_or(blk == 0, jnp.logical_or(blk == cur, blk == cur - 1))
    score = jnp.where(blk <= cur, jnp.where(forced, FORCE, imp), -1.0)
    rank = jnp.zeros((tq, nb), F32)
    for n in range(nb):
        col = score[:, n:n + 1]
        beats = jnp.logical_or(col > score, jnp.logical_and(col == score, blk > n))
        rank = rank + jnp.where(beats, 1.0, 0.0)
    sel_ref[...] = jnp.where(jnp.logical_and(rank < n_sel, score >= 0.0), 1.0, 0.0)


def _nsa_slc_kernel(q_ref, k_ref, v_ref, sel_ref, tz_ref, g_ref, oin_ref, o_ref, *, tq, nb, grp, scale):
    i = pl.program_id(2)
    tk = tq
    sel = sel_ref[...].astype(BF16)
    qpos = i * tq + lax.broadcasted_iota(jnp.int32, (tq, tk), 0)
    koff = lax.broadcasted_iota(jnp.int32, (tq, tk), 1)
    n_iota = lax.broadcasted_iota(jnp.int32, (nb, tk), 0)
    c_blk = lax.broadcasted_iota(jnp.int32, (nb, tk), 1) // SEL_BLK
    gates = g_ref[...]
    for g in range(grp):
        q = q_ref[:, g * HEAD_DIM:(g + 1) * HEAD_DIM]

        def body(j, carry, q=q, g=g):
            m, l, acc = carry
            start = pl.multiple_of(j * tk, tk)
            k = k_ref[pl.ds(start, tk), :]
            v = v_ref[pl.ds(start, tk), :]
            e = jnp.where(n_iota == (tk // SEL_BLK) * j + c_blk, 1.0, 0.0).astype(BF16)
            chosen = jnp.dot(sel, e, preferred_element_type=F32) > 0.5
            mask = jnp.logical_and(chosen, qpos >= koff + j * tk)
            s = lax.dot_general(q, k, (((1,), (1,)), ((), ())), preferred_element_type=F32) * scale
            s = s + tz_ref[g, jnp.minimum(i - j, 2)]
            s = jnp.where(mask, s, NEG)
            m_new = jnp.maximum(m, jnp.max(s, axis=-1, keepdims=True))
            alpha = jnp.exp(m - m_new)
            p = jnp.exp(s - m_new)
            l = alpha * l + jnp.sum(p, axis=-1, keepdims=True)
            acc = alpha * acc + jnp.dot(p.astype(BF16), v, preferred_element_type=F32)
            return m_new, l, acc

        init = (jnp.full((tq, 1), NEG, F32), jnp.zeros((tq, 1), F32), jnp.zeros((tq, HEAD_DIM), F32))
        m, l, acc = lax.fori_loop(0, i + 1, body, init)
        cs = slice(g * HEAD_DIM, (g + 1) * HEAD_DIM)
        o_ref[:, cs] = oin_ref[:, cs] + gates[:, grp + g:grp + g + 1] * (acc / l)


def _nsa_win_kernel(q_ref, k_ref, v_ref, wz_ref, g_ref, oin_ref, o_ref, *, tq, grp, scale, window):
    i = pl.program_id(2)
    span = window + tq
    start = pl.multiple_of(i * tq, tq)
    kw = k_ref[pl.ds(start, span), :]
    vw = v_ref[pl.ds(start, span), :]
    r = lax.broadcasted_iota(jnp.int32, (tq, span), 0)
    c = lax.broadcasted_iota(jnp.int32, (tq, span), 1)
    dist = r + window - c
    mask = jnp.logical_and(jnp.logical_and(dist >= 0, dist <= window), i * tq - window + c >= 0)
    gates = g_ref[...]
    for g in range(grp):
        cs = slice(g * HEAD_DIM, (g + 1) * HEAD_DIM)
        s = lax.dot_general(q_ref[:, cs], kw, (((1,), (1,)), ((), ())), preferred_element_type=F32) * scale + wz_ref[g]
        s = jnp.where(mask, s, NEG)
        p = jnp.exp(s - jnp.max(s, axis=-1, keepdims=True))
        o = jnp.dot(p.astype(BF16), vw, preferred_element_type=F32) / jnp.sum(p, axis=-1, keepdims=True)
        o_ref[:, cs] = (oin_ref[:, cs] + gates[:, 2 * grp + g:2 * grp + g + 1] * o).astype(o_ref.dtype)


def nsa_attention(q16, rows16, wrows16, gates, kc, vc, table, b, t, nkv):
    n, d = q16.shape
    nh = d // HEAD_DIM
    grp = nh // nkv
    gw = grp * HEAD_DIM
    nb = t // CMP_BLK
    scale = HEAD_DIM ** -0.5
    sem = ("parallel", "parallel", "arbitrary")
    gk = jnp.transpose(gates.reshape(n, 3, nkv, grp), (2, 0, 1, 3)).reshape(nkv, n, 3 * grp)
    tq = min(256, t)
    nq = t // tq
    dist_c = np.arange(t)[:, None] - (np.arange(nb) * CMP_BLK + CMP_BLK - 1)[None, :]
    bias_c = _dist_bias(table, dist_c)
    qspec = pl.BlockSpec((tq, gw), lambda bi, h, i: (bi * nq + i, h))
    gspec = pl.BlockSpec((None, tq, 3 * grp), lambda bi, h, i: (h, bi * nq + i, 0))
    o1, sel = pl.pallas_call(
        functools.partial(_nsa_cmp_attn_kernel, tq=tq, nb=nb, grp=grp, n_sel=min(N_SEL, nb), scale=scale),
        grid=(b, nkv, nq),
        in_specs=[qspec,
                  pl.BlockSpec((None, nb, HEAD_DIM), lambda bi, h, i: (bi, 0, h)),
                  pl.BlockSpec((None, nb, HEAD_DIM), lambda bi, h, i: (bi, 0, h)),
                  pl.BlockSpec((grp, tq, nb), lambda bi, h, i: (h, i, 0)),
                  gspec],
        out_specs=[qspec, pl.BlockSpec((None, None, tq, nb), lambda bi, h, i: (bi, h, i, 0))],
        out_shape=[jax.ShapeDtypeStruct((n, d), F32), jax.ShapeDtypeStruct((b, nkv, t, nb), F32)],
        compiler_params=_cparams(sem, 16 << 20), name="nsa_cmp_attn",
    )(q16, kc, vc, bias_c, gk)
    assert tq >= MAX_DIST
    r = np.arange(tq)
    tz = _dist_bias(table, np.stack([dd * tq + r[:, None] - r[None, :] for dd in range(3)]))
    o2 = pl.pallas_call(
        functools.partial(_nsa_slc_kernel, tq=tq, nb=nb, grp=grp, scale=scale),
        grid=(b, nkv, nq),
        in_specs=[qspec,
                  pl.BlockSpec((t, HEAD_DIM), lambda bi, h, i: (bi, 2 * nkv + h)),
                  pl.BlockSpec((t, HEAD_DIM), lambda bi, h, i: (bi, 3 * nkv + h)),
                  pl.BlockSpec((None, None, tq, nb), lambda bi, h, i: (bi, h, i, 0)),
                  pl.BlockSpec((grp, 3, tq, tq), lambda bi, h, i: (h, 0, 0, 0)),
                  gspec, qspec],
        out_specs=qspec,
        out_shape=jax.ShapeDtypeStruct((n, d), F32),
        compiler_params=_cparams(sem, 32 << 20), name="nsa_slc_attn",
    )(q16, rows16, rows16, sel, tz, gk, o1)
    tq = min(128, t)
    nq = t // tq
    r = np.arange(tq)
    qspec = pl.BlockSpec((tq, gw), lambda bi, h, i: (bi * nq + i, h))
    gspec = pl.BlockSpec((None, tq, 3 * grp), lambda bi, h, i: (h, bi * nq + i, 0))
    span = WINDOW + tq
    wz = _dist_bias(table, r[:, None] + WINDOW - np.arange(span)[None, :])
    kvw = nkv * HEAD_DIM
    kwp = jnp.pad(wrows16.reshape(b, t, 2 * kvw), ((0, 0), (WINDOW, 0), (0, 0))).reshape(b * (t + WINDOW), 2 * kvw)
    og = pl.pallas_call(
        functools.partial(_nsa_win_kernel, tq=tq, grp=grp, scale=scale, window=WINDOW),
        grid=(b, nkv, nq),
        in_specs=[qspec,
                  pl.BlockSpec((t + WINDOW, HEAD_DIM), lambda bi, h, i: (bi, h)),
                  pl.BlockSpec((t + WINDOW, HEAD_DIM), lambda bi, h, i: (bi, nkv + h)),
                  pl.BlockSpec((grp, tq, span), lambda bi, h, i: (h, 0, 0)),
                  gspec, qspec],
        out_specs=qspec,
        out_shape=jax.ShapeDtypeStruct((n, d), BF16),
        compiler_params=_cparams(sem, 24 << 20), name="nsa_win_attn",
    )(q16, kwp, kwp, wz, gk, o2)
    return og


DEC_ROWS = 16
DEC_BLOCKS = 256


def _nsa_dec_compress_kernel(pt_ref, page_ref, new_ref, wk_ref, wv_ref, posk_ref, posv_ref, w1_ref, w2_ref, kn_ref,
                             kc_ref, vc_ref, ak_s, av_s, *, nkv, n_pages, page):
    p = pl.program_id(1)
    hp = lax.Precision.HIGHEST
    kvw = nkv * HEAD_DIM

    @pl.when(p == 0)
    def _():
        ak_s[...] = jnp.zeros(ak_s.shape, F32)
        av_s[...] = jnp.zeros(av_s.shape, F32)

    last = p == n_pages
    n_iota = lax.broadcasted_iota(jnp.int32, (DEC_BLOCKS, page), 0)
    tok_blk = lax.broadcasted_iota(jnp.int32, (DEC_BLOCKS, page), 1) // CMP_BLK
    hit = n_iota == (page // CMP_BLK) * p + tok_blk
    pk = jnp.where(hit, wk_ref[...], 0.0)
    pv = jnp.where(hit, wv_ref[...], 0.0)
    per_tok = 4 * nkv
    for h in range(nkv):
        kr = pl.ds(h, page, stride=per_tok)
        vr = pl.ds(nkv + h, page, stride=per_tok)
        xk = jnp.where(last, new_ref[kr, :], page_ref[kr, :]) + posk_ref[...]
        xv = jnp.where(last, new_ref[vr, :], page_ref[vr, :]) + posv_ref[...]
        cs = slice(h * HEAD_DIM, (h + 1) * HEAD_DIM)
        ak_s[:, cs] += jnp.dot(pk, xk, preferred_element_type=F32, precision=hp)
        av_s[:, cs] += jnp.dot(pv, xv, preferred_element_type=F32, precision=hp)

    @pl.when(p == n_pages)
    def _():
        _cmp_mlp((ak_s[...], av_s[...]), w1_ref, w2_ref, kn_ref, kc_ref, vc_ref, nkv)


def _nsa_dec_cmp_kernel(q_ref, kc_ref, vc_ref, bias_ref, g_ref, o_ref, sel_ref, *, grp, qpos, n_sel, scale):
    q = q_ref[...]
    rows = q.shape[0]
    s = lax.dot_general(q, kc_ref[...].astype(BF16), (((1,), (1,)), ((), ())), preferred_element_type=F32) * scale
    s = s + bias_ref[...]
    blk = lax.broadcasted_iota(jnp.int32, s.shape, 1)
    vis = qpos >= blk * CMP_BLK + (CMP_BLK - 1)
    s = jnp.where(vis, s, NEG)
    p = jnp.exp(s - jnp.max(s, axis=-1, keepdims=True))
    p = p / jnp.sum(p, axis=-1, keepdims=True) * jnp.where(vis, 1.0, 0.0)
    o = jnp.dot(p.astype(BF16), vc_ref[...].astype(BF16), preferred_element_type=F32)
    o_ref[...] = g_ref[...][:, 0:1] * o
    head_row = lax.broadcasted_iota(jnp.int32, s.shape, 0) < grp
    imp = jnp.broadcast_to(jnp.sum(jnp.where(head_row, p, 0.0), axis=0, keepdims=True), s.shape)
    cur = qpos // SEL_BLK
    forced = jnp.logical_or(blk == 0, jnp.logical_or(blk == cur, blk == cur - 1))
    score = jnp.where(blk <= cur, jnp.where(forced, FORCE, imp), -1.0)
    rank = jnp.zeros(s.shape, F32)
    for n in range(cur + 1):
        col = _lane_column(score, blk, n)
        beats = jnp.logical_or(col > score, jnp.logical_and(col == score, blk > n))
        rank = rank + jnp.where(beats, 1.0, 0.0)
    sel_ref[...] = jnp.where(jnp.logical_and(rank < n_sel, score >= 0.0), 1.0, 0.0)


def _nsa_dec_slc_kernel(pt_ref, q_ref, page_ref, sel_ref, blast_ref, bfar_ref, b0_ref, kn_ref, vn_ref, g_ref,
                        oin_ref, o_ref, m_s, l_s, acc_s, *, nkv, n_pages, page, scale):
    p = pl.program_id(1)

    @pl.when(p == 0)
    def _():
        m_s[...] = jnp.full(m_s.shape, NEG, F32)
        l_s[...] = jnp.zeros(l_s.shape, F32)
        acc_s[...] = jnp.zeros(acc_s.shape, F32)

    n_iota = lax.broadcasted_iota(jnp.int32, (DEC_BLOCKS, page), 0)
    tok_blk = lax.broadcasted_iota(jnp.int32, (DEC_BLOCKS, page), 1) // SEL_BLK
    e = jnp.where(n_iota == (page // SEL_BLK) * p + tok_blk, 1.0, 0.0).astype(BF16)
    per_tok = 4 * nkv
    for h in range(nkv):
        q = q_ref[h]
        k = page_ref[pl.ds(2 * nkv + h, page, stride=per_tok), :].astype(BF16)
        v = page_ref[pl.ds(3 * nkv + h, page, stride=per_tok), :].astype(BF16)
        chosen = jnp.dot(sel_ref[h].astype(BF16), e, preferred_element_type=F32) > 0.5
        s = lax.dot_general(q, k, (((1,), (1,)), ((), ())), preferred_element_type=F32) * scale
        s = s + jnp.where(p == n_pages - 1, blast_ref[h], bfar_ref[h])
        s = jnp.where(chosen, s, NEG)
        m_old = m_s[h]
        m_new = jnp.maximum(m_old, jnp.max(s, axis=-1, keepdims=True))
        alpha = jnp.exp(m_old - m_new)
        pm = jnp.exp(s - m_new)
        l_s[h] = alpha * l_s[h] + jnp.sum(pm, axis=-1, keepdims=True)
        acc_s[h] = alpha * acc_s[h] + jnp.dot(pm.astype(BF16), v, preferred_element_type=F32)
        m_s[h] = m_new

    @pl.when(p == n_pages - 1)
    def _():
        nblk = n_pages * (page // SEL_BLK)
        for h in range(nkv):
            sel = sel_ref[h]
            s_n = jnp.sum(q_ref[h].astype(F32) * kn_ref[h], axis=-1, keepdims=True) * scale + b0_ref[h]
            sel_new = _lane_column(sel, lax.broadcasted_iota(jnp.int32, sel.shape, 1), nblk)
            s_n = jnp.where(sel_new > 0.5, s_n, NEG)
            m_f = jnp.maximum(m_s[h], s_n)
            a_f = jnp.exp(m_s[h] - m_f)
            p_n = jnp.exp(s_n - m_f)
            l_f = a_f * l_s[h] + p_n
            acc = a_f * acc_s[h] + p_n.astype(BF16).astype(F32) * vn_ref[h]
            o_ref[h] = oin_ref[h] + g_ref[h][:, 1:2] * (acc / l_f)


def _nsa_dec_win_kernel(q_ref, k_ref, v_ref, bias_ref, b0_ref, kn_ref, vn_ref, g_ref, oin_ref, o_ref, *, scale):
    q = q_ref[...]
    s = lax.dot_general(q, k_ref[...].astype(BF16), (((1,), (1,)), ((), ())), preferred_element_type=F32) * scale
    s = s + bias_ref[...]
    s_n = jnp.sum(q.astype(F32) * kn_ref[...], axis=-1, keepdims=True) * scale + b0_ref[...]
    m = jnp.maximum(jnp.max(s, axis=-1, keepdims=True), s_n)
    pm = jnp.exp(s - m)
    p_n = jnp.exp(s_n - m)
    l = jnp.sum(pm, axis=-1, keepdims=True) + p_n
    o = jnp.dot(pm.astype(BF16), v_ref[...].astype(BF16), preferred_element_type=F32)
    o = (o + p_n.astype(BF16).astype(F32) * vn_ref[...]) / l
    o_ref[...] = oin_ref[...] + g_ref[...][:, 2:3] * o


def nsa_decode(q16, gates, rows_new, wrows_new, cache, layer, win_buf, page_table, cmp_pos, cmp_w, cmp_w1, cmp_w2, kn_cmp, table):
    bq, d = q16.shape
    nkv = cache.shape[-2]
    nh = d // HEAD_DIM
    grp = nh // nkv
    kvw = nkv * HEAD_DIM
    n_phys, page = cache.shape[1], cache.shape[2]
    n_pages = page_table.shape[1]
    p_len = n_pages * page
    lb = win_buf.shape[1]
    assert grp <= DEC_ROWS and p_len // CMP_BLK + 1 <= DEC_BLOCKS and page % CMP_BLK == 0 and lb <= min(WINDOW, p_len)
    assert page >= MAX_DIST
    scale = HEAD_DIM ** -0.5
    pt = page_table.reshape(-1).astype(jnp.int32)
    per_tok = 4 * nkv
    prow = page * per_tok
    pool2 = cache.reshape(-1, HEAD_DIM)
    page_spec2 = lambda clamp: pl.BlockSpec(
        (prow, HEAD_DIM), lambda b, p, t_: (layer * n_phys + t_[b * n_pages + (jnp.minimum(p, n_pages - 1) if clamp else p)], 0))
    rpad = ((0, 0), (0, 0), (0, DEC_ROWS - grp), (0, 0))
    qg = jnp.pad(q16.reshape(bq, nkv, grp, HEAD_DIM), rpad)
    gg = jnp.pad(jnp.transpose(gates.reshape(bq, 3, nkv, grp), (0, 2, 3, 1)), rpad)

    def head_rows(bias):
        return jnp.pad(bias.reshape(nkv, grp, -1), ((0, 0), (0, DEC_ROWS - grp), (0, 0)))

    reps = page // CMP_BLK
    newpage = jnp.zeros((bq, prow, HEAD_DIM), F32).at[:, :per_tok].set(rows_new.reshape(bq, per_tok, HEAD_DIM))
    cspec = lambda shape: pl.BlockSpec(shape, lambda b, p, t_: (0,) * len(shape))
    kc, vc = pl.pallas_call(
        functools.partial(_nsa_dec_compress_kernel, nkv=nkv, n_pages=n_pages, page=page),
        grid_spec=pltpu.PrefetchScalarGridSpec(
            num_scalar_prefetch=1, grid=(bq, n_pages + 1),
            in_specs=[page_spec2(True),
                      pl.BlockSpec((None, prow, HEAD_DIM), lambda b, p, t_: (b, 0, 0)),
                      cspec((1, page)), cspec((1, page)), cspec((page, HEAD_DIM)), cspec((page, HEAD_DIM)),
                      cspec((2, HEAD_DIM, HEAD_DIM)), cspec((2, HEAD_DIM, HEAD_DIM)), cspec((1, HEAD_DIM))],
            out_specs=[pl.BlockSpec((None, DEC_BLOCKS, kvw), lambda b, p, t_: (b, 0, 0))] * 2,
            scratch_shapes=[pltpu.VMEM((DEC_BLOCKS, kvw), F32)] * 2),
        out_shape=[jax.ShapeDtypeStruct((bq, DEC_BLOCKS, kvw), F32)] * 2,
        compiler_params=_cparams(("parallel", "arbitrary"), 16 << 20), name="nsa_dec_compress",
    )(pt, pool2, newpage, jnp.tile(cmp_w[0].astype(F32), reps)[None], jnp.tile(cmp_w[1].astype(F32), reps)[None],
      jnp.tile(cmp_pos[0].astype(F32), (reps, 1)), jnp.tile(cmp_pos[1].astype(F32), (reps, 1)),
      cmp_w1.astype(F32), cmp_w2.astype(F32), kn_cmp.reshape(1, HEAD_DIM).astype(F32))
    blk_end = np.arange(DEC_BLOCKS) * CMP_BLK + CMP_BLK - 1
    bias_c = head_rows(_dist_bias(table, p_len - blk_end))
    hspec = lambda n: pl.BlockSpec((None, None, DEC_ROWS, n), lambda b, h: (b, h, 0, 0))
    bspec = lambda n: pl.BlockSpec((None, DEC_ROWS, n), lambda b, h: (h, 0, 0))
    o1, sel = pl.pallas_call(
        functools.partial(_nsa_dec_cmp_kernel, grp=grp, qpos=p_len, n_sel=min(N_SEL, p_len // CMP_BLK + 1), scale=scale),
        grid=(bq, nkv),
        in_specs=[hspec(HEAD_DIM), pl.BlockSpec((None, DEC_BLOCKS, HEAD_DIM), lambda b, h: (b, 0, h)),
                  pl.BlockSpec((None, DEC_BLOCKS, HEAD_DIM), lambda b, h: (b, 0, h)), bspec(DEC_BLOCKS), hspec(3)],
        out_specs=[hspec(HEAD_DIM), hspec(DEC_BLOCKS)],
        out_shape=[jax.ShapeDtypeStruct((bq, nkv, DEC_ROWS, HEAD_DIM), F32),
                   jax.ShapeDtypeStruct((bq, nkv, DEC_ROWS, DEC_BLOCKS), F32)],
        compiler_params=_cparams(("parallel", "parallel"), 8 << 20), name="nsa_dec_cmp",
    )(qg, kc, vc, bias_c, gg)
    blast = head_rows(_dist_bias(table, page - np.arange(page)))
    bfar = head_rows(_dist_bias(table, np.array([MAX_DIST])))
    b0 = head_rows(_dist_bias(table, np.array([0])))
    hspec3 = lambda r, n: pl.BlockSpec((None, nkv, r, n), lambda b, p, t_: (b, 0, 0, 0))
    bspec3 = lambda n: pl.BlockSpec((nkv, DEC_ROWS, n), lambda b, p, t_: (0, 0, 0))
    ksn = rows_new[:, 2 * kvw:3 * kvw].reshape(bq, nkv, 1, HEAD_DIM)
    vsn = rows_new[:, 3 * kvw:].reshape(bq, nkv, 1, HEAD_DIM)
    o2 = pl.pallas_call(
        functools.partial(_nsa_dec_slc_kernel, nkv=nkv, n_pages=n_pages, page=page, scale=scale),
        grid_spec=pltpu.PrefetchScalarGridSpec(
            num_scalar_prefetch=1, grid=(bq, n_pages),
            in_specs=[hspec3(DEC_ROWS, HEAD_DIM), page_spec2(False), hspec3(DEC_ROWS, DEC_BLOCKS),
                      bspec3(page), bspec3(1), bspec3(1), hspec3(1, HEAD_DIM), hspec3(1, HEAD_DIM),
                      hspec3(DEC_ROWS, 3), hspec3(DEC_ROWS, HEAD_DIM)],
            out_specs=hspec3(DEC_ROWS, HEAD_DIM),
            scratch_shapes=[pltpu.VMEM((nkv, DEC_ROWS, 1), F32), pltpu.VMEM((nkv, DEC_ROWS, 1), F32),
                            pltpu.VMEM((nkv, DEC_ROWS, HEAD_DIM), F32)]),
        out_shape=jax.ShapeDtypeStruct((bq, nkv, DEC_ROWS, HEAD_DIM), F32),
        compiler_params=_cparams(("parallel", "arbitrary"), 8 << 20), name="nsa_dec_slc",
    )(pt, qg, pool2, sel, blast, bfar, b0, ksn, vsn, gg, o1)
    kpos_ok = np.arange(lb) >= lb - p_len
    assert kpos_ok.all()
    bias_w = head_rows(_dist_bias(table, lb - np.arange(lb)))
    win3 = win_buf.reshape(bq, lb, 2 * kvw)
    nspec2 = pl.BlockSpec((None, None, 1, HEAD_DIM), lambda b, h: (b, h, 0, 0))
    kwn = wrows_new[:, :kvw].reshape(bq, nkv, 1, HEAD_DIM)
    vwn = wrows_new[:, kvw:].reshape(bq, nkv, 1, HEAD_DIM)
    o3 = pl.pallas_call(
        functools.partial(_nsa_dec_win_kernel, scale=scale),
        grid=(bq, nkv),
        in_specs=[hspec(HEAD_DIM), pl.BlockSpec((None, lb, HEAD_DIM), lambda b, h: (b, 0, h)),
                  pl.BlockSpec((None, lb, HEAD_DIM), lambda b, h: (b, 0, nkv + h)), bspec(lb), bspec(1),
                  nspec2, nspec2, hspec(3), hspec(HEAD_DIM)],
        out_specs=hspec(HEAD_DIM),
        out_shape=jax.ShapeDtypeStruct((bq, nkv, DEC_ROWS, HEAD_DIM), F32),
        compiler_params=_cparams(("parallel", "parallel"), 8 << 20), name="nsa_dec_win",
    )(qg, win3, win3, bias_w, b0, kwn, vwn, gg, o2)
    return o3[:, :, :grp].reshape(bq, d).astype(BF16)


def nsa_layer(xp, xs, hp, hs, b, t, n_s, cache, layer, win_buf, page_table, w_in, qn, kn, cmp_pos, cmp_w, cmp_w1, cmp_w2, w_out, table):
    n_p, d = hp.shape
    nh = d // HEAD_DIM
    nkv = cache.shape[-2]
    tmp = _row_tile(n_p, 1024)
    q16, gates, rows, wrows, rows16, wrows16 = _nsa_proj(hp, w_in, qn, kn, tmp, nkv)
    rows5 = rows.reshape(b, t, 4, nkv, HEAD_DIM)
    wrows5 = wrows.reshape(b, t, 2, nkv, HEAD_DIM)
    kc, vc = nsa_compress(rows, b, t, nkv, cmp_pos, cmp_w, cmp_w1, cmp_w2, kn[0])
    og = nsa_attention(q16, rows16, wrows16, gates, kc, vc, table, b, t, nkv)
    xp = matmul(og, [(w_out, 0)], [(d, F32, 0)], _resid_epi, tm=tmp, tn=512, nj=d // 512, tiles=[(xp, 0)], name="nsa_out")[0]
    lbw = min(WINDOW, t)
    out_p = (rows5[None], wrows5[:, t - lbw:][None])
    sr = hs.shape[0]
    q16, gates, rows, wrows, _, _ = _nsa_proj(hs, w_in, qn, kn, sr, nkv)
    rows_new = rows[:n_s].reshape(n_s, 1, 4, nkv, HEAD_DIM)
    wrows_new = wrows[:n_s].reshape(n_s, 1, 2, nkv, HEAD_DIM)
    lb_ = win_buf.shape[1]
    kw = jnp.concatenate([win_buf, wrows_new], axis=1)
    ogs = nsa_decode(q16[:n_s], gates[:n_s], rows[:n_s], wrows[:n_s], cache, layer, win_buf, page_table,
                     cmp_pos, cmp_w, cmp_w1, cmp_w2, kn[0], table)
    ogs = jnp.pad(ogs, ((0, sr - n_s), (0, 0)))
    xs = matmul(ogs, [(w_out, 0)], [(d, F32, 0)], _resid_epi, tm=sr, tn=512, nj=d // 512, tiles=[(xs, 0)], name="nsa_out")[0]
    out_s = (rows_new[None], kw[:, -lb_:][None])
    return xp, xs, out_p + out_s


def kernel(x_prompt, x_sample, cache_fox_kv, cache_fox_logf, state_hgrn, state_s5, cache_nsa_kv, state_nsa_win, page_table, norm_mix, norm_ffn, fox_w_in, fox_b_f, fox_q_norm, fox_k_norm, fox_w_out, hg_w_in, hg_lb, hg_o_norm, hg_w_out, s5_a_re, s5_a_im, s5_log_dt, s5_b_re, s5_b_im, s5_c_re, s5_c_im, s5_d, s5_w_glu, nsa_w_in, nsa_q_norm, nsa_k_norm, nsa_cmp_pos, nsa_cmp_w, nsa_cmp_w1, nsa_cmp_w2, nsa_w_out, rel_bias, ffn_w1, ffn_w3, ffn_w2, moe_router, moe_router_b, moe_w1, moe_w3, moe_w2):
    b, t, d = x_prompt.shape
    n_s = x_sample.shape[0]
    assert x_sample.shape[1] == 1
    depth = norm_mix.shape[0]
    xp = x_prompt.reshape(b * t, d)
    xs = jnp.pad(x_sample.reshape(n_s, d), ((0, SAMPLE_ROWS - n_s), (0, 0)))
    sm = jax.nn.softmax(hg_lb.astype(F32), axis=0)
    lower_bounds = jnp.cumsum(sm, axis=0) - sm[0]
    res = {}
    for i in range(depth):
        j = i // 4
        kind = i % 4
        if kind == 2:
            hp = rms_norm(xp, norm_mix[i], F32)
            hs = rms_norm(xs, norm_mix[i], F32)
        else:
            hp = rms_norm(xp, norm_mix[i])
            hs = rms_norm(xs, norm_mix[i])
        if kind == 0:
            xp, xs, o = fox_layer(xp, xs, hp, hs, b, t, n_s, cache_fox_kv, j, cache_fox_logf[j], page_table,
                                  fox_w_in[j], fox_b_f[j], fox_q_norm[j], fox_k_norm[j], fox_w_out[j])
        elif kind == 1:
            xp, xs, o = hgrn_layer(xp, xs, hp, hs, b, t, n_s, state_hgrn[j], hg_w_in[j], lower_bounds[i],
                                   hg_o_norm[j], hg_w_out[j])
        elif kind == 2:
            xp, xs, o = s5_layer(xp, xs, hp, hs, b, t, n_s, state_s5[j], s5_a_re[j], s5_a_im[j], s5_log_dt[j],
                                 s5_b_re[j], s5_b_im[j], s5_c_re[j], s5_c_im[j], s5_d[j], s5_w_glu[j])
        else:
            xp, xs, o = nsa_layer(xp, xs, hp, hs, b, t, n_s, cache_nsa_kv, j, state_nsa_win[j], page_table,
                                  nsa_w_in[j], nsa_q_norm[j], nsa_k_norm[j], nsa_cmp_pos[j], nsa_cmp_w[j],
                                  nsa_cmp_w1[j], nsa_cmp_w2[j], nsa_w_out[j], rel_bias)
        res[kind] = o
        hp = rms_norm(xp, norm_ffn[i])
        hs = rms_norm(xs, norm_ffn[i])
        f = i // 2
        if i % 2 == 0:
            xp, xs = dense_ffn(xp, xs, hp, hs, ffn_w1, ffn_w3, ffn_w2, f)
        else:
            xp, xs = moe_ffn(xp, xs, hp, hs, n_s, moe_router[f], moe_router_b[f], moe_w1, moe_w3, moe_w2, f)
    return (xp.reshape(b, t, d), xs[:n_s].reshape(n_s, 1, d)) + res[0] + res[1] + res[2] + res[3]
```
